```python
import math
import jax, jax.numpy as jnp
from jax import lax
import numpy as np

D_MODEL = 1024
BATCH = 8
SEQ = 2048
DEPTH = 1
DEC_BATCH = 128
DEC_SEQ = 8
PAST_LEN = 16384
PAGE_SIZE = 128

H_A = 4
DK_A = D_MODEL // 16
DV_A = D_MODEL // 8
GATE_RANK = 16
GATE_NORM = 16.0
H_B = 4
DK_B = D_MODEL // 8
DV_B = D_MODEL // 8
D_FF = -(-8 * D_MODEL // (3 * 256)) * 256
CHUNK = 32
EPS = 1e-6
SPLIT_SIZES = (H_A * DK_A, H_A * DK_A, H_A * DV_A, H_A * DV_A, GATE_RANK,
               H_B * DK_B, H_B * DK_B, H_B * DV_B, H_B * DV_B)
D_IN = sum(SPLIT_SIZES)

kernel_name = "hymba_gla_hgrn2_step"


def rmsnorm(x, g):
    xf = x.astype(jnp.float32)
    y = xf * lax.rsqrt(jnp.mean(xf * xf, axis=-1, keepdims=True) + EPS) * g.astype(jnp.float32)
    return y.astype(x.dtype)


def head_rmsnorm(o, g):
    return o * lax.rsqrt(jnp.mean(o * o, axis=-1, keepdims=True) + EPS) * g.astype(jnp.float32)


def chunk_gated_linear(q, k, v, log_a, s0):
    B, T, H, K = q.shape
    n = -(-T // CHUNK)
    pad = n * CHUNK - T
    f32 = jnp.float32
    q, k, v, log_a = (t.astype(f32) for t in (q, k, v, log_a))
    if pad:
        pw = ((0, 0), (0, pad), (0, 0), (0, 0))
        q, k, v, log_a = (jnp.pad(t, pw) for t in (q, k, v, log_a))

    def to_chunks(t):
        return t.reshape(B, n, CHUNK, H, t.shape[-1]).transpose(1, 0, 3, 2, 4)

    qc, kc, vc = to_chunks(q), to_chunks(k), to_chunks(v)
    bc = jnp.cumsum(to_chunks(log_a), axis=3)
    mask = jnp.tril(jnp.ones((CHUNK, CHUNK), dtype=bool))

    def step(S, xs):
        qi, ki, vi, bi = xs
        q_in = qi * jnp.exp(bi)
        k_in = ki * jnp.exp(-bi)
        att = jnp.where(mask, jnp.einsum('bhtk,bhsk->bhts', q_in, k_in), 0.0)
        o = jnp.einsum('bhtk,bhkv->bhtv', q_in, S) + jnp.einsum('bhts,bhsv->bhtv', att, vi)
        b_last = bi[:, :, -1:, :]
        S = jnp.exp(b_last[:, :, 0, :])[..., None] * S + jnp.einsum(
            'bhsk,bhsv->bhkv', ki * jnp.exp(b_last - bi), vi)
        return S, o

    S, o = lax.scan(step, s0.astype(f32), (qc, kc, vc, bc))
    o = o.transpose(1, 0, 3, 2, 4).reshape(B, n * CHUNK, H, -1)[:, :T]
    return o, S


def mixer(h, s_gla, s_hgrn, w_in, w_a2, b_a, lb, onorm_a, onorm_b, w_o):
    B, T, _ = h.shape
    idx, acc = [], 0
    for s in SPLIT_SIZES[:-1]:
        acc += s
        idx.append(acc)
    p = h @ w_in
    qa, ka, va, ga, lra, qb, fb, ib, gb = jnp.split(p.astype(jnp.float32), idx, axis=-1)

    qa = qa.reshape(B, T, H_A, DK_A) * (DK_A ** -0.5)
    ka = ka.reshape(B, T, H_A, DK_A)
    va = va.reshape(B, T, H_A, DV_A)
    log_alpha = jax.nn.log_sigmoid(lra @ w_a2.astype(jnp.float32) + b_a.astype(jnp.float32)) / GATE_NORM
    log_alpha = log_alpha.reshape(B, T, H_A, DK_A)
    oa, sa = chunk_gated_linear(qa, ka, va, log_alpha, s_gla)
    oa = head_rmsnorm(oa, onorm_a) * jax.nn.silu(ga.reshape(B, T, H_A, DV_A))

    qb = jax.nn.silu(qb.reshape(B, T, H_B, DK_B))
    f = lb + (1.0 - lb) * jax.nn.sigmoid(fb.reshape(B, T, H_B, DK_B))
    kb = 1.0 - f
    ib = ib.reshape(B, T, H_B, DV_B)
    ob, sb = chunk_gated_linear(qb, kb, ib, jnp.log(f), s_hgrn)
    ob = head_rmsnorm(ob, onorm_b) * jax.nn.silu(gb.reshape(B, T, H_B, DV_B))

    o = jnp.concatenate([oa.reshape(B, T, H_A * DV_A), ob.reshape(B, T, H_B * DV_B)], axis=-1)
    return o.astype(h.dtype) @ w_o, sa, sb


def swiglu(h, w1, w3, w2):
    return (jax.nn.silu(h @ w1) * (h @ w3)) @ w2


def trunk(x, s_gla, s_hgrn, norm_mix, w_in, w_a2, b_a, lb_all, onorm_a, onorm_b, w_o,
          norm_ffn, w1, w3, w2, norm_final):
    sa_list, sb_list = [], []
    for l in range(DEPTH):
        lb = lb_all[l].reshape(H_B, DK_B)
        m, sa, sb = mixer(rmsnorm(x, norm_mix[l]), s_gla[l], s_hgrn[l], w_in[l], w_a2[l], b_a[l],
                          lb, onorm_a[l], onorm_b[l], w_o[l])
        x = x + m
        x = x + swiglu(rmsnorm(x, norm_ffn[l]), w1[l], w3[l], w2[l])
        sa_list.append(sa)
        sb_list.append(sb)
    return rmsnorm(x, norm_final), jnp.stack(sa_list), jnp.stack(sb_list)


def setup_inputs(seed: int = 0) -> dict:
    key = jax.random.key(seed)
    ks = jax.random.split(key, 20)
    f32 = jnp.float32
    nrm = lambda k, shape, s: jax.random.normal(k, shape, f32) * s
    return {
        "x_prompt": nrm(ks[0], (BATCH, SEQ, D_MODEL), 1.0),
        "x_sample": nrm(ks[1], (DEC_BATCH, DEC_SEQ, D_MODEL), 1.0),
        "state_gla": nrm(ks[2], (DEPTH, DEC_BATCH, H_A, DK_A, DV_A), 2.0),
        "state_hgrn": nrm(ks[3], (DEPTH, DEC_BATCH, H_B, DK_B, DV_B), 0.5),
        "norm_mix": 1.0 + nrm(ks[4], (DEPTH, D_MODEL), 0.02),
        "w_in": nrm(ks[5], (DEPTH, D_MODEL, D_IN), D_MODEL ** -0.5),
        "w_a2": nrm(ks[6], (DEPTH, GATE_RANK, H_A * DK_A), GATE_RANK ** -0.5),
        "b_a": nrm(ks[7], (DEPTH, H_A * DK_A), 0.01),
        "lb_param": nrm(ks[8], (DEPTH + 1, H_B * DK_B), 1.0),
        "onorm_a": 1.0 + nrm(ks[9], (DEPTH, DV_A), 0.02),
        "onorm_b": 1.0 + nrm(ks[10], (DEPTH, DV_B), 0.02),
        "w_o": nrm(ks[11], (DEPTH, D_MODEL, D_MODEL), D_MODEL ** -0.5),
        "norm_ffn": 1.0 + nrm(ks[12], (DEPTH, D_MODEL), 0.02),
        "w1": nrm(ks[13], (DEPTH, D_MODEL, D_FF), D_MODEL ** -0.5),
        "w3": nrm(ks[14], (DEPTH, D_MODEL, D_FF), D_MODEL ** -0.5),
        "w2": nrm(ks[15], (DEPTH, D_FF, D_MODEL), D_FF ** -0.5),
        "norm_final": 1.0 + nrm(ks[16], (D_MODEL,), 0.02),
    }


def reference(x_prompt, x_sample, state_gla, state_hgrn, norm_mix, w_in, w_a2, b_a, lb_param,
              onorm_a, onorm_b, w_o, norm_ffn, w1, w3, w2, norm_final):
    lb_all = jnp.cumsum(jax.nn.softmax(lb_param.astype(jnp.float32), axis=0), axis=0)
    zeros_a = jnp.zeros((DEPTH, BATCH, H_A, DK_A, DV_A), jnp.float32)
    zeros_b = jnp.zeros((DEPTH, BATCH, H_B, DK_B, DV_B), jnp.float32)
    weights = (norm_mix, w_in, w_a2, b_a, lb_all, onorm_a, onorm_b, w_o, norm_ffn, w1, w3, w2, norm_final)
    y_prompt, sa_p, sb_p = trunk(x_prompt, zeros_a, zeros_b, *weights)
    y_sample, sa_s, sb_s = trunk(x_sample, state_gla, state_hgrn, *weights)
    return (y_prompt, y_sample, sa_p, sb_p, sa_s, sb_s)
```

```python
import functools

import jax
import jax.numpy as jnp
from jax import lax
from jax.experimental import pallas as pl
from jax.experimental.pallas import tpu as pltpu

D_MODEL = 1024
N_HEADS = 8
H_A = 4
DK_A = 64
HEAD = 128
GATE_RANK = 16
GATE_NORM = 16.0
D_FF = 2816
FF_CHUNK = 256
EPS = 1e-6
PROMPT_CHUNK = 32
SUBLANES = 8

SEC = H_A * HEAD
OFF_QA, OFF_KA, OFF_VA, OFF_GA, OFF_QB, OFF_FB, OFF_IB, OFF_GB, OFF_LR = (
    0, SEC, 2 * SEC, 3 * SEC, 4 * SEC, 5 * SEC, 6 * SEC, 7 * SEC, 8 * SEC)
D_P = 8 * SEC + HEAD

VMEM_LIMIT = 56 * 1024 * 1024

f32 = jnp.float32
bf16 = jnp.bfloat16


def _const_spec(shape):
    nd = len(shape)
    return pl.BlockSpec(shape, lambda *_: (0,) * nd, pipeline_mode=pl.Buffered(1))


def _rmsnorm(x, g):
    return x * lax.rsqrt(jnp.mean(x * x, axis=-1, keepdims=True) + EPS) * g


def _sigmoid(x):
    return 1.0 / (1.0 + jnp.exp(-x))


def _proj_kernel(x_ref, g_ref, w_ref, p_ref):
    h = _rmsnorm(x_ref[...], g_ref[...]).astype(bf16)
    p_ref[...] = jnp.dot(h, w_ref[...], preferred_element_type=f32)


def _proj(x2d, g, w_in_p, tm):
    n = x2d.shape[0]
    return pl.pallas_call(
        _proj_kernel,
        out_shape=jax.ShapeDtypeStruct((n, D_P), f32),
        grid=(n // tm,),
        in_specs=[
            pl.BlockSpec((tm, D_MODEL), lambda i: (i, 0)),
            _const_spec((1, D_MODEL)),
            _const_spec((D_MODEL, D_P)),
        ],
        out_specs=pl.BlockSpec((tm, D_P), lambda i: (i, 0)),
        compiler_params=pltpu.CompilerParams(
            dimension_semantics=("arbitrary",), vmem_limit_bytes=VMEM_LIMIT),
        name="proj",
    )(x2d, g, w_in_p)


def _cumsum_rows(x, chunk):
    rows, width = x.shape
    row = lax.broadcasted_iota(jnp.int32, (SUBLANES, width), 0)
    groups_per_chunk = chunk // SUBLANES
    out = []
    carry = None
    for g in range(rows // SUBLANES):
        y = x[g * SUBLANES:(g + 1) * SUBLANES]
        for s in (1, 2, 4):
            y = y + jnp.where(row >= s, pltpu.roll(y, s, 0), 0.0)
        if g % groups_per_chunk != 0:
            y = y + carry
        carry = y[SUBLANES - 1:SUBLANES]
        out.append(y)
    return out[0] if len(out) == 1 else jnp.concatenate(out, axis=0)


def _lower_bound(lbp_ref):
    lbp = lbp_ref[...]
    m = jnp.max(lbp, axis=0, keepdims=True)
    e = jnp.exp(lbp - m)
    return e[0:1] / jnp.sum(e, axis=0, keepdims=True)


def _dot_nt(a, b):
    return lax.dot_general(a, b, (((1,), (1,)), ((), ())), preferred_element_type=f32)


def _dot_tn(a, b):
    return lax.dot_general(a, b, (((0,), (0,)), ((), ())), preferred_element_type=f32)


def _gates(p, w_a2, b_a, lb, chunk):
    rows = p.shape[0]
    qa = p[:, OFF_QA:OFF_QA + SEC] * (DK_A ** -0.5)
    ka = p[:, OFF_KA:OFF_KA + SEC]
    va = p[:, OFF_VA:OFF_VA + SEC]
    ga = p[:, OFF_GA:OFF_GA + SEC]
    qb = p[:, OFF_QB:OFF_QB + SEC]
    fb = p[:, OFF_FB:OFF_FB + SEC]
    ib = p[:, OFF_IB:OFF_IB + SEC]
    gb = p[:, OFF_GB:OFF_GB + SEC]
    lra = p[:, OFF_LR:OFF_LR + HEAD]

    z = jnp.dot(lra.astype(bf16), w_a2, preferred_element_type=f32) + b_a
    la_a = (jnp.minimum(z, 0.0) - jnp.log1p(jnp.exp(-jnp.abs(z)))) * (1.0 / GATE_NORM)

    qb = qb * _sigmoid(qb)
    f = lb + (1.0 - lb) * _sigmoid(fb)
    kb = 1.0 - f
    la_b = jnp.log(f)

    q = jnp.concatenate([qa, qb], axis=1)
    k = jnp.concatenate([ka, kb], axis=1)
    v = jnp.concatenate([va, ib], axis=1)
    la = jnp.concatenate([la_a, la_b], axis=1)
    g = jnp.concatenate([ga, gb], axis=1)
    g = g * _sigmoid(g)

    b = _cumsum_rows(la, chunk)
    n_chunks = rows // chunk
    if n_chunks == 1:
        b_last = b[rows - 1:rows]
        b_last_rows = b_last
    else:
        lasts = [b[(c + 1) * chunk - 1:(c + 1) * chunk] for c in range(n_chunks)]
        b_last = jnp.concatenate(lasts, axis=0)
        b_last_rows = jnp.concatenate(
            [jnp.broadcast_to(l, (chunk, l.shape[1])) for l in lasts], axis=0)
    q_in = (q * jnp.exp(b)).astype(bf16)
    k_in = (k * jnp.exp(-b)).astype(bf16)
    k_out = (k * jnp.exp(b_last_rows - b)).astype(bf16)
    decay = jnp.exp(b_last)
    return q_in, k_in, k_out, v.astype(bf16), decay, g


def _head_step(q_in, k_in, k_out, v, decay, st, tril):
    att = jnp.where(tril, _dot_nt(q_in, k_in), 0.0)
    o = _dot_nt(q_in, st.astype(bf16)) + jnp.dot(att.astype(bf16), v, preferred_element_type=f32)
    st_new = decay * st + _dot_tn(v, k_out)
    return o, st_new


def _head_norm(o, onorm):
    return o * lax.rsqrt(jnp.mean(o * o, axis=-1, keepdims=True) + EPS) * onorm


def _mixer_prompt_kernel(p_ref, w_a2_ref, b_a_ref, lbp_ref, ona_ref, onb_ref,
                         o_ref, sa_ref, sb_ref, st_ref, *, tile, chunk):
    j = pl.program_id(1)

    @pl.when(j == 0)
    def _():
        st_ref[...] = jnp.zeros_like(st_ref)

    lb = _lower_bound(lbp_ref)
    w_a2 = w_a2_ref[...]
    b_a = b_a_ref[...]
    onorm = [ona_ref[...]] * H_A + [onb_ref[...]] * (N_HEADS - H_A)
    r = lax.broadcasted_iota(jnp.int32, (chunk, chunk), 0)
    c = lax.broadcasted_iota(jnp.int32, (chunk, chunk), 1)
    tril = r >= c

    def body(ci, carry):
        start = pl.multiple_of(ci * chunk, chunk)
        p = p_ref[pl.ds(start, chunk), :]
        q_in, k_in, k_out, v, decay, g = _gates(p, w_a2, b_a, lb, chunk)
        outs = []
        for h in range(N_HEADS):
            sl = slice(h * HEAD, (h + 1) * HEAD)
            o, st_new = _head_step(q_in[:, sl], k_in[:, sl], k_out[:, sl], v[:, sl],
                                   decay[:, sl], st_ref[h], tril)
            st_ref[h] = st_new
            outs.append(_head_norm(o, onorm[h]) * g[:, sl])
        o_ref[pl.ds(start, chunk), :] = jnp.concatenate(outs, axis=1).astype(o_ref.dtype)
        return carry

    lax.fori_loop(0, tile // chunk, body, 0)

    @pl.when(j == pl.num_programs(1) - 1)
    def _():
        for h in range(H_A):
            sa_ref[0, h] = st_ref[h].T[:DK_A, :]
            sb_ref[0, h] = st_ref[H_A + h].T


def _mixer_prompt(p2d, w_a2p, b_ap, lb_param, onorm_a, onorm_b, batch, seq, tile):
    nt = seq // tile
    kern = functools.partial(_mixer_prompt_kernel, tile=tile, chunk=PROMPT_CHUNK)
    return pl.pallas_call(
        kern,
        out_shape=(
            jax.ShapeDtypeStruct((batch * seq, D_MODEL), bf16),
            jax.ShapeDtypeStruct((batch, H_A, DK_A, HEAD), f32),
            jax.ShapeDtypeStruct((batch, H_A, HEAD, HEAD), f32),
        ),
        grid=(batch, nt),
        in_specs=[
            pl.BlockSpec((tile, D_P), lambda b, j: (b * nt + j, 0)),
            _const_spec(w_a2p.shape),
            _const_spec(b_ap.shape),
            _const_spec(lb_param.shape),
            _const_spec(onorm_a.shape),
            _const_spec(onorm_b.shape),
        ],
        out_specs=(
            pl.BlockSpec((tile, D_MODEL), lambda b, j: (b * nt + j, 0)),
            pl.BlockSpec((1, H_A, DK_A, HEAD), lambda b, j: (b, 0, 0, 0)),
            pl.BlockSpec((1, H_A, HEAD, HEAD), lambda b, j: (b, 0, 0, 0)),
        ),
        scratch_shapes=[pltpu.VMEM((N_HEADS, HEAD, HEAD), f32)],
        compiler_params=pltpu.CompilerParams(
            dimension_semantics=("arbitrary", "arbitrary"), vmem_limit_bytes=VMEM_LIMIT),
        name="mixer_prompt",
    )(p2d, w_a2p, b_ap, lb_param, onorm_a, onorm_b)


def _mixer_sample_kernel(p_ref, sa_in_ref, sb_in_ref, w_a2_ref, b_a_ref, lbp_ref, ona_ref, onb_ref,
                         o_ref, sa_ref, sb_ref, *, bt, seq):
    lb = _lower_bound(lbp_ref)
    w_a2 = w_a2_ref[...]
    b_a = b_a_ref[...]
    onorm = [ona_ref[...]] * H_A + [onb_ref[...]] * (N_HEADS - H_A)
    r = lax.broadcasted_iota(jnp.int32, (seq, seq), 0)
    c = lax.broadcasted_iota(jnp.int32, (seq, seq), 1)
    tril = r >= c
    pad = jnp.zeros((HEAD - DK_A, HEAD), f32)

    def body(bi, carry):
        start = pl.multiple_of(bi * seq, seq)
        p = p_ref[pl.ds(start, seq), :]
        q_in, k_in, k_out, v, decay, g = _gates(p, w_a2, b_a, lb, seq)
        outs = []
        for h in range(N_HEADS):
            sl = slice(h * HEAD, (h + 1) * HEAD)
            if h < H_A:
                st = jnp.concatenate([sa_in_ref[bi, h], pad], axis=0).T
            else:
                st = sb_in_ref[bi, h - H_A].T
            o, st_new = _head_step(q_in[:, sl], k_in[:, sl], k_out[:, sl], v[:, sl],
                                   decay[:, sl], st, tril)
            if h < H_A:
                sa_ref[bi, h] = st_new.T[:DK_A, :]
            else:
                sb_ref[bi, h - H_A] = st_new.T
            outs.append(_head_norm(o, onorm[h]) * g[:, sl])
        o_ref[pl.ds(start, seq), :] = jnp.concatenate(outs, axis=1).astype(o_ref.dtype)
        return carry

    lax.fori_loop(0, bt, body, 0)


def _mixer_sample(p2d, sa_in, sb_in, w_a2p, b_ap, lb_param, onorm_a, onorm_b, batch, seq, bt):
    kern = functools.partial(_mixer_sample_kernel, bt=bt, seq=seq)
    return pl.pallas_call(
        kern,
        out_shape=(
            jax.ShapeDtypeStruct((batch * seq, D_MODEL), f32),
            jax.ShapeDtypeStruct((batch, H_A, DK_A, HEAD), f32),
            jax.ShapeDtypeStruct((batch, H_A, HEAD, HEAD), f32),
        ),
        grid=(batch // bt,),
        in_specs=[
            pl.BlockSpec((bt * seq, D_P), lambda i: (i, 0)),
            pl.BlockSpec((bt, H_A, DK_A, HEAD), lambda i: (i, 0, 0, 0)),
            pl.BlockSpec((bt, H_A, HEAD, HEAD), lambda i: (i, 0, 0, 0)),
            _const_spec(w_a2p.shape),
            _const_spec(b_ap.shape),
            _const_spec(lb_param.shape),
            _const_spec(onorm_a.shape),
            _const_spec(onorm_b.shape),
        ],
        out_specs=(
            pl.BlockSpec((bt * seq, D_MODEL), lambda i: (i, 0)),
            pl.BlockSpec((bt, H_A, DK_A, HEAD), lambda i: (i, 0, 0, 0)),
            pl.BlockSpec((bt, H_A, HEAD, HEAD), lambda i: (i, 0, 0, 0)),
        ),
        compiler_params=pltpu.CompilerParams(
            dimension_semantics=("arbitrary",), vmem_limit_bytes=VMEM_LIMIT),
        name="mixer_sample",
    )(p2d, sa_in, sb_in, w_a2p, b_ap, lb_param, onorm_a, onorm_b)


def _out_kernel(o_ref, x_ref, w_o_ref, gf_ref, w1_ref, w3_ref, w2_ref, gl_ref, y_ref):
    x1 = x_ref[...] + jnp.dot(o_ref[...].astype(bf16), w_o_ref[...], preferred_element_type=f32)
    h = _rmsnorm(x1, gf_ref[...]).astype(bf16)
    acc = x1
    for c in range(D_FF // FF_CHUNK):
        cols = slice(c * FF_CHUNK, (c + 1) * FF_CHUNK)
        a = jnp.dot(h, w1_ref[:, cols], preferred_element_type=f32)
        b = jnp.dot(h, w3_ref[:, cols], preferred_element_type=f32)
        u = (a * _sigmoid(a) * b).astype(bf16)
        acc = acc + jnp.dot(u, w2_ref[cols, :], preferred_element_type=f32)
    y_ref[...] = _rmsnorm(acc, gl_ref[...])


def _out(o2d, x2d, w_o, g_ffn, w1, w3, w2, g_final, tm):
    n = x2d.shape[0]
    return pl.pallas_call(
        _out_kernel,
        out_shape=jax.ShapeDtypeStruct((n, D_MODEL), f32),
        grid=(n // tm,),
        in_specs=[
            pl.BlockSpec((tm, D_MODEL), lambda i: (i, 0)),
            pl.BlockSpec((tm, D_MODEL), lambda i: (i, 0)),
            _const_spec((D_MODEL, D_MODEL)),
            _const_spec((1, D_MODEL)),
            _const_spec((D_MODEL, D_FF)),
            _const_spec((D_MODEL, D_FF)),
            _const_spec((D_FF, D_MODEL)),
            _const_spec((1, D_MODEL)),
        ],
        out_specs=pl.BlockSpec((tm, D_MODEL), lambda i: (i, 0)),
        compiler_params=pltpu.CompilerParams(
            dimension_semantics=("arbitrary",), vmem_limit_bytes=VMEM_LIMIT),
        name="out_ffn",
    )(o2d, x2d, w_o, g_ffn, w1, w3, w2, g_final)


def _pad_heads(w):
    lead = w.shape[:-1]
    w = w.reshape(lead + (H_A, DK_A))
    w = jnp.pad(w, [(0, 0)] * len(lead) + [(0, 0), (0, HEAD - DK_A)])
    return w.reshape(lead + (H_A * HEAD,))


def _layout_w_in(w_in):
    sizes = (H_A * DK_A, H_A * DK_A, SEC, SEC, GATE_RANK, SEC, SEC, SEC, SEC)
    parts, acc = [], 0
    for s in sizes:
        parts.append(w_in[:, acc:acc + s])
        acc += s
    qa, ka, va, ga, lra, qb, fb, ib, gb = parts
    lra = jnp.pad(lra, ((0, 0), (0, HEAD - GATE_RANK)))
    return jnp.concatenate([_pad_heads(qa), _pad_heads(ka), va, ga, qb, fb, ib, gb, lra], axis=1)


def kernel(x_prompt, x_sample, state_gla, state_hgrn, norm_mix, w_in, w_a2, b_a, lb_param,
           onorm_a, onorm_b, w_o, norm_ffn, w1, w3, w2, norm_final):
    batch, seq, _ = x_prompt.shape
    dec_batch, dec_seq, _ = x_sample.shape

    w_in_p = _layout_w_in(w_in[0]).astype(bf16)
    w_a2p = jnp.pad(_pad_heads(w_a2[0]), ((0, HEAD - GATE_RANK), (0, 0))).astype(bf16)
    b_ap = _pad_heads(b_a[0]).reshape(1, SEC)
    g_mix = norm_mix[0].reshape(1, D_MODEL)
    g_ffn = norm_ffn[0].reshape(1, D_MODEL)
    g_fin = norm_final.reshape(1, D_MODEL)
    ona = onorm_a[0].reshape(1, HEAD)
    onb = onorm_b[0].reshape(1, HEAD)
    w_o_b = w_o[0].astype(bf16)
    w1_b = w1[0].astype(bf16)
    w3_b = w3[0].astype(bf16)
    w2_b = w2[0].astype(bf16)

    xp = x_prompt.reshape(batch * seq, D_MODEL)
    xs = x_sample.reshape(dec_batch * dec_seq, D_MODEL)

    pp = _proj(xp, g_mix, w_in_p, tm=256)
    op, sa_p, sb_p = _mixer_prompt(pp, w_a2p, b_ap, lb_param, ona, onb, batch, seq, tile=256)
    yp = _out(op, xp, w_o_b, g_ffn, w1_b, w3_b, w2_b, g_fin, tm=512)

    ps = _proj(xs, g_mix, w_in_p, tm=256)
    os_, sa_s, sb_s = _mixer_sample(ps, state_gla[0], state_hgrn[0], w_a2p, b_ap, lb_param,
                                    ona, onb, dec_batch, dec_seq, bt=8)
    ys = _out(os_, xs, w_o_b, g_ffn, w1_b, w3_b, w2_b, g_fin, tm=512)

    return (yp.reshape(batch, seq, D_MODEL), ys.reshape(dec_batch, dec_seq, D_MODEL),
            sa_p[None], sb_p[None], sa_s[None], sb_s[None])
```

```python
import functools

import jax
import jax.numpy as jnp
from jax import lax
from jax.experimental import pallas as pl
from jax.experimental.pallas import tpu as pltpu

D_MODEL = 1024
N_HEADS = 8
H_A = 4
DK_A = 64
HEAD = 128
GATE_RANK = 16
GATE_NORM = 16.0
D_FF = 2816
FF_CHUNK = 256
EPS = 1e-6
PROMPT_CHUNK = 32
SUBLANES = 8

SEC = H_A * HEAD
OFF_QA, OFF_KA, OFF_VA, OFF_GA, OFF_QB, OFF_FB, OFF_IB, OFF_GB, OFF_LR = (
    0, SEC, 2 * SEC, 3 * SEC, 4 * SEC, 5 * SEC, 6 * SEC, 7 * SEC, 8 * SEC)
D_P = 8 * SEC + HEAD

VMEM_LIMIT = 56 * 1024 * 1024

f32 = jnp.float32
bf16 = jnp.bfloat16


def _const_spec(shape):
    nd = len(shape)
    return pl.BlockSpec(shape, lambda *_: (0,) * nd, pipeline_mode=pl.Buffered(1))


def _rmsnorm(x, g):
    return x * lax.rsqrt(jnp.mean(x * x, axis=-1, keepdims=True) + EPS) * g


def _sigmoid(x):
    return 1.0 / (1.0 + jnp.exp(-x))


def _dot_nt(a, b):
    return lax.dot_general(a, b, (((1,), (1,)), ((), ())), preferred_element_type=f32)


def _dot_tn(a, b):
    return lax.dot_general(a, b, (((0,), (0,)), ((), ())), preferred_element_type=f32)


def _cumsum_rows(x, chunk):
    rows, width = x.shape
    row = lax.broadcasted_iota(jnp.int32, (SUBLANES, width), 0)
    groups_per_chunk = chunk // SUBLANES
    out = []
    carry = None
    for g in range(rows // SUBLANES):
        y = x[g * SUBLANES:(g + 1) * SUBLANES]
        for s in (1, 2, 4):
            y = y + jnp.where(row >= s, pltpu.roll(y, s, 0), 0.0)
        if g % groups_per_chunk != 0:
            y = y + carry
        carry = y[SUBLANES - 1:SUBLANES]
        out.append(y)
    return out[0] if len(out) == 1 else jnp.concatenate(out, axis=0)


def _lower_bound(lbp_ref):
    lbp = lbp_ref[...]
    m = jnp.max(lbp, axis=0, keepdims=True)
    e = jnp.exp(lbp - m)
    return e[0:1] / jnp.sum(e, axis=0, keepdims=True)


def _gates(p, w_a2, b_a, lb, chunk):
    rows = p.shape[0]
    qa = p[:, OFF_QA:OFF_QA + SEC] * (DK_A ** -0.5)
    ka = p[:, OFF_KA:OFF_KA + SEC]
    va = p[:, OFF_VA:OFF_VA + SEC]
    ga = p[:, OFF_GA:OFF_GA + SEC]
    qb = p[:, OFF_QB:OFF_QB + SEC]
    fb = p[:, OFF_FB:OFF_FB + SEC]
    ib = p[:, OFF_IB:OFF_IB + SEC]
    gb = p[:, OFF_GB:OFF_GB + SEC]
    lra = p[:, OFF_LR:OFF_LR + HEAD]

    z = jnp.dot(lra.astype(bf16), w_a2, preferred_element_type=f32) + b_a
    la_a = (jnp.minimum(z, 0.0) - jnp.log1p(jnp.exp(-jnp.abs(z)))) * (1.0 / GATE_NORM)

    qb = qb * _sigmoid(qb)
    f = lb + (1.0 - lb) * _sigmoid(fb)
    kb = 1.0 - f
    la_b = jnp.log(f)

    q = jnp.concatenate([qa, qb], axis=1)
    k = jnp.concatenate([ka, kb], axis=1)
    v = jnp.concatenate([va, ib], axis=1)
    la = jnp.concatenate([la_a, la_b], axis=1)
    g = jnp.concatenate([ga, gb], axis=1)
    g = g * _sigmoid(g)

    b = _cumsum_rows(la, chunk)
    n_chunks = rows // chunk
    lasts = [b[(c + 1) * chunk - 1:(c + 1) * chunk] for c in range(n_chunks)]
    if n_chunks == 1:
        b_last = lasts[0]
        b_last_rows = b_last
    else:
        b_last = jnp.concatenate(lasts, axis=0)
        b_last_rows = jnp.concatenate(
            [jnp.broadcast_to(l, (chunk, l.shape[1])) for l in lasts], axis=0)
    q_in = q * jnp.exp(b)
    k_in = k * jnp.exp(-b)
    k_out = k * jnp.exp(b_last_rows - b)
    decay = jnp.exp(b_last)
    return q_in, k_in, k_out, v, g, decay


def _proj_kernel(x_ref, gm_ref, w_ref, w_a2_ref, b_a_ref, lbp_ref,
                 qi_ref, ki_ref, ko_ref, v_ref, g_ref, d_ref, p_ref, *, chunk, block):
    h = _rmsnorm(x_ref[...], gm_ref[...]).astype(bf16)
    p_ref[...] = jnp.dot(h, w_ref[...], preferred_element_type=f32)
    lb = _lower_bound(lbp_ref)
    w_a2 = w_a2_ref[...]
    b_a = b_a_ref[...]
    cpb = block // chunk
    for r in range(x_ref.shape[0] // block):
        rows = slice(r * block, (r + 1) * block)
        q_in, k_in, k_out, v, g, decay = _gates(p_ref[rows, :], w_a2, b_a, lb, chunk)
        qi_ref[rows, :] = q_in.astype(qi_ref.dtype)
        ki_ref[rows, :] = k_in.astype(ki_ref.dtype)
        ko_ref[rows, :] = k_out.astype(ko_ref.dtype)
        v_ref[rows, :] = v.astype(v_ref.dtype)
        g_ref[rows, :] = g
        d_ref[r * cpb:(r + 1) * cpb, :] = decay


def _proj(x2d, g_mix, w_in_p, w_a2p, b_ap, lb_param, *, tm, chunk, act_dtype):
    n = x2d.shape[0]
    block = chunk if chunk >= PROMPT_CHUNK else SUBLANES * chunk
    kern = functools.partial(_proj_kernel, chunk=chunk, block=block)
    act = jax.ShapeDtypeStruct((n, D_MODEL), act_dtype)
    row_spec = pl.BlockSpec((tm, D_MODEL), lambda i: (i, 0))
    return pl.pallas_call(
        kern,
        out_shape=(act, act, act, act,
                   jax.ShapeDtypeStruct((n, D_MODEL), f32),
                   jax.ShapeDtypeStruct((n // chunk, D_MODEL), f32)),
        grid=(n // tm,),
        in_specs=[
            row_spec,
            _const_spec((1, D_MODEL)),
            _const_spec((D_MODEL, D_P)),
            _const_spec(w_a2p.shape),
            _const_spec(b_ap.shape),
            _const_spec(lb_param.shape),
        ],
        out_specs=(row_spec, row_spec, row_spec, row_spec, row_spec,
                   pl.BlockSpec((tm // chunk, D_MODEL), lambda i: (i, 0))),
        scratch_shapes=[pltpu.VMEM((tm, D_P), f32)],
        compiler_params=pltpu.CompilerParams(
            dimension_semantics=("arbitrary",), vmem_limit_bytes=VMEM_LIMIT),
        name="proj",
    )(x2d, g_mix, w_in_p, w_a2p, b_ap, lb_param)


def _head_norm(o, onorm):
    return o * lax.rsqrt(jnp.mean(o * o, axis=-1, keepdims=True) + EPS) * onorm


def _tril(n):
    r = lax.broadcasted_iota(jnp.int32, (n, n), 0)
    c = lax.broadcasted_iota(jnp.int32, (n, n), 1)
    return r >= c


def _mixer_prompt_kernel(qi_ref, ki_ref, ko_ref, v_ref, g_ref, d_ref, ona_ref, onb_ref,
                         o_ref, sa_ref, sb_ref, st_ref, att_ref, up_ref, start_ref, *, tile, chunk):
    j = pl.program_id(1)

    @pl.when(j == 0)
    def _():
        st_ref[...] = jnp.zeros_like(st_ref)

    onorm = [ona_ref[...]] * H_A + [onb_ref[...]] * (N_HEADS - H_A)
    tril = _tril(chunk)
    n_chunks = tile // chunk

    for h in range(N_HEADS):
        sl = slice(h * HEAD, (h + 1) * HEAD)
        for c in range(n_chunks):
            u = h * n_chunks + c
            rows = slice(c * chunk, (c + 1) * chunk)
            att = jnp.where(tril, _dot_nt(qi_ref[rows, sl], ki_ref[rows, sl]), 0.0)
            att_ref[u] = att.astype(bf16)
            up_ref[u] = _dot_tn(v_ref[rows, sl], ko_ref[rows, sl])

    for h in range(N_HEADS):
        sl = slice(h * HEAD, (h + 1) * HEAD)
        st = st_ref[h]
        for c in range(n_chunks):
            u = h * n_chunks + c
            start_ref[u] = st.astype(bf16)
            st = d_ref[c:c + 1, sl] * st + up_ref[u]
        st_ref[h] = st

    for h in range(N_HEADS):
        sl = slice(h * HEAD, (h + 1) * HEAD)
        outs = []
        for c in range(n_chunks):
            u = h * n_chunks + c
            rows = slice(c * chunk, (c + 1) * chunk)
            outs.append(_dot_nt(qi_ref[rows, sl], start_ref[u])
                        + jnp.dot(att_ref[u], v_ref[rows, sl], preferred_element_type=f32))
        o = jnp.concatenate(outs, axis=0)
        o_ref[:, sl] = (_head_norm(o, onorm[h]) * g_ref[:, sl]).astype(o_ref.dtype)

    @pl.when(j == pl.num_programs(1) - 1)
    def _():
        for h in range(H_A):
            sa_ref[0, h] = st_ref[h].T[:DK_A, :]
            sb_ref[0, h] = st_ref[H_A + h].T


def _mixer_prompt(acts, ona, onb, batch, seq, tile):
    nt = seq // tile
    chunk = PROMPT_CHUNK
    units = N_HEADS * (tile // chunk)
    kern = functools.partial(_mixer_prompt_kernel, tile=tile, chunk=chunk)
    row_spec = pl.BlockSpec((tile, D_MODEL), lambda b, j: (b * nt + j, 0))
    return pl.pallas_call(
        kern,
        out_shape=(
            jax.ShapeDtypeStruct((batch * seq, D_MODEL), bf16),
            jax.ShapeDtypeStruct((batch, H_A, DK_A, HEAD), f32),
            jax.ShapeDtypeStruct((batch, H_A, HEAD, HEAD), f32),
        ),
        grid=(batch, nt),
        in_specs=[
            row_spec, row_spec, row_spec, row_spec, row_spec,
            pl.BlockSpec((tile // chunk, D_MODEL), lambda b, j: (b * nt + j, 0)),
            _const_spec(ona.shape),
            _const_spec(onb.shape),
        ],
        out_specs=(
            row_spec,
            pl.BlockSpec((1, H_A, DK_A, HEAD), lambda b, j: (b, 0, 0, 0)),
            pl.BlockSpec((1, H_A, HEAD, HEAD), lambda b, j: (b, 0, 0, 0)),
        ),
        scratch_shapes=[
            pltpu.VMEM((N_HEADS, HEAD, HEAD), f32),
            pltpu.VMEM((units, chunk, chunk), bf16),
            pltpu.VMEM((units, HEAD, HEAD), f32),
            pltpu.VMEM((units, HEAD, HEAD), bf16),
        ],
        compiler_params=pltpu.CompilerParams(
            dimension_semantics=("arbitrary", "arbitrary"), vmem_limit_bytes=VMEM_LIMIT),
        name="mixer_prompt",
    )(*acts, ona, onb)


def _mixer_sample_kernel(qi_ref, ki_ref, ko_ref, v_ref, g_ref, d_ref, sa_in_ref, sb_in_ref,
                         ona_ref, onb_ref, o_ref, sa_ref, sb_ref, att_ref, oi_ref, *, bt, seq):
    onorm = [ona_ref[...]] * H_A + [onb_ref[...]] * (N_HEADS - H_A)
    tril = _tril(seq)
    zpad = jnp.zeros((HEAD - DK_A, HEAD), f32)
    zeros = jnp.zeros((seq, HEAD), f32)
    row = lax.broadcasted_iota(jnp.int32, (seq, HEAD), 0)
    ones_rows = jnp.where(row < 2, 1.0, 0.0)
    rhs_bottom = jnp.concatenate([zeros, ones_rows], axis=1)

    for bi in range(bt):
        rows = slice(bi * seq, (bi + 1) * seq)
        decay = d_ref[bi:bi + 1, :]
        d_hi = decay.astype(bf16).astype(f32)
        d_lo = decay - d_hi
        for h in range(N_HEADS):
            sl = slice(h * HEAD, (h + 1) * HEAD)
            q_in = qi_ref[rows, sl].astype(bf16)
            k_in = ki_ref[rows, sl].astype(bf16)
            v32 = v_ref[rows, sl]
            if h < H_A:
                s = jnp.concatenate([sa_in_ref[bi, h], zpad], axis=0)
            else:
                s = sb_in_ref[bi, h - H_A]
            att_ref[bi * N_HEADS + h] = jnp.where(tril, _dot_nt(q_in, k_in), 0.0)
            oi_ref[rows, sl] = jnp.dot(q_in, s.astype(bf16), preferred_element_type=f32)
            d_rows = jnp.where(row == 0, d_hi[:, sl], jnp.where(row == 1, d_lo[:, sl], 0.0))
            lhs = jnp.concatenate([ko_ref[rows, sl], d_rows], axis=0).astype(bf16)
            rhs = jnp.concatenate(
                [jnp.concatenate([v32, zeros], axis=1), rhs_bottom], axis=0).astype(bf16)
            upd = _dot_tn(lhs, rhs)
            s_new = upd[:, HEAD:] * s + upd[:, :HEAD]
            if h < H_A:
                sa_ref[bi, h] = s_new[:DK_A, :]
            else:
                sb_ref[bi, h - H_A] = s_new

    for bi in range(bt):
        rows = slice(bi * seq, (bi + 1) * seq)
        for h in range(N_HEADS):
            sl = slice(h * HEAD, (h + 1) * HEAD)
            att = att_ref[bi * N_HEADS + h].astype(bf16)
            o = oi_ref[rows, sl] + jnp.dot(att, v_ref[rows, sl].astype(bf16),
                                           preferred_element_type=f32)
            o_ref[rows, sl] = (_head_norm(o, onorm[h]) * g_ref[rows, sl]).astype(o_ref.dtype)


def _mixer_sample(acts, sa_in, sb_in, ona, onb, batch, seq, bt):
    kern = functools.partial(_mixer_sample_kernel, bt=bt, seq=seq)
    row_spec = pl.BlockSpec((bt * seq, D_MODEL), lambda i: (i, 0))
    sa_spec = pl.BlockSpec((bt, H_A, DK_A, HEAD), lambda i: (i, 0, 0, 0))
    sb_spec = pl.BlockSpec((bt, H_A, HEAD, HEAD), lambda i: (i, 0, 0, 0))
    return pl.pallas_call(
        kern,
        out_shape=(
            jax.ShapeDtypeStruct((batch * seq, D_MODEL), f32),
            jax.ShapeDtypeStruct((batch, H_A, DK_A, HEAD), f32),
            jax.ShapeDtypeStruct((batch, H_A, HEAD, HEAD), f32),
        ),
        grid=(batch // bt,),
        in_specs=[
            row_spec, row_spec, row_spec, row_spec, row_spec,
            pl.BlockSpec((bt, D_MODEL), lambda i: (i, 0)),
            sa_spec, sb_spec,
            _const_spec(ona.shape),
            _const_spec(onb.shape),
        ],
        out_specs=(row_spec, sa_spec, sb_spec),
        scratch_shapes=[
            pltpu.VMEM((bt * N_HEADS, seq, seq), f32),
            pltpu.VMEM((bt * seq, D_MODEL), f32),
        ],
        compiler_params=pltpu.CompilerParams(
            dimension_semantics=("arbitrary",), vmem_limit_bytes=VMEM_LIMIT),
        name="mixer_sample",
    )(*acts, sa_in, sb_in, ona, onb)


def _out_kernel(o_ref, x_ref, w_o_ref, gf_ref, w1_ref, w3_ref, w2_ref, gl_ref, y_ref):
    x1 = x_ref[...] + jnp.dot(o_ref[...].astype(bf16), w_o_ref[...], preferred_element_type=f32)
    h = _rmsnorm(x1, gf_ref[...]).astype(bf16)
    acc = x1
    for c in range(D_FF // FF_CHUNK):
        cols = slice(c * FF_CHUNK, (c + 1) * FF_CHUNK)
        a = jnp.dot(h, w1_ref[:, cols], preferred_element_type=f32)
        b = jnp.dot(h, w3_ref[:, cols], preferred_element_type=f32)
        u = (a * _sigmoid(a) * b).astype(bf16)
        acc = acc + jnp.dot(u, w2_ref[cols, :], preferred_element_type=f32)
    y_ref[...] = _rmsnorm(acc, gl_ref[...])


def _out(o2d, x2d, w_o, g_ffn, w1, w3, w2, g_final, tm):
    n = x2d.shape[0]
    return pl.pallas_call(
        _out_kernel,
        out_shape=jax.ShapeDtypeStruct((n, D_MODEL), f32),
        grid=(n // tm,),
        in_specs=[
            pl.BlockSpec((tm, D_MODEL), lambda i: (i, 0)),
            pl.BlockSpec((tm, D_MODEL), lambda i: (i, 0)),
            _const_spec((D_MODEL, D_MODEL)),
            _const_spec((1, D_MODEL)),
            _const_spec((D_MODEL, D_FF)),
            _const_spec((D_MODEL, D_FF)),
            _const_spec((D_FF, D_MODEL)),
            _const_spec((1, D_MODEL)),
        ],
        out_specs=pl.BlockSpec((tm, D_MODEL), lambda i: (i, 0)),
        compiler_params=pltpu.CompilerParams(
            dimension_semantics=("arbitrary",), vmem_limit_bytes=VMEM_LIMIT),
        name="out_ffn",
    )(o2d, x2d, w_o, g_ffn, w1, w3, w2, g_final)


def _pad_heads(w):
    lead = w.shape[:-1]
    w = w.reshape(lead + (H_A, DK_A))
    w = jnp.pad(w, [(0, 0)] * len(lead) + [(0, 0), (0, HEAD - DK_A)])
    return w.reshape(lead + (H_A * HEAD,))


def _layout_w_in(w_in):
    sizes = (H_A * DK_A, H_A * DK_A, SEC, SEC, GATE_RANK, SEC, SEC, SEC, SEC)
    parts, acc = [], 0
    for s in sizes:
        parts.append(w_in[:, acc:acc + s])
        acc += s
    qa, ka, va, ga, lra, qb, fb, ib, gb = parts
    lra = jnp.pad(lra, ((0, 0), (0, HEAD - GATE_RANK)))
    return jnp.concatenate([_pad_heads(qa), _pad_heads(ka), va, ga, qb, fb, ib, gb, lra], axis=1)


def kernel(x_prompt, x_sample, state_gla, state_hgrn, norm_mix, w_in, w_a2, b_a, lb_param,
           onorm_a, onorm_b, w_o, norm_ffn, w1, w3, w2, norm_final):
    batch, seq, _ = x_prompt.shape
    dec_batch, dec_seq, _ = x_sample.shape

    w_in_p = _layout_w_in(w_in[0]).astype(bf16)
    w_a2p = jnp.pad(_pad_heads(w_a2[0]), ((0, HEAD - GATE_RANK), (0, 0))).astype(bf16)
    b_ap = _pad_heads(b_a[0]).reshape(1, SEC)
    g_mix = norm_mix[0].reshape(1, D_MODEL)
    g_ffn = norm_ffn[0].reshape(1, D_MODEL)
    g_fin = norm_final.reshape(1, D_MODEL)
    ona = onorm_a[0].reshape(1, HEAD)
    onb = onorm_b[0].reshape(1, HEAD)
    w_o_b = w_o[0].astype(bf16)
    w1_b = w1[0].astype(bf16)
    w3_b = w3[0].astype(bf16)
    w2_b = w2[0].astype(bf16)

    xp = x_prompt.reshape(batch * seq, D_MODEL)
    xs = x_sample.reshape(dec_batch * dec_seq, D_MODEL)

    acts_p = _proj(xp, g_mix, w_in_p, w_a2p, b_ap, lb_param,
                   tm=256, chunk=PROMPT_CHUNK, act_dtype=bf16)
    op, sa_p, sb_p = _mixer_prompt(acts_p, ona, onb, batch, seq, tile=256)
    yp = _out(op, xp, w_o_b, g_ffn, w1_b, w3_b, w2_b, g_fin, tm=512)

    acts_s = _proj(xs, g_mix, w_in_p, w_a2p, b_ap, lb_param,
                   tm=256, chunk=dec_seq, act_dtype=f32)
    os_, sa_s, sb_s = _mixer_sample(acts_s, state_gla[0], state_hgrn[0], ona, onb,
                                    dec_batch, dec_seq, bt=8)
    ys = _out(os_, xs, w_o_b, g_ffn, w1_b, w3_b, w2_b, g_fin, tm=512)

    return (yp.reshape(batch, seq, D_MODEL), ys.reshape(dec_batch, dec_seq, D_MODEL),
            sa_p[None], sb_p[None], sa_s[None], sb_s[None])
```

```python
import functools

import jax
import jax.numpy as jnp
from jax import lax
from jax.experimental import pallas as pl
from jax.experimental.pallas import tpu as pltpu

D_MODEL = 1024
N_HEADS = 8
H_A = 4
DK_A = 64
HEAD = 128
GATE_RANK = 16
GATE_NORM = 16.0
D_FF = 2816
FF_CHUNK = 256
EPS = 1e-6
PROMPT_CHUNK = 32
SUBLANES = 8

KA_W = H_A * DK_A
SEC = H_A * HEAD
K_W = KA_W + SEC
OFF_LR = 0
OFF_QA = HEAD
OFF_KA = OFF_QA + KA_W
OFF_VA = OFF_KA + KA_W
OFF_GA = OFF_VA + SEC
OFF_FB = OFF_GA + SEC
OFF_QB = OFF_FB + SEC
OFF_IB = OFF_QB + SEC
OFF_GB = OFF_IB + SEC
D_P = OFF_GB + SEC
MXU_N = 256
PROJ_GROUPS = ((OFF_LR, OFF_VA), (OFF_FB, OFF_IB), (OFF_VA, OFF_FB), (OFF_IB, D_P))

VMEM_LIMIT = 56 * 1024 * 1024

f32 = jnp.float32
bf16 = jnp.bfloat16


def _const_spec(shape):
    nd = len(shape)
    return pl.BlockSpec(shape, lambda *_: (0,) * nd, pipeline_mode=pl.Buffered(1))


def _rmsnorm(x, g):
    return x * lax.rsqrt(jnp.mean(x * x, axis=-1, keepdims=True) + EPS) * g


def _sigmoid(x):
    return 0.5 * jnp.tanh(0.5 * x) + 0.5


def _silu(x):
    hx = 0.5 * x
    return hx + hx * jnp.tanh(hx)


def _dot_nt(a, b):
    return lax.dot_general(a, b, (((1,), (1,)), ((), ())), preferred_element_type=f32)


def _dot_tn(a, b):
    return lax.dot_general(a, b, (((0,), (0,)), ((), ())), preferred_element_type=f32)


def _key_lanes(h):
    return HEAD * (h // 2) if h < H_A else KA_W + HEAD * (h - H_A)


def _own_keys(h):
    return slice(DK_A * (h % 2), DK_A * (h % 2 + 1)) if h < H_A else slice(0, HEAD)


def _cumsum_rows(x, chunk):
    rows, width = x.shape
    row = lax.broadcasted_iota(jnp.int32, (SUBLANES, width), 0)
    groups_per_chunk = chunk // SUBLANES
    out = []
    carry = None
    for g in range(rows // SUBLANES):
        y = x[g * SUBLANES:(g + 1) * SUBLANES]
        for s in (1, 2, 4):
            y = y + jnp.where(row >= s, pltpu.roll(y, s, 0), 0.0)
        if g % groups_per_chunk != 0:
            y = y + carry
        carry = y[SUBLANES - 1:SUBLANES]
        out.append(y)
    return out[0] if len(out) == 1 else jnp.concatenate(out, axis=0)


def _lower_bound(lbp_ref):
    lbp = lbp_ref[...]
    m = jnp.max(lbp, axis=0, keepdims=True)
    e = jnp.exp(lbp - m)
    return e[0:1] / jnp.sum(e, axis=0, keepdims=True)


def _decays(la, chunk):
    rows = la.shape[0]
    b = _cumsum_rows(la, chunk)
    n_chunks = rows // chunk
    lasts = [b[(c + 1) * chunk - 1:(c + 1) * chunk] for c in range(n_chunks)]
    if n_chunks == 1:
        b_last = lasts[0]
        b_last_rows = b_last
    else:
        b_last = jnp.concatenate(lasts, axis=0)
        b_last_rows = jnp.concatenate(
            [jnp.broadcast_to(l, (chunk, l.shape[1])) for l in lasts], axis=0)
    return jnp.exp(b), jnp.exp(-b), jnp.exp(b_last_rows - b), jnp.exp(b_last)


def _proj_kernel(x_ref, gm_ref, w_ref, w_a2_ref, b_a_ref, lbp_ref,
                 qi_ref, ki_ref, ko_ref, v_ref, g_ref, d_ref, *p_refs, chunk, block):
    tm = x_ref.shape[0]
    cpb = block // chunk
    blocks = [(slice(r * block, (r + 1) * block), slice(r * cpb, (r + 1) * cpb))
              for r in range(tm // block)]
    h = _rmsnorm(x_ref[...], gm_ref[...]).astype(bf16)

    w_a2 = w_a2_ref[...]
    b_a = b_a_ref[...]
    lb = _lower_bound(lbp_ref)
    lane = lax.broadcasted_iota(jnp.int32, (block, HEAD), 1)

    def p_slot(lo, hi):
        for (g_lo, g_hi), ref in zip(PROJ_GROUPS, p_refs):
            if g_lo <= lo and hi <= g_hi:
                return ref, slice(lo - g_lo, hi - g_lo)
        raise ValueError((lo, hi))

    def project(lo, hi):
        ref, cols = p_slot(lo, hi)
        ref[:, cols] = jnp.dot(h, w_ref[:, lo:hi], preferred_element_type=f32)

    def p(rows, lo, width):
        ref, cols = p_slot(lo, lo + width)
        return ref[rows, cols]

    def gla_keys(rows, crow):
        lra = p(rows, OFF_LR, HEAD)
        z = jnp.dot(lra.astype(bf16), w_a2, preferred_element_type=f32) + b_a
        for pr in range(H_A // 2):
            kl = slice(pr * HEAD, (pr + 1) * HEAD)
            zz = z[:, kl]
            la = (jnp.minimum(zz, 0.0) - jnp.log1p(jnp.exp(-jnp.abs(zz)))) * (1.0 / GATE_NORM)
            e_b, e_nb, e_out, decay = _decays(la, chunk)
            q_in = p(rows, OFF_QA + pr * HEAD, HEAD) * (DK_A ** -0.5) * e_b
            k = p(rows, OFF_KA + pr * HEAD, HEAD)
            for hd, own in ((2 * pr, lane < DK_A), (2 * pr + 1, lane >= DK_A)):
                qi_ref[rows, hd * HEAD:(hd + 1) * HEAD] = jnp.where(own, q_in, 0.0).astype(
                    qi_ref.dtype)
            ki_ref[rows, kl] = (k * e_nb).astype(ki_ref.dtype)
            ko_ref[rows, kl] = (k * e_out).astype(ko_ref.dtype)
            d_ref[crow, kl] = decay

    def hgrn_keys(rows, crow):
        for hd in range(N_HEADS - H_A):
            cl = slice(hd * HEAD, (hd + 1) * HEAD)
            lbh = lb[:, cl]
            f = lbh + (1.0 - lbh) * _sigmoid(p(rows, OFF_FB + hd * HEAD, HEAD))
            k = 1.0 - f
            e_b, e_nb, e_out, decay = _decays(jnp.log(f), chunk)
            ql = slice(SEC + hd * HEAD, SEC + (hd + 1) * HEAD)
            kl = slice(KA_W + hd * HEAD, KA_W + (hd + 1) * HEAD)
            qi_ref[rows, ql] = (_silu(p(rows, OFF_QB + hd * HEAD, HEAD)) * e_b).astype(
                qi_ref.dtype)
            ki_ref[rows, kl] = (k * e_nb).astype(ki_ref.dtype)
            ko_ref[rows, kl] = (k * e_out).astype(ko_ref.dtype)
            d_ref[crow, kl] = decay

    def gla_values(rows, crow):
        v_ref[rows, 0:SEC] = p(rows, OFF_VA, SEC).astype(v_ref.dtype)
        g_ref[rows, 0:SEC] = _silu(p(rows, OFF_GA, SEC))

    def hgrn_values(rows, crow):
        v_ref[rows, SEC:2 * SEC] = p(rows, OFF_IB, SEC).astype(v_ref.dtype)
        g_ref[rows, SEC:2 * SEC] = _silu(p(rows, OFF_GB, SEC))

    def tasks(fn, which):
        return [functools.partial(fn, *blocks[r]) for r in which]

    nb = len(blocks)
    first, second = range(nb // 2), range(nb // 2, nb)
    windows = (
        [],
        tasks(gla_keys, range(nb)),
        tasks(hgrn_keys, first),
        tasks(hgrn_keys, second) + tasks(gla_values, range(nb)),
    )
    for (lo, hi), work in zip(PROJ_GROUPS, windows):
        starts = list(range(lo, hi, MXU_N))
        for i, c0 in enumerate(starts):
            project(c0, min(c0 + MXU_N, hi))
            for t in work[i * len(work) // len(starts):(i + 1) * len(work) // len(starts)]:
                t()
    for t in tasks(hgrn_values, range(nb)):
        t()


def _proj(x2d, g_mix, w_in_p, w_a2p, b_ap, lb_param, *, tm, chunk, act_dtype):
    n = x2d.shape[0]
    block = chunk if chunk >= PROMPT_CHUNK else SUBLANES * chunk
    kern = functools.partial(_proj_kernel, chunk=chunk, block=block)

    def rows(width):
        return pl.BlockSpec((tm, width), lambda i: (i, 0))

    return pl.pallas_call(
        kern,
        out_shape=(jax.ShapeDtypeStruct((n, D_MODEL), act_dtype),
                   jax.ShapeDtypeStruct((n, K_W), act_dtype),
                   jax.ShapeDtypeStruct((n, K_W), act_dtype),
                   jax.ShapeDtypeStruct((n, D_MODEL), act_dtype),
                   jax.ShapeDtypeStruct((n, D_MODEL), f32),
                   jax.ShapeDtypeStruct((n // chunk, K_W), f32)),
        grid=(n // tm,),
        in_specs=[
            rows(D_MODEL),
            _const_spec((1, D_MODEL)),
            _const_spec((D_MODEL, D_P)),
            _const_spec(w_a2p.shape),
            _const_spec(b_ap.shape),
            _const_spec(lb_param.shape),
        ],
        out_specs=(rows(D_MODEL), rows(K_W), rows(K_W), rows(D_MODEL), rows(D_MODEL),
                   pl.BlockSpec((tm // chunk, K_W), lambda i: (i, 0))),
        scratch_shapes=[pltpu.VMEM((tm, hi - lo), f32) for lo, hi in PROJ_GROUPS],
        compiler_params=pltpu.CompilerParams(
            dimension_semantics=("arbitrary",), vmem_limit_bytes=VMEM_LIMIT),
        name="proj",
    )(x2d, g_mix, w_in_p, w_a2p, b_ap, lb_param)


def _head_norm(o, onorm):
    return o * lax.rsqrt(jnp.mean(o * o, axis=-1, keepdims=True) + EPS) * onorm


def _tril(n):
    r = lax.broadcasted_iota(jnp.int32, (n, n), 0)
    c = lax.broadcasted_iota(jnp.int32, (n, n), 1)
    return r >= c


def _mixer_prompt_kernel(qi_ref, ki_ref, ko_ref, v_ref, g_ref, d_ref, ona_ref, onb_ref,
                         o_ref, sa_ref, sb_ref, st_ref, att_ref, up_ref, start_ref, *, tile, chunk):
    j = pl.program_id(1)

    @pl.when(j == 0)
    def _():
        st_ref[...] = jnp.zeros_like(st_ref)

    onorm = [ona_ref[...]] * H_A + [onb_ref[...]] * (N_HEADS - H_A)
    tril = _tril(chunk)
    n_chunks = tile // chunk

    for h in range(N_HEADS):
        sl = slice(h * HEAD, (h + 1) * HEAD)
        kl = slice(_key_lanes(h), _key_lanes(h) + HEAD)
        for c in range(n_chunks):
            u = h * n_chunks + c
            rows = slice(c * chunk, (c + 1) * chunk)
            att = jnp.where(tril, _dot_nt(qi_ref[rows, sl], ki_ref[rows, kl]), 0.0)
            att_ref[u] = att.astype(bf16)
            up_ref[u] = _dot_tn(v_ref[rows, sl], ko_ref[rows, kl])

    for h in range(N_HEADS):
        kl = slice(_key_lanes(h), _key_lanes(h) + HEAD)
        st = st_ref[h]
        for c in range(n_chunks):
            u = h * n_chunks + c
            start_ref[u] = st.astype(bf16)
            st = d_ref[c:c + 1, kl] * st + up_ref[u]
        st_ref[h] = st

    for h in range(N_HEADS):
        sl = slice(h * HEAD, (h + 1) * HEAD)
        outs = []
        for c in range(n_chunks):
            u = h * n_chunks + c
            rows = slice(c * chunk, (c + 1) * chunk)
            outs.append(_dot_nt(qi_ref[rows, sl], start_ref[u])
                        + jnp.dot(att_ref[u], v_ref[rows, sl], preferred_element_type=f32))
        o = jnp.concatenate(outs, axis=0)
        o_ref[:, sl] = (_head_norm(o, onorm[h]) * g_ref[:, sl]).astype(o_ref.dtype)

    @pl.when(j == pl.num_programs(1) - 1)
    def _():
        for h in range(H_A):
            sa_ref[0, h] = st_ref[h].T[_own_keys(h), :]
            sb_ref[0, h] = st_ref[H_A + h].T


def _mixer_prompt(acts, ona, onb, batch, seq, tile):
    nt = seq // tile
    chunk = PROMPT_CHUNK
    units = N_HEADS * (tile // chunk)
    kern = functools.partial(_mixer_prompt_kernel, tile=tile, chunk=chunk)

    def rows(width):
        return pl.BlockSpec((tile, width), lambda b, j: (b * nt + j, 0))

    return pl.pallas_call(
        kern,
        out_shape=(
            jax.ShapeDtypeStruct((batch * seq, D_MODEL), bf16),
            jax.ShapeDtypeStruct((batch, H_A, DK_A, HEAD), f32),
            jax.ShapeDtypeStruct((batch, H_A, HEAD, HEAD), f32),
        ),
        grid=(batch, nt),
        in_specs=[
            rows(D_MODEL), rows(K_W), rows(K_W), rows(D_MODEL), rows(D_MODEL),
            pl.BlockSpec((tile // chunk, K_W), lambda b, j: (b * nt + j, 0)),
            _const_spec(ona.shape),
            _const_spec(onb.shape),
        ],
        out_specs=(
            rows(D_MODEL),
            pl.BlockSpec((1, H_A, DK_A, HEAD), lambda b, j: (b, 0, 0, 0)),
            pl.BlockSpec((1, H_A, HEAD, HEAD), lambda b, j: (b, 0, 0, 0)),
        ),
        scratch_shapes=[
            pltpu.VMEM((N_HEADS, HEAD, HEAD), f32),
            pltpu.VMEM((units, chunk, chunk), bf16),
            pltpu.VMEM((units, HEAD, HEAD), f32),
            pltpu.VMEM((units, HEAD, HEAD), bf16),
        ],
        compiler_params=pltpu.CompilerParams(
            dimension_semantics=("arbitrary", "arbitrary"), vmem_limit_bytes=VMEM_LIMIT),
        name="mixer_prompt",
    )(*acts, ona, onb)


def _mixer_sample_kernel(qi_ref, ki_ref, ko_ref, v_ref, g_ref, d_ref, sa_in_ref, sb_in_ref,
                         ona_ref, onb_ref, o_ref, sa_ref, sb_ref, att_ref, oi_ref, *, bt, seq):
    onorm = [ona_ref[...]] * H_A + [onb_ref[...]] * (N_HEADS - H_A)
    tril = _tril(seq)
    zpad = jnp.zeros((HEAD - DK_A, HEAD), f32)
    zeros = jnp.zeros((seq, HEAD), f32)
    row = lax.broadcasted_iota(jnp.int32, (seq, HEAD), 0)
    ones_rows = jnp.where(row < 2, 1.0, 0.0)
    rhs_bottom = jnp.concatenate([zeros, ones_rows], axis=1)

    for bi in range(bt):
        rows = slice(bi * seq, (bi + 1) * seq)
        decay = d_ref[bi:bi + 1, :]
        d_hi = decay.astype(bf16).astype(f32)
        d_lo = decay - d_hi
        for h in range(N_HEADS):
            sl = slice(h * HEAD, (h + 1) * HEAD)
            kl = slice(_key_lanes(h), _key_lanes(h) + HEAD)
            q_in = qi_ref[rows, sl].astype(bf16)
            k_in = ki_ref[rows, kl].astype(bf16)
            v32 = v_ref[rows, sl]
            if h >= H_A:
                s = sb_in_ref[bi, h - H_A]
            elif h % 2 == 0:
                s = jnp.concatenate([sa_in_ref[bi, h], zpad], axis=0)
            else:
                s = jnp.concatenate([zpad, sa_in_ref[bi, h]], axis=0)
            att_ref[bi * N_HEADS + h] = jnp.where(tril, _dot_nt(q_in, k_in), 0.0)
            oi_ref[rows, sl] = jnp.dot(q_in, s.astype(bf16), preferred_element_type=f32)
            d_rows = jnp.where(row == 0, d_hi[:, kl], jnp.where(row == 1, d_lo[:, kl], 0.0))
            lhs = jnp.concatenate([ko_ref[rows, kl], d_rows], axis=0).astype(bf16)
            rhs = jnp.concatenate(
                [jnp.concatenate([v32, zeros], axis=1), rhs_bottom], axis=0).astype(bf16)
            upd = _dot_tn(lhs, rhs)
            s_new = upd[:, HEAD:] * s + upd[:, :HEAD]
            if h < H_A:
                sa_ref[bi, h] = s_new[_own_keys(h), :]
            else:
                sb_ref[bi, h - H_A] = s_new

    for bi in range(bt):
        rows = slice(bi * seq, (bi + 1) * seq)
        for h in range(N_HEADS):
            sl = slice(h * HEAD, (h + 1) * HEAD)
            att = att_ref[bi * N_HEADS + h].astype(bf16)
            o = oi_ref[rows, sl] + jnp.dot(att, v_ref[rows, sl].astype(bf16),
                                           preferred_element_type=f32)
            o_ref[rows, sl] = (_head_norm(o, onorm[h]) * g_ref[rows, sl]).astype(o_ref.dtype)


def _mixer_sample(acts, sa_in, sb_in, ona, onb, batch, seq, bt):
    kern = functools.partial(_mixer_sample_kernel, bt=bt, seq=seq)

    def rows(width):
        return pl.BlockSpec((bt * seq, width), lambda i: (i, 0))

    sa_spec = pl.BlockSpec((bt, H_A, DK_A, HEAD), lambda i: (i, 0, 0, 0))
    sb_spec = pl.BlockSpec((bt, H_A, HEAD, HEAD), lambda i: (i, 0, 0, 0))
    return pl.pallas_call(
        kern,
        out_shape=(
            jax.ShapeDtypeStruct((batch * seq, D_MODEL), f32),
            jax.ShapeDtypeStruct((batch, H_A, DK_A, HEAD), f32),
            jax.ShapeDtypeStruct((batch, H_A, HEAD, HEAD), f32),
        ),
        grid=(batch // bt,),
        in_specs=[
            rows(D_MODEL), rows(K_W), rows(K_W), rows(D_MODEL), rows(D_MODEL),
            pl.BlockSpec((bt, K_W), lambda i: (i, 0)),
            sa_spec, sb_spec,
            _const_spec(ona.shape),
            _const_spec(onb.shape),
        ],
        out_specs=(rows(D_MODEL), sa_spec, sb_spec),
        scratch_shapes=[
            pltpu.VMEM((bt * N_HEADS, seq, seq), f32),
            pltpu.VMEM((bt * seq, D_MODEL), f32),
        ],
        compiler_params=pltpu.CompilerParams(
            dimension_semantics=("arbitrary",), vmem_limit_bytes=VMEM_LIMIT),
        name="mixer_sample",
    )(*acts, sa_in, sb_in, ona, onb)


def _out_kernel(o_ref, x_ref, w_o_ref, gf_ref, w1_ref, w3_ref, w2_ref, gl_ref, y_ref):
    x1 = x_ref[...] + jnp.dot(o_ref[...].astype(bf16), w_o_ref[...], preferred_element_type=f32)
    h = _rmsnorm(x1, gf_ref[...]).astype(bf16)
    acc = x1
    for c in range(D_FF // FF_CHUNK):
        cols = slice(c * FF_CHUNK, (c + 1) * FF_CHUNK)
        a = jnp.dot(h, w1_ref[:, cols], preferred_element_type=f32)
        b = jnp.dot(h, w3_ref[:, cols], preferred_element_type=f32)
        u = (_silu(a) * b).astype(bf16)
        acc = acc + jnp.dot(u, w2_ref[cols, :], preferred_element_type=f32)
    y_ref[...] = _rmsnorm(acc, gl_ref[...])


def _out(o2d, x2d, w_o, g_ffn, w1, w3, w2, g_final, tm):
    n = x2d.shape[0]
    return pl.pallas_call(
        _out_kernel,
        out_shape=jax.ShapeDtypeStruct((n, D_MODEL), f32),
        grid=(n // tm,),
        in_specs=[
            pl.BlockSpec((tm, D_MODEL), lambda i: (i, 0)),
            pl.BlockSpec((tm, D_MODEL), lambda i: (i, 0)),
            _const_spec((D_MODEL, D_MODEL)),
            _const_spec((1, D_MODEL)),
            _const_spec((D_MODEL, D_FF)),
            _const_spec((D_MODEL, D_FF)),
            _const_spec((D_FF, D_MODEL)),
            _const_spec((1, D_MODEL)),
        ],
        out_specs=pl.BlockSpec((tm, D_MODEL), lambda i: (i, 0)),
        compiler_params=pltpu.CompilerParams(
            dimension_semantics=("arbitrary",), vmem_limit_bytes=VMEM_LIMIT),
        name="out_ffn",
    )(o2d, x2d, w_o, g_ffn, w1, w3, w2, g_final)


def _layout_w_in(w_in):
    sizes = (KA_W, KA_W, SEC, SEC, GATE_RANK, SEC, SEC, SEC, SEC)
    parts, acc = [], 0
    for s in sizes:
        parts.append(w_in[:, acc:acc + s])
        acc += s
    qa, ka, va, ga, lra, qb, fb, ib, gb = parts
    lra = jnp.pad(lra, ((0, 0), (0, HEAD - GATE_RANK)))
    return jnp.concatenate([lra, qa, ka, va, ga, fb, qb, ib, gb], axis=1)


def kernel(x_prompt, x_sample, state_gla, state_hgrn, norm_mix, w_in, w_a2, b_a, lb_param,
           onorm_a, onorm_b, w_o, norm_ffn, w1, w3, w2, norm_final):
    batch, seq, _ = x_prompt.shape
    dec_batch, dec_seq, _ = x_sample.shape

    w_in_p = _layout_w_in(w_in[0]).astype(bf16)
    w_a2p = jnp.pad(w_a2[0], ((0, HEAD - GATE_RANK), (0, 0))).astype(bf16)
    b_ap = b_a[0].reshape(1, KA_W)
    g_mix = norm_mix[0].reshape(1, D_MODEL)
    g_ffn = norm_ffn[0].reshape(1, D_MODEL)
    g_fin = norm_final.reshape(1, D_MODEL)
    ona = onorm_a[0].reshape(1, HEAD)
    onb = onorm_b[0].reshape(1, HEAD)
    w_o_b = w_o[0].astype(bf16)
    w1_b = w1[0].astype(bf16)
    w3_b = w3[0].astype(bf16)
    w2_b = w2[0].astype(bf16)

    xp = x_prompt.reshape(batch * seq, D_MODEL)
    xs = x_sample.reshape(dec_batch * dec_seq, D_MODEL)

    acts_p = _proj(xp, g_mix, w_in_p, w_a2p, b_ap, lb_param,
                   tm=256, chunk=PROMPT_CHUNK, act_dtype=bf16)
    op, sa_p, sb_p = _mixer_prompt(acts_p, ona, onb, batch, seq, tile=256)
    yp = _out(op, xp, w_o_b, g_ffn, w1_b, w3_b, w2_b, g_fin, tm=512)

    acts_s = _proj(xs, g_mix, w_in_p, w_a2p, b_ap, lb_param,
                   tm=256, chunk=dec_seq, act_dtype=f32)
    os_, sa_s, sb_s = _mixer_sample(acts_s, state_gla[0], state_hgrn[0], ona, onb,
                                    dec_batch, dec_seq, bt=8)
    ys = _out(os_, xs, w_o_b, g_ffn, w1_b, w3_b, w2_b, g_fin, tm=512)

    return (yp.reshape(batch, seq, D_MODEL), ys.reshape(dec_batch, dec_seq, D_MODEL),
            sa_p[None], sb_p[None], sa_s[None], sb_s[None])
```

```python
import functools

import jax
import jax.numpy as jnp
from jax import lax
from jax.experimental import pallas as pl
from jax.experimental.pallas import tpu as pltpu

D_MODEL = 1024
N_HEADS = 8
H_A = 4
DK_A = 64
HEAD = 128
GATE_RANK = 16
GATE_NORM = 16.0
D_FF = 2816
FF_CHUNK = 256
EPS = 1e-6
PROMPT_CHUNK = 32
SUBLANES = 8
BF16_ROWS = 16

KA_W = H_A * DK_A
SEC = H_A * HEAD
K_W = KA_W + SEC
OFF_LR = 0
OFF_QA = HEAD
OFF_KA = OFF_QA + KA_W
OFF_VA = OFF_KA + KA_W
OFF_GA = OFF_VA + SEC
OFF_FB = OFF_GA + SEC
OFF_QB = OFF_FB + SEC
OFF_IB = OFF_QB + SEC
OFF_GB = OFF_IB + SEC
D_P = OFF_GB + SEC
MXU_N = 256
PROJ_GROUPS = ((OFF_LR, OFF_VA), (OFF_FB, OFF_IB), (OFF_VA, OFF_FB), (OFF_IB, D_P))

VMEM_LIMIT = 56 * 1024 * 1024

f32 = jnp.float32
bf16 = jnp.bfloat16


def _const_spec(shape):
    nd = len(shape)
    return pl.BlockSpec(shape, lambda *_: (0,) * nd, pipeline_mode=pl.Buffered(1))


def _rmsnorm(x, g):
    return x * lax.rsqrt(jnp.mean(x * x, axis=-1, keepdims=True) + EPS) * g


def _sigmoid(x):
    return 0.5 * jnp.tanh(0.5 * x) + 0.5


def _silu(x):
    hx = 0.5 * x
    return hx + hx * jnp.tanh(hx)


def _dot_nt(a, b):
    return lax.dot_general(a, b, (((1,), (1,)), ((), ())), preferred_element_type=f32)


def _dot_tn(a, b):
    return lax.dot_general(a, b, (((0,), (0,)), ((), ())), preferred_element_type=f32)


def _key_lanes(h):
    return HEAD * (h // 2) if h < H_A else KA_W + HEAD * (h - H_A)


def _own_keys(h):
    return slice(DK_A * (h % 2), DK_A * (h % 2 + 1)) if h < H_A else slice(0, HEAD)


def _cumsum_rows(x, chunk):
    rows, width = x.shape
    row = lax.broadcasted_iota(jnp.int32, (SUBLANES, width), 0)
    groups_per_chunk = chunk // SUBLANES
    out = []
    carry = None
    for g in range(rows // SUBLANES):
        y = x[g * SUBLANES:(g + 1) * SUBLANES]
        for s in (1, 2, 4):
            y = y + jnp.where(row >= s, pltpu.roll(y, s, 0), 0.0)
        if g % groups_per_chunk != 0:
            y = y + carry
        carry = y[SUBLANES - 1:SUBLANES]
        out.append(y)
    return out[0] if len(out) == 1 else jnp.concatenate(out, axis=0)


def _lower_bound(lbp_ref):
    lbp = lbp_ref[...]
    m = jnp.max(lbp, axis=0, keepdims=True)
    e = jnp.exp(lbp - m)
    return e[0:1] / jnp.sum(e, axis=0, keepdims=True)


def _decays(la, chunk):
    rows = la.shape[0]
    b = _cumsum_rows(la, chunk)
    n_chunks = rows // chunk
    lasts = [b[(c + 1) * chunk - 1:(c + 1) * chunk] for c in range(n_chunks)]
    if n_chunks == 1:
        b_last = lasts[0]
        b_last_rows = b_last
    else:
        b_last = jnp.concatenate(lasts, axis=0)
        b_last_rows = jnp.concatenate(
            [jnp.broadcast_to(l, (chunk, l.shape[1])) for l in lasts], axis=0)
    return jnp.exp(b), jnp.exp(-b), jnp.exp(b_last_rows - b), jnp.exp(b_last)


def _emit_keys_paired(refs, r, q_targets, kl, q, k, la):
    q32_ref, q64_ref, kin_ref, kx_ref, ko_ref, d_ref = refs
    c = PROMPT_CHUNK
    rows = slice(2 * c * r, 2 * c * (r + 1))
    e_b, e_nb, e_out, decay = _decays(la, c)
    d0, d1 = decay[0:1], decay[1:2]
    q32 = q * e_b
    q64 = jnp.concatenate([q32[:c], q32[c:] * d0], axis=0)
    k_in = k * e_nb
    k_out = k * e_out
    for ql, own in q_targets:
        q32_ref[rows, ql] = (q32 if own is None else jnp.where(own, q32, 0.0)).astype(bf16)
        q64_ref[rows, ql] = (q64 if own is None else jnp.where(own, q64, 0.0)).astype(bf16)
    kin_ref[c * r:c * (r + 1), kl] = k_in[:c].astype(bf16)
    kx_ref[rows, kl] = jnp.concatenate([k_out[:c], k_in[c:]], axis=0).astype(bf16)
    ko_ref[rows, kl] = jnp.concatenate([k_out[:c] * d1, k_out[c:]], axis=0).astype(bf16)
    d_ref[r:r + 1, kl] = d0 * d1


def _emit_keys_single(refs, r, q_targets, kl, q, k, la, *, chunk):
    q_ref, kin_ref, ko_ref, d_ref = refs
    rows = slice(SUBLANES * chunk * r, SUBLANES * chunk * (r + 1))
    e_b, e_nb, e_out, decay = _decays(la, chunk)
    q_in = q * e_b
    for ql, own in q_targets:
        q_ref[rows, ql] = q_in if own is None else jnp.where(own, q_in, 0.0)
    kin_ref[rows, kl] = k * e_nb
    ko_ref[rows, kl] = k * e_out
    d_ref[SUBLANES * r:SUBLANES * (r + 1), kl] = decay


def _proj_kernel(x_ref, gm_ref, w_ref, w_a2_ref, b_a_ref, lbp_ref, *refs, chunk, paired):
    n_key_refs = 6 if paired else 4
    key_refs = refs[:n_key_refs]
    v_ref, g_ref = refs[n_key_refs:n_key_refs + 2]
    p_refs = refs[n_key_refs + 2:]
    tm = x_ref.shape[0]
    block = 2 * chunk if paired else SUBLANES * chunk
    emit_keys = (_emit_keys_paired if paired
                 else functools.partial(_emit_keys_single, chunk=chunk))
    blocks = [(slice(r * block, (r + 1) * block), r) for r in range(tm // block)]
    h = _rmsnorm(x_ref[...], gm_ref[...]).astype(bf16)

    w_a2 = w_a2_ref[...]
    b_a = b_a_ref[...]
    lb = _lower_bound(lbp_ref)
    lane = lax.broadcasted_iota(jnp.int32, (block, HEAD), 1)

    def p_slot(lo, hi):
        for (g_lo, g_hi), ref in zip(PROJ_GROUPS, p_refs):
            if g_lo <= lo and hi <= g_hi:
                return ref, slice(lo - g_lo, hi - g_lo)
        raise ValueError((lo, hi))

    def project(lo, hi):
        ref, cols = p_slot(lo, hi)
        ref[:, cols] = jnp.dot(h, w_ref[:, lo:hi], preferred_element_type=f32)

    def p(rows, lo, width):
        ref, cols = p_slot(lo, lo + width)
        return ref[rows, cols]

    def gla_keys(rows, r):
        lra = p(rows, OFF_LR, HEAD)
        z = jnp.dot(lra.astype(bf16), w_a2, preferred_element_type=f32) + b_a
        for pr in range(H_A // 2):
            kl = slice(pr * HEAD, (pr + 1) * HEAD)
            zz = z[:, kl]
            la = (jnp.minimum(zz, 0.0) - jnp.log1p(jnp.exp(-jnp.abs(zz)))) * (1.0 / GATE_NORM)
            q = p(rows, OFF_QA + pr * HEAD, HEAD) * (DK_A ** -0.5)
            k = p(rows, OFF_KA + pr * HEAD, HEAD)
            q_targets = [(slice(hd * HEAD, (hd + 1) * HEAD), own)
                         for hd, own in ((2 * pr, lane < DK_A), (2 * pr + 1, lane >= DK_A))]
            emit_keys(key_refs, r, q_targets, kl, q, k, la)

    def hgrn_keys(rows, r):
        for hd in range(N_HEADS - H_A):
            lbh = lb[:, hd * HEAD:(hd + 1) * HEAD]
            f = lbh + (1.0 - lbh) * _sigmoid(p(rows, OFF_FB + hd * HEAD, HEAD))
            q = _silu(p(rows, OFF_QB + hd * HEAD, HEAD))
            ql = slice(SEC + hd * HEAD, SEC + (hd + 1) * HEAD)
            kl = slice(KA_W + hd * HEAD, KA_W + (hd + 1) * HEAD)
            emit_keys(key_refs, r, [(ql, None)], kl, q, 1.0 - f, jnp.log(f))

    def gla_values(rows, crow):
        v_ref[rows, 0:SEC] = p(rows, OFF_VA, SEC).astype(v_ref.dtype)
        g_ref[rows, 0:SEC] = _silu(p(rows, OFF_GA, SEC))

    def hgrn_values(rows, crow):
        v_ref[rows, SEC:2 * SEC] = p(rows, OFF_IB, SEC).astype(v_ref.dtype)
        g_ref[rows, SEC:2 * SEC] = _silu(p(rows, OFF_GB, SEC))

    def tasks(fn, which):
        return [functools.partial(fn, *blocks[r]) for r in which]

    nb = len(blocks)
    first, second = range(nb // 2), range(nb // 2, nb)
    windows = (
        [],
        tasks(gla_keys, range(nb)),
        tasks(hgrn_keys, first),
        tasks(hgrn_keys, second) + tasks(gla_values, range(nb)),
    )
    for (lo, hi), work in zip(PROJ_GROUPS, windows):
        starts = list(range(lo, hi, MXU_N))
        for i, c0 in enumerate(starts):
            project(c0, min(c0 + MXU_N, hi))
            for t in work[i * len(work) // len(starts):(i + 1) * len(work) // len(starts)]:
                t()
    for t in tasks(hgrn_values, range(nb)):
        t()


def _proj(x2d, g_mix, w_in_p, w_a2p, b_ap, lb_param, *, tm, chunk, paired):
    n = x2d.shape[0]
    kern = functools.partial(_proj_kernel, chunk=chunk, paired=paired)

    def arr(rows, width, dtype):
        return (jax.ShapeDtypeStruct((n // rows, width), dtype),
                pl.BlockSpec((tm // rows, width), lambda i: (i, 0)))

    if paired:
        keys = [arr(1, D_MODEL, bf16), arr(1, D_MODEL, bf16), arr(2, K_W, bf16),
                arr(1, K_W, bf16), arr(1, K_W, bf16), arr(2 * chunk, K_W, f32)]
        v_out = arr(1, D_MODEL, bf16)
    else:
        keys = [arr(1, D_MODEL, f32), arr(1, K_W, f32), arr(1, K_W, f32), arr(chunk, K_W, f32)]
        v_out = arr(1, D_MODEL, f32)
    outs = keys + [v_out, arr(1, D_MODEL, f32)]

    return pl.pallas_call(
        kern,
        out_shape=tuple(o[0] for o in outs),
        grid=(n // tm,),
        in_specs=[
            pl.BlockSpec((tm, D_MODEL), lambda i: (i, 0)),
            _const_spec((1, D_MODEL)),
            _const_spec((D_MODEL, D_P)),
            _const_spec(w_a2p.shape),
            _const_spec(b_ap.shape),
            _const_spec(lb_param.shape),
        ],
        out_specs=tuple(o[1] for o in outs),
        scratch_shapes=[pltpu.VMEM((tm, hi - lo), f32) for lo, hi in PROJ_GROUPS],
        compiler_params=pltpu.CompilerParams(
            dimension_semantics=("arbitrary",), vmem_limit_bytes=VMEM_LIMIT),
        name="proj",
    )(x2d, g_mix, w_in_p, w_a2p, b_ap, lb_param)


def _head_norm(o, onorm):
    return o * lax.rsqrt(jnp.mean(o * o, axis=-1, keepdims=True) + EPS) * onorm


def _tril(n):
    r = lax.broadcasted_iota(jnp.int32, (n, n), 0)
    c = lax.broadcasted_iota(jnp.int32, (n, n), 1)
    return r >= c


def _mixer_prompt_kernel(q32_ref, q64_ref, kin_ref, kx_ref, ko_ref, d_ref, v_ref, g_ref,
                         ona_ref, onb_ref, o_ref, sa_ref, sb_ref,
                         st_ref, att0_ref, att1_ref, up_ref, start_ref, *, tile, chunk):
    j = pl.program_id(1)

    @pl.when(j == 0)
    def _():
        st_ref[...] = jnp.zeros_like(st_ref)

    onorm = [ona_ref[...]] * H_A + [onb_ref[...]] * (N_HEADS - H_A)
    pair = 2 * chunk
    n_pairs = tile // pair
    tril = _tril(chunk)
    r = lax.broadcasted_iota(jnp.int32, (chunk, pair), 0)
    c = lax.broadcasted_iota(jnp.int32, (chunk, pair), 1)
    cross = (c < chunk) | (r >= c - chunk)

    for h in range(N_HEADS):
        sl = slice(h * HEAD, (h + 1) * HEAD)
        kl = slice(_key_lanes(h), _key_lanes(h) + HEAD)
        for s in range(n_pairs):
            u = h * n_pairs + s
            ra = slice(s * pair, s * pair + chunk)
            rb = slice(s * pair + chunk, (s + 1) * pair)
            rp = slice(s * pair, (s + 1) * pair)
            a_keys = kin_ref[s * chunk:(s + 1) * chunk, kl]
            att0_ref[u] = jnp.where(tril, _dot_nt(q32_ref[ra, sl], a_keys), 0.0).astype(bf16)
            att1_ref[u] = jnp.where(cross, _dot_nt(q32_ref[rb, sl], kx_ref[rp, kl]),
                                    0.0).astype(bf16)
            up_ref[u] = _dot_tn(v_ref[rp, sl], ko_ref[rp, kl])

    for h in range(N_HEADS):
        kl = slice(_key_lanes(h), _key_lanes(h) + HEAD)
        st = st_ref[h]
        for s in range(n_pairs):
            u = h * n_pairs + s
            start_ref[u] = st.astype(bf16)
            st = d_ref[s:s + 1, kl] * st + up_ref[u]
        st_ref[h] = st

    for h in range(N_HEADS):
        sl = slice(h * HEAD, (h + 1) * HEAD)
        outs = []
        for s in range(n_pairs):
            u = h * n_pairs + s
            ra = slice(s * pair, s * pair + chunk)
            rp = slice(s * pair, (s + 1) * pair)
            inter = _dot_nt(q64_ref[rp, sl], start_ref[u])
            outs.append(inter[:chunk]
                        + jnp.dot(att0_ref[u], v_ref[ra, sl], preferred_element_type=f32))
            outs.append(inter[chunk:]
                        + jnp.dot(att1_ref[u], v_ref[rp, sl], preferred_element_type=f32))
        o = jnp.concatenate(outs, axis=0)
        o_ref[:, sl] = (_head_norm(o, onorm[h]) * g_ref[:, sl]).astype(o_ref.dtype)

    @pl.when(j == pl.num_programs(1) - 1)
    def _():
        for h in range(H_A):
            sa_ref[0, h] = st_ref[h].T[_own_keys(h), :]
            sb_ref[0, h] = st_ref[H_A + h].T


def _mixer_prompt(acts, ona, onb, batch, seq, tile):
    nt = seq // tile
    chunk = PROMPT_CHUNK
    units = N_HEADS * (tile // (2 * chunk))
    kern = functools.partial(_mixer_prompt_kernel, tile=tile, chunk=chunk)

    def rows(width, per=1):
        return pl.BlockSpec((tile // per, width), lambda b, j: (b * nt + j, 0))

    return pl.pallas_call(
        kern,
        out_shape=(
            jax.ShapeDtypeStruct((batch * seq, D_MODEL), bf16),
            jax.ShapeDtypeStruct((batch, H_A, DK_A, HEAD), f32),
            jax.ShapeDtypeStruct((batch, H_A, HEAD, HEAD), f32),
        ),
        grid=(batch, nt),
        in_specs=[
            rows(D_MODEL), rows(D_MODEL), rows(K_W, 2), rows(K_W), rows(K_W),
            rows(K_W, 2 * chunk), rows(D_MODEL), rows(D_MODEL),
            _const_spec(ona.shape),
            _const_spec(onb.shape),
        ],
        out_specs=(
            rows(D_MODEL),
            pl.BlockSpec((1, H_A, DK_A, HEAD), lambda b, j: (b, 0, 0, 0)),
            pl.BlockSpec((1, H_A, HEAD, HEAD), lambda b, j: (b, 0, 0, 0)),
        ),
        scratch_shapes=[
            pltpu.VMEM((N_HEADS, HEAD, HEAD), f32),
            pltpu.VMEM((units, chunk, chunk), bf16),
            pltpu.VMEM((units, chunk, 2 * chunk), bf16),
            pltpu.VMEM((units, HEAD, HEAD), f32),
            pltpu.VMEM((units, HEAD, HEAD), bf16),
        ],
        compiler_params=pltpu.CompilerParams(
            dimension_semantics=("arbitrary", "arbitrary"), vmem_limit_bytes=VMEM_LIMIT),
        name="mixer_prompt",
    )(*acts, ona, onb)


def _mixer_sample_kernel(qi_ref, ki_ref, ko_ref, v_ref, g_ref, d_ref, sa_in_ref, sb_in_ref,
                         ona_ref, onb_ref, o_ref, sa_ref, sb_ref, att_ref, oi_ref, *, bt, seq):
    onorm = [ona_ref[...]] * H_A + [onb_ref[...]] * (N_HEADS - H_A)
    tril = _tril(seq)
    zpad = jnp.zeros((HEAD - DK_A, HEAD), f32)
    zeros = jnp.zeros((seq, HEAD), f32)
    row = lax.broadcasted_iota(jnp.int32, (seq, HEAD), 0)
    ones_rows = jnp.where(row < 2, 1.0, 0.0)
    rhs_bottom = jnp.concatenate([zeros, ones_rows], axis=1)

    for bi in range(bt):
        rows = slice(bi * seq, (bi + 1) * seq)
        decay = d_ref[bi:bi + 1, :]
        d_hi = decay.astype(bf16).astype(f32)
        d_lo = decay - d_hi
        for h in range(N_HEADS):
            sl = slice(h * HEAD, (h + 1) * HEAD)
            kl = slice(_key_lanes(h), _key_lanes(h) + HEAD)
            q_in = qi_ref[rows, sl].astype(bf16)
            k_in = ki_ref[rows, kl].astype(bf16)
            v32 = v_ref[rows, sl]
            if h >= H_A:
                s = sb_in_ref[bi, h - H_A]
            elif h % 2 == 0:
                s = jnp.concatenate([sa_in_ref[bi, h], zpad], axis=0)
            else:
                s = jnp.concatenate([zpad, sa_in_ref[bi, h]], axis=0)
            att_ref[bi * N_HEADS + h] = jnp.where(tril, _dot_nt(q_in, k_in), 0.0)
            oi_ref[rows, sl] = jnp.dot(q_in, s.astype(bf16), preferred_element_type=f32)
            d_rows = jnp.where(row == 0, d_hi[:, kl], jnp.where(row == 1, d_lo[:, kl], 0.0))
            lhs = jnp.concatenate([ko_ref[rows, kl], d_rows], axis=0).astype(bf16)
            rhs = jnp.concatenate(
                [jnp.concatenate([v32, zeros], axis=1), rhs_bottom], axis=0).astype(bf16)
            upd = _dot_tn(lhs, rhs)
            s_new = upd[:, HEAD:] * s + upd[:, :HEAD]
            if h < H_A:
                sa_ref[bi, h] = s_new[_own_keys(h), :]
            else:
                sb_ref[bi, h - H_A] = s_new

    for bi in range(bt):
        rows = slice(bi * seq, (bi + 1) * seq)
        for h in range(N_HEADS):
            sl = slice(h * HEAD, (h + 1) * HEAD)
            att = att_ref[bi * N_HEADS + h].astype(bf16)
            o = oi_ref[rows, sl] + jnp.dot(att, v_ref[rows, sl].astype(bf16),
                                           preferred_element_type=f32)
            o_ref[rows, sl] = (_head_norm(o, onorm[h]) * g_ref[rows, sl]).astype(o_ref.dtype)


def _mixer_sample(acts, sa_in, sb_in, ona, onb, batch, seq, bt):
    kern = functools.partial(_mixer_sample_kernel, bt=bt, seq=seq)

    def rows(width):
        return pl.BlockSpec((bt * seq, width), lambda i: (i, 0))

    sa_spec = pl.BlockSpec((bt, H_A, DK_A, HEAD), lambda i: (i, 0, 0, 0))
    sb_spec = pl.BlockSpec((bt, H_A, HEAD, HEAD), lambda i: (i, 0, 0, 0))
    return pl.pallas_call(
        kern,
        out_shape=(
            jax.ShapeDtypeStruct((batch * seq, D_MODEL), f32),
            jax.ShapeDtypeStruct((batch, H_A, DK_A, HEAD), f32),
            jax.ShapeDtypeStruct((batch, H_A, HEAD, HEAD), f32),
        ),
        grid=(batch // bt,),
        in_specs=[
            rows(D_MODEL), rows(K_W), rows(K_W), rows(D_MODEL), rows(D_MODEL),
            pl.BlockSpec((bt, K_W), lambda i: (i, 0)),
            sa_spec, sb_spec,
            _const_spec(ona.shape),
            _const_spec(onb.shape),
        ],
        out_specs=(rows(D_MODEL), sa_spec, sb_spec),
        scratch_shapes=[
            pltpu.VMEM((bt * N_HEADS, seq, seq), f32),
            pltpu.VMEM((bt * seq, D_MODEL), f32),
        ],
        compiler_params=pltpu.CompilerParams(
            dimension_semantics=("arbitrary",), vmem_limit_bytes=VMEM_LIMIT),
        name="mixer_sample",
    )(*acts, sa_in, sb_in, ona, onb)


def _out_kernel(o_ref, x_ref, w_o_ref, gf_ref, w1_ref, w3_ref, w2_ref, gl_ref, y_ref):
    x1 = x_ref[...] + jnp.dot(o_ref[...].astype(bf16), w_o_ref[...], preferred_element_type=f32)
    h = _rmsnorm(x1, gf_ref[...]).astype(bf16)
    acc = x1
    for c in range(D_FF // FF_CHUNK):
        cols = slice(c * FF_CHUNK, (c + 1) * FF_CHUNK)
        a = jnp.dot(h, w1_ref[:, cols], preferred_element_type=f32)
        b = jnp.dot(h, w3_ref[:, cols], preferred_element_type=f32)
        u = (_silu(a) * b).astype(bf16)
        acc = acc + jnp.dot(u, w2_ref[cols, :], preferred_element_type=f32)
    y_ref[...] = _rmsnorm(acc, gl_ref[...])


def _out(o2d, x2d, w_o, g_ffn, w1, w3, w2, g_final, tm):
    n = x2d.shape[0]
    return pl.pallas_call(
        _out_kernel,
        out_shape=jax.ShapeDtypeStruct((n, D_MODEL), f32),
        grid=(n // tm,),
        in_specs=[
            pl.BlockSpec((tm, D_MODEL), lambda i: (i, 0)),
            pl.BlockSpec((tm, D_MODEL), lambda i: (i, 0)),
            _const_spec((D_MODEL, D_MODEL)),
            _const_spec((1, D_MODEL)),
            _const_spec((D_MODEL, D_FF)),
            _const_spec((D_MODEL, D_FF)),
            _const_spec((D_FF, D_MODEL)),
            _const_spec((1, D_MODEL)),
        ],
        out_specs=pl.BlockSpec((tm, D_MODEL), lambda i: (i, 0)),
        compiler_params=pltpu.CompilerParams(
            dimension_semantics=("arbitrary",), vmem_limit_bytes=VMEM_LIMIT),
        name="out_ffn",
    )(o2d, x2d, w_o, g_ffn, w1, w3, w2, g_final)


def _layout_w_in(w_in):
    sizes = (KA_W, KA_W, SEC, SEC, GATE_RANK, SEC, SEC, SEC, SEC)
    parts, acc = [], 0
    for s in sizes:
        parts.append(w_in[:, acc:acc + s])
        acc += s
    qa, ka, va, ga, lra, qb, fb, ib, gb = parts
    lra = jnp.pad(lra, ((0, 0), (0, HEAD - GATE_RANK)))
    return jnp.concatenate([lra, qa, ka, va, ga, fb, qb, ib, gb], axis=1)


def kernel(x_prompt, x_sample, state_gla, state_hgrn, norm_mix, w_in, w_a2, b_a, lb_param,
           onorm_a, onorm_b, w_o, norm_ffn, w1, w3, w2, norm_final):
    batch, seq, _ = x_prompt.shape
    dec_batch, dec_seq, _ = x_sample.shape

    w_in_p = _layout_w_in(w_in[0]).astype(bf16)
    w_a2p = jnp.pad(w_a2[0], ((0, HEAD - GATE_RANK), (0, 0))).astype(bf16)
    b_ap = b_a[0].reshape(1, KA_W)
    g_mix = norm_mix[0].reshape(1, D_MODEL)
    g_ffn = norm_ffn[0].reshape(1, D_MODEL)
    g_fin = norm_final.reshape(1, D_MODEL)
    ona = onorm_a[0].reshape(1, HEAD)
    onb = onorm_b[0].reshape(1, HEAD)
    w_o_b = w_o[0].astype(bf16)
    w1_b = w1[0].astype(bf16)
    w3_b = w3[0].astype(bf16)
    w2_b = w2[0].astype(bf16)

    xp = x_prompt.reshape(batch * seq, D_MODEL)
    xs = x_sample.reshape(dec_batch * dec_seq, D_MODEL)

    acts_p = _proj(xp, g_mix, w_in_p, w_a2p, b_ap, lb_param,
                   tm=512, chunk=PROMPT_CHUNK, paired=True)
    op, sa_p, sb_p = _mixer_prompt(acts_p, ona, onb, batch, seq, tile=512)
    yp = _out(op, xp, w_o_b, g_ffn, w1_b, w3_b, w2_b, g_fin, tm=512)

    q_s, kin_s, ko_s, d_s, v_s, g_s = _proj(xs, g_mix, w_in_p, w_a2p, b_ap, lb_param,
                                            tm=512, chunk=dec_seq, paired=False)
    os_, sa_s, sb_s = _mixer_sample((q_s, kin_s, ko_s, v_s, g_s, d_s), state_gla[0],
                                    state_hgrn[0], ona, onb, dec_batch, dec_seq, bt=8)
    ys = _out(os_, xs, w_o_b, g_ffn, w1_b, w3_b, w2_b, g_fin, tm=512)

    return (yp.reshape(batch, seq, D_MODEL), ys.reshape(dec_batch, dec_seq, D_MODEL),
            sa_p[None], sb_p[None], sa_s[None], sb_s[None])
```

```python
import functools

import jax
import jax.numpy as jnp
from jax import lax
from jax.experimental import pallas as pl
from jax.experimental.pallas import tpu as pltpu

D_MODEL = 1024
N_HEADS = 8
H_A = 4
DK_A = 64
HEAD = 128
GATE_RANK = 16
GATE_NORM = 16.0
D_FF = 2816
FF_CHUNK = 256
EPS = 1e-6
PROMPT_CHUNK = 32
SUBLANES = 8
BF16_ROWS = 16

KA_W = H_A * DK_A
SEC = H_A * HEAD
K_W = KA_W + SEC
OFF_LR = 0
OFF_QA = HEAD
OFF_KA = OFF_QA + KA_W
OFF_VA = OFF_KA + KA_W
OFF_GA = OFF_VA + SEC
OFF_FB = OFF_GA + SEC
OFF_QB = OFF_FB + SEC
OFF_IB = OFF_QB + SEC
OFF_GB = OFF_IB + SEC
D_P = OFF_GB + SEC
MXU_N = 256
PROJ_GROUPS = ((OFF_LR, OFF_VA), (OFF_FB, OFF_IB), (OFF_VA, OFF_FB), (OFF_IB, D_P))

VMEM_LIMIT = 56 * 1024 * 1024

f32 = jnp.float32
bf16 = jnp.bfloat16


def _const_spec(shape):
    nd = len(shape)
    return pl.BlockSpec(shape, lambda *_: (0,) * nd, pipeline_mode=pl.Buffered(1))


def _rmsnorm(x, g):
    return x * lax.rsqrt(jnp.mean(x * x, axis=-1, keepdims=True) + EPS) * g


def _sigmoid(x):
    return 0.5 * jnp.tanh(0.5 * x) + 0.5


def _silu(x):
    hx = 0.5 * x
    return hx + hx * jnp.tanh(hx)


def _dot_nt(a, b):
    return lax.dot_general(a, b, (((1,), (1,)), ((), ())), preferred_element_type=f32)


def _dot_tn(a, b):
    return lax.dot_general(a, b, (((0,), (0,)), ((), ())), preferred_element_type=f32)


def _key_lanes(h):
    return HEAD * (h // 2) if h < H_A else KA_W + HEAD * (h - H_A)


def _own_keys(h):
    return slice(DK_A * (h % 2), DK_A * (h % 2 + 1)) if h < H_A else slice(0, HEAD)


def _cumsum_rows(x, chunk):
    rows, width = x.shape
    row = lax.broadcasted_iota(jnp.int32, (SUBLANES, width), 0)
    groups_per_chunk = chunk // SUBLANES
    out = []
    carry = None
    for g in range(rows // SUBLANES):
        y = x[g * SUBLANES:(g + 1) * SUBLANES]
        for s in (1, 2, 4):
            y = y + jnp.where(row >= s, pltpu.roll(y, s, 0), 0.0)
        if g % groups_per_chunk != 0:
            y = y + carry
        carry = y[SUBLANES - 1:SUBLANES]
        out.append(y)
    return out[0] if len(out) == 1 else jnp.concatenate(out, axis=0)


def _lower_bound(lbp_ref):
    lbp = lbp_ref[...]
    m = jnp.max(lbp, axis=0, keepdims=True)
    e = jnp.exp(lbp - m)
    return e[0:1] / jnp.sum(e, axis=0, keepdims=True)


def _decays(la, chunk):
    rows = la.shape[0]
    b = _cumsum_rows(la, chunk)
    n_chunks = rows // chunk
    lasts = [b[(c + 1) * chunk - 1:(c + 1) * chunk] for c in range(n_chunks)]
    if n_chunks == 1:
        b_last = lasts[0]
        b_last_rows = b_last
    else:
        b_last = jnp.concatenate(lasts, axis=0)
        b_last_rows = jnp.concatenate(
            [jnp.broadcast_to(l, (chunk, l.shape[1])) for l in lasts], axis=0)
    return jnp.exp(b), jnp.exp(-b), jnp.exp(b_last_rows - b), jnp.exp(b_last)


def _emit_keys_paired(refs, r, q_targets, kl, q, k, la):
    q32_ref, q64_ref, kin_ref, kx_ref, ko_ref, d_ref = refs
    c = PROMPT_CHUNK
    rows = slice(2 * c * r, 2 * c * (r + 1))
    e_b, e_nb, e_out, decay = _decays(la, c)
    d0, d1 = decay[0:1], decay[1:2]
    q32 = q * e_b
    q64 = jnp.concatenate([q32[:c], q32[c:] * d0], axis=0)
    k_in = k * e_nb
    k_out = k * e_out
    for ql, own in q_targets:
        q32_ref[rows, ql] = (q32 if own is None else jnp.where(own, q32, 0.0)).astype(bf16)
        q64_ref[rows, ql] = (q64 if own is None else jnp.where(own, q64, 0.0)).astype(bf16)
    kin_ref[c * r:c * (r + 1), kl] = k_in[:c].astype(bf16)
    kx_ref[rows, kl] = jnp.concatenate([k_out[:c], k_in[c:]], axis=0).astype(bf16)
    ko_ref[rows, kl] = jnp.concatenate([k_out[:c] * d1, k_out[c:]], axis=0).astype(bf16)
    d_ref[r:r + 1, kl] = d0 * d1


def _emit_keys_single(refs, r, q_targets, kl, q, k, la, *, chunk):
    q_ref, kin_ref, ko_ref, d_ref = refs
    rows = slice(SUBLANES * chunk * r, SUBLANES * chunk * (r + 1))
    e_b, e_nb, e_out, decay = _decays(la, chunk)
    q_in = q * e_b
    for ql, own in q_targets:
        q_ref[rows, ql] = q_in if own is None else jnp.where(own, q_in, 0.0)
    kin_ref[rows, kl] = k * e_nb
    ko_ref[rows, kl] = k * e_out
    d_ref[SUBLANES * r:SUBLANES * (r + 1), kl] = decay


def _proj_kernel(x_ref, gm_ref, w_ref, w_a2_ref, b_a_ref, lbp_ref, *refs, chunk, paired):
    n_key_refs = 6 if paired else 4
    key_refs = refs[:n_key_refs]
    v_ref, g_ref = refs[n_key_refs:n_key_refs + 2]
    p_refs = refs[n_key_refs + 2:]
    tm = x_ref.shape[0]
    block = 2 * chunk if paired else SUBLANES * chunk
    emit_keys = (_emit_keys_paired if paired
                 else functools.partial(_emit_keys_single, chunk=chunk))
    blocks = [(slice(r * block, (r + 1) * block), r) for r in range(tm // block)]
    h = _rmsnorm(x_ref[...], gm_ref[...]).astype(bf16)

    w_a2 = w_a2_ref[...]
    b_a = b_a_ref[...]
    lb = _lower_bound(lbp_ref)
    lane = lax.broadcasted_iota(jnp.int32, (block, HEAD), 1)

    def p_slot(lo, hi):
        for (g_lo, g_hi), ref in zip(PROJ_GROUPS, p_refs):
            if g_lo <= lo and hi <= g_hi:
                return ref, slice(lo - g_lo, hi - g_lo)
        raise ValueError((lo, hi))

    def project(lo, hi):
        ref, cols = p_slot(lo, hi)
        ref[:, cols] = jnp.dot(h, w_ref[:, lo:hi], preferred_element_type=f32)

    def p(rows, lo, width):
        ref, cols = p_slot(lo, lo + width)
        return ref[rows, cols]

    def gla_keys(rows, r):
        lra = p(rows, OFF_LR, HEAD)
        z = jnp.dot(lra.astype(bf16), w_a2, preferred_element_type=f32) + b_a
        for pr in range(H_A // 2):
            kl = slice(pr * HEAD, (pr + 1) * HEAD)
            zz = z[:, kl]
            la = (jnp.minimum(zz, 0.0) - jnp.log(1.0 + jnp.exp(-jnp.abs(zz)))) * (1.0 / GATE_NORM)
            q = p(rows, OFF_QA + pr * HEAD, HEAD) * (DK_A ** -0.5)
            k = p(rows, OFF_KA + pr * HEAD, HEAD)
            q_targets = [(slice(hd * HEAD, (hd + 1) * HEAD), own)
                         for hd, own in ((2 * pr, lane < DK_A), (2 * pr + 1, lane >= DK_A))]
            emit_keys(key_refs, r, q_targets, kl, q, k, la)

    def hgrn_keys(rows, r):
        for hd in range(N_HEADS - H_A):
            lbh = lb[:, hd * HEAD:(hd + 1) * HEAD]
            f = lbh + (1.0 - lbh) * _sigmoid(p(rows, OFF_FB + hd * HEAD, HEAD))
            q = _silu(p(rows, OFF_QB + hd * HEAD, HEAD))
            ql = slice(SEC + hd * HEAD, SEC + (hd + 1) * HEAD)
            kl = slice(KA_W + hd * HEAD, KA_W + (hd + 1) * HEAD)
            emit_keys(key_refs, r, [(ql, None)], kl, q, 1.0 - f, jnp.log(f))

    def gla_values(rows, crow):
        v_ref[rows, 0:SEC] = p(rows, OFF_VA, SEC).astype(v_ref.dtype)
        g_ref[rows, 0:SEC] = p(rows, OFF_GA, SEC)

    def hgrn_values(rows, crow):
        v_ref[rows, SEC:2 * SEC] = p(rows, OFF_IB, SEC).astype(v_ref.dtype)
        g_ref[rows, SEC:2 * SEC] = p(rows, OFF_GB, SEC)

    def tasks(fn, which):
        return [functools.partial(fn, *blocks[r]) for r in which]

    nb = len(blocks)
    first, second = range(nb // 2), range(nb // 2, nb)
    windows = (
        [],
        tasks(gla_keys, range(nb)),
        tasks(hgrn_keys, first),
        tasks(hgrn_keys, second) + tasks(gla_values, range(nb)),
    )
    for (lo, hi), work in zip(PROJ_GROUPS, windows):
        starts = list(range(lo, hi, MXU_N))
        for i, c0 in enumerate(starts):
            project(c0, min(c0 + MXU_N, hi))
            for t in work[i * len(work) // len(starts):(i + 1) * len(work) // len(starts)]:
                t()
    for t in tasks(hgrn_values, range(nb)):
        t()


def _proj(x2d, g_mix, w_in_p, w_a2p, b_ap, lb_param, *, tm, chunk, paired):
    n = x2d.shape[0]
    kern = functools.partial(_proj_kernel, chunk=chunk, paired=paired)

    def arr(rows, width, dtype):
        return (jax.ShapeDtypeStruct((n // rows, width), dtype),
                pl.BlockSpec((tm // rows, width), lambda i: (i, 0)))

    if paired:
        keys = [arr(1, D_MODEL, bf16), arr(1, D_MODEL, bf16), arr(2, K_W, bf16),
                arr(1, K_W, bf16), arr(1, K_W, bf16), arr(2 * chunk, K_W, f32)]
        v_out = arr(1, D_MODEL, bf16)
    else:
        keys = [arr(1, D_MODEL, f32), arr(1, K_W, f32), arr(1, K_W, f32), arr(chunk, K_W, f32)]
        v_out = arr(1, D_MODEL, f32)
    outs = keys + [v_out, arr(1, D_MODEL, f32)]

    return pl.pallas_call(
        kern,
        out_shape=tuple(o[0] for o in outs),
        grid=(n // tm,),
        in_specs=[
            pl.BlockSpec((tm, D_MODEL), lambda i: (i, 0)),
            _const_spec((1, D_MODEL)),
            _const_spec((D_MODEL, D_P)),
            _const_spec(w_a2p.shape),
            _const_spec(b_ap.shape),
            _const_spec(lb_param.shape),
        ],
        out_specs=tuple(o[1] for o in outs),
        scratch_shapes=[pltpu.VMEM((tm, hi - lo), f32) for lo, hi in PROJ_GROUPS],
        compiler_params=pltpu.CompilerParams(
            dimension_semantics=("arbitrary",), vmem_limit_bytes=VMEM_LIMIT),
        name="proj",
    )(x2d, g_mix, w_in_p, w_a2p, b_ap, lb_param)


def _head_norm(o, onorm):
    return o * lax.rsqrt(jnp.mean(o * o, axis=-1, keepdims=True) + EPS) * onorm


def _tril(n):
    r = lax.broadcasted_iota(jnp.int32, (n, n), 0)
    c = lax.broadcasted_iota(jnp.int32, (n, n), 1)
    return r >= c


def _mixer_prompt_kernel(q32_ref, q64_ref, kin_ref, kx_ref, ko_ref, d_ref, v_ref, g_ref,
                         ona_ref, onb_ref, o_ref, sa_ref, sb_ref,
                         st_ref, att0_ref, att1_ref, up_ref, start_ref, *, tile, chunk):
    j = pl.program_id(1)

    @pl.when(j == 0)
    def _():
        st_ref[...] = jnp.zeros_like(st_ref)

    onorm = [ona_ref[...]] * H_A + [onb_ref[...]] * (N_HEADS - H_A)
    pair = 2 * chunk
    n_pairs = tile // pair
    tril = _tril(chunk)
    r = lax.broadcasted_iota(jnp.int32, (chunk, pair), 0)
    c = lax.broadcasted_iota(jnp.int32, (chunk, pair), 1)
    cross = (c < chunk) | (r >= c - chunk)

    for h in range(N_HEADS):
        sl = slice(h * HEAD, (h + 1) * HEAD)
        kl = slice(_key_lanes(h), _key_lanes(h) + HEAD)
        for s in range(n_pairs):
            u = h * n_pairs + s
            ra = slice(s * pair, s * pair + chunk)
            rb = slice(s * pair + chunk, (s + 1) * pair)
            rp = slice(s * pair, (s + 1) * pair)
            a_keys = kin_ref[s * chunk:(s + 1) * chunk, kl]
            att0_ref[u] = jnp.where(tril, _dot_nt(q32_ref[ra, sl], a_keys), 0.0).astype(bf16)
            att1_ref[u] = jnp.where(cross, _dot_nt(q32_ref[rb, sl], kx_ref[rp, kl]),
                                    0.0).astype(bf16)
            up_ref[u] = _dot_tn(v_ref[rp, sl], ko_ref[rp, kl])

    for h in range(N_HEADS):
        kl = slice(_key_lanes(h), _key_lanes(h) + HEAD)
        st = st_ref[h]
        for s in range(n_pairs):
            u = h * n_pairs + s
            start_ref[u] = st.astype(bf16)
            st = d_ref[s:s + 1, kl] * st + up_ref[u]
        st_ref[h] = st

    for h in range(N_HEADS):
        sl = slice(h * HEAD, (h + 1) * HEAD)
        outs = []
        for s in range(n_pairs):
            u = h * n_pairs + s
            ra = slice(s * pair, s * pair + chunk)
            rp = slice(s * pair, (s + 1) * pair)
            inter = _dot_nt(q64_ref[rp, sl], start_ref[u])
            outs.append(inter[:chunk]
                        + jnp.dot(att0_ref[u], v_ref[ra, sl], preferred_element_type=f32))
            outs.append(inter[chunk:]
                        + jnp.dot(att1_ref[u], v_ref[rp, sl], preferred_element_type=f32))
        o = jnp.concatenate(outs, axis=0)
        o_ref[:, sl] = (_head_norm(o, onorm[h]) * _silu(g_ref[:, sl])).astype(o_ref.dtype)

    @pl.when(j == pl.num_programs(1) - 1)
    def _():
        for h in range(H_A):
            sa_ref[0, h] = st_ref[h].T[_own_keys(h), :]
            sb_ref[0, h] = st_ref[H_A + h].T


def _mixer_prompt(acts, ona, onb, batch, seq, tile):
    nt = seq // tile
    chunk = PROMPT_CHUNK
    units = N_HEADS * (tile // (2 * chunk))
    kern = functools.partial(_mixer_prompt_kernel, tile=tile, chunk=chunk)

    def rows(width, per=1):
        return pl.BlockSpec((tile // per, width), lambda b, j: (b * nt + j, 0))

    return pl.pallas_call(
        kern,
        out_shape=(
            jax.ShapeDtypeStruct((batch * seq, D_MODEL), bf16),
            jax.ShapeDtypeStruct((batch, H_A, DK_A, HEAD), f32),
            jax.ShapeDtypeStruct((batch, H_A, HEAD, HEAD), f32),
        ),
        grid=(batch, nt),
        in_specs=[
            rows(D_MODEL), rows(D_MODEL), rows(K_W, 2), rows(K_W), rows(K_W),
            rows(K_W, 2 * chunk), rows(D_MODEL), rows(D_MODEL),
            _const_spec(ona.shape),
            _const_spec(onb.shape),
        ],
        out_specs=(
            rows(D_MODEL),
            pl.BlockSpec((1, H_A, DK_A, HEAD), lambda b, j: (b, 0, 0, 0)),
            pl.BlockSpec((1, H_A, HEAD, HEAD), lambda b, j: (b, 0, 0, 0)),
        ),
        scratch_shapes=[
            pltpu.VMEM((N_HEADS, HEAD, HEAD), f32),
            pltpu.VMEM((units, chunk, chunk), bf16),
            pltpu.VMEM((units, chunk, 2 * chunk), bf16),
            pltpu.VMEM((units, HEAD, HEAD), f32),
            pltpu.VMEM((units, HEAD, HEAD), bf16),
        ],
        compiler_params=pltpu.CompilerParams(
            dimension_semantics=("arbitrary", "arbitrary"), vmem_limit_bytes=VMEM_LIMIT),
        name="mixer_prompt",
    )(*acts, ona, onb)


def _mixer_sample_kernel(qi_ref, ki_ref, ko_ref, v_ref, g_ref, d_ref, sa_in_ref, sb_in_ref,
                         ona_ref, onb_ref, o_ref, sa_ref, sb_ref, att_ref, oi_ref, *, bt, seq):
    onorm = [ona_ref[...]] * H_A + [onb_ref[...]] * (N_HEADS - H_A)
    tril = _tril(seq)
    zpad = jnp.zeros((HEAD - DK_A, HEAD), f32)
    zeros = jnp.zeros((seq, HEAD), f32)
    row = lax.broadcasted_iota(jnp.int32, (seq, HEAD), 0)
    ones_rows = jnp.where(row < 2, 1.0, 0.0)
    rhs_bottom = jnp.concatenate([zeros, ones_rows], axis=1)

    for bi in range(bt):
        rows = slice(bi * seq, (bi + 1) * seq)
        decay = d_ref[bi:bi + 1, :]
        d_hi = decay.astype(bf16).astype(f32)
        d_lo = decay - d_hi
        for h in range(N_HEADS):
            sl = slice(h * HEAD, (h + 1) * HEAD)
            kl = slice(_key_lanes(h), _key_lanes(h) + HEAD)
            q_in = qi_ref[rows, sl].astype(bf16)
            k_in = ki_ref[rows, kl].astype(bf16)
            v32 = v_ref[rows, sl]
            if h >= H_A:
                s = sb_in_ref[bi, h - H_A]
            elif h % 2 == 0:
                s = jnp.concatenate([sa_in_ref[bi, h], zpad], axis=0)
            else:
                s = jnp.concatenate([zpad, sa_in_ref[bi, h]], axis=0)
            att_ref[bi * N_HEADS + h] = jnp.where(tril, _dot_nt(q_in, k_in), 0.0)
            oi_ref[rows, sl] = jnp.dot(q_in, s.astype(bf16), preferred_element_type=f32)
            d_rows = jnp.where(row == 0, d_hi[:, kl], jnp.where(row == 1, d_lo[:, kl], 0.0))
            lhs = jnp.concatenate([ko_ref[rows, kl], d_rows], axis=0).astype(bf16)
            rhs = jnp.concatenate(
                [jnp.concatenate([v32, zeros], axis=1), rhs_bottom], axis=0).astype(bf16)
            upd = _dot_tn(lhs, rhs)
            s_new = upd[:, HEAD:] * s + upd[:, :HEAD]
            if h < H_A:
                sa_ref[bi, h] = s_new[_own_keys(h), :]
            else:
                sb_ref[bi, h - H_A] = s_new

    for bi in range(bt):
        rows = slice(bi * seq, (bi + 1) * seq)
        for h in range(N_HEADS):
            sl = slice(h * HEAD, (h + 1) * HEAD)
            att = att_ref[bi * N_HEADS + h].astype(bf16)
            o = oi_ref[rows, sl] + jnp.dot(att, v_ref[rows, sl].astype(bf16),
                                           preferred_element_type=f32)
            o_ref[rows, sl] = (_head_norm(o, onorm[h]) * _silu(g_ref[rows, sl])).astype(
                o_ref.dtype)


def _mixer_sample(acts, sa_in, sb_in, ona, onb, batch, seq, bt):
    kern = functools.partial(_mixer_sample_kernel, bt=bt, seq=seq)

    def rows(width):
        return pl.BlockSpec((bt * seq, width), lambda i: (i, 0))

    sa_spec = pl.BlockSpec((bt, H_A, DK_A, HEAD), lambda i: (i, 0, 0, 0))
    sb_spec = pl.BlockSpec((bt, H_A, HEAD, HEAD), lambda i: (i, 0, 0, 0))
    return pl.pallas_call(
        kern,
        out_shape=(
            jax.ShapeDtypeStruct((batch * seq, D_MODEL), f32),
            jax.ShapeDtypeStruct((batch, H_A, DK_A, HEAD), f32),
            jax.ShapeDtypeStruct((batch, H_A, HEAD, HEAD), f32),
        ),
        grid=(batch // bt,),
        in_specs=[
            rows(D_MODEL), rows(K_W), rows(K_W), rows(D_MODEL), rows(D_MODEL),
            pl.BlockSpec((bt, K_W), lambda i: (i, 0)),
            sa_spec, sb_spec,
            _const_spec(ona.shape),
            _const_spec(onb.shape),
        ],
        out_specs=(rows(D_MODEL), sa_spec, sb_spec),
        scratch_shapes=[
            pltpu.VMEM((bt * N_HEADS, seq, seq), f32),
            pltpu.VMEM((bt * seq, D_MODEL), f32),
        ],
        compiler_params=pltpu.CompilerParams(
            dimension_semantics=("arbitrary",), vmem_limit_bytes=VMEM_LIMIT),
        name="mixer_sample",
    )(*acts, sa_in, sb_in, ona, onb)


def _swiglu_out(x1, h, w1_ref, w3_ref, w2_ref, gl_ref, y_ref):
    acc = x1
    for c in range(D_FF // FF_CHUNK):
        cols = slice(c * FF_CHUNK, (c + 1) * FF_CHUNK)
        a = jnp.dot(h, w1_ref[:, cols], preferred_element_type=f32)
        b = jnp.dot(h, w3_ref[:, cols], preferred_element_type=f32)
        u = (_silu(a) * b).astype(bf16)
        acc = acc + jnp.dot(u, w2_ref[cols, :], preferred_element_type=f32)
    y_ref[...] = _rmsnorm(acc, gl_ref[...])


def _out_kernel(o_ref, x_ref, w_o_ref, gf_ref, w1_ref, w3_ref, w2_ref, gl_ref, y_ref):
    x1 = x_ref[...] + jnp.dot(o_ref[...].astype(bf16), w_o_ref[...], preferred_element_type=f32)
    h = _rmsnorm(x1, gf_ref[...]).astype(bf16)
    _swiglu_out(x1, h, w1_ref, w3_ref, w2_ref, gl_ref, y_ref)


def _out(o2d, x2d, w_o, g_ffn, w1, w3, w2, g_final, tm):
    n = x2d.shape[0]
    row_spec = pl.BlockSpec((tm, D_MODEL), lambda i: (i, 0))
    return pl.pallas_call(
        _out_kernel,
        out_shape=jax.ShapeDtypeStruct((n, D_MODEL), f32),
        grid=(n // tm,),
        in_specs=[
            row_spec, row_spec,
            _const_spec((D_MODEL, D_MODEL)),
            _const_spec((1, D_MODEL)),
            _const_spec((D_MODEL, D_FF)),
            _const_spec((D_MODEL, D_FF)),
            _const_spec((D_FF, D_MODEL)),
            _const_spec((1, D_MODEL)),
        ],
        out_specs=row_spec,
        compiler_params=pltpu.CompilerParams(
            dimension_semantics=("arbitrary",), vmem_limit_bytes=VMEM_LIMIT),
        name="out_ffn",
    )(o2d, x2d, w_o, g_ffn, w1, w3, w2, g_final)


def _layout_w_in(w_in):
    sizes = (KA_W, KA_W, SEC, SEC, GATE_RANK, SEC, SEC, SEC, SEC)
    parts, acc = [], 0
    for s in sizes:
        parts.append(w_in[:, acc:acc + s])
        acc += s
    qa, ka, va, ga, lra, qb, fb, ib, gb = parts
    lra = jnp.pad(lra, ((0, 0), (0, HEAD - GATE_RANK)))
    return jnp.concatenate([lra, qa, ka, va, ga, fb, qb, ib, gb], axis=1)


def kernel(x_prompt, x_sample, state_gla, state_hgrn, norm_mix, w_in, w_a2, b_a, lb_param,
           onorm_a, onorm_b, w_o, norm_ffn, w1, w3, w2, norm_final):
    batch, seq, _ = x_prompt.shape
    dec_batch, dec_seq, _ = x_sample.shape

    w_in_p = _layout_w_in(w_in[0]).astype(bf16)
    w_a2p = jnp.pad(w_a2[0], ((0, HEAD - GATE_RANK), (0, 0))).astype(bf16)
    b_ap = b_a[0].reshape(1, KA_W)
    g_mix = norm_mix[0].reshape(1, D_MODEL)
    g_ffn = norm_ffn[0].reshape(1, D_MODEL)
    g_fin = norm_final.reshape(1, D_MODEL)
    ona = onorm_a[0].reshape(1, HEAD)
    onb = onorm_b[0].reshape(1, HEAD)
    w_o_b = w_o[0].astype(bf16)
    w1_b = w1[0].astype(bf16)
    w3_b = w3[0].astype(bf16)
    w2_b = w2[0].astype(bf16)

    xp = x_prompt.reshape(batch * seq, D_MODEL)
    xs = x_sample.reshape(dec_batch * dec_seq, D_MODEL)

    acts_p = _proj(xp, g_mix, w_in_p, w_a2p, b_ap, lb_param,
                   tm=512, chunk=PROMPT_CHUNK, paired=True)
    op, sa_p, sb_p = _mixer_prompt(acts_p, ona, onb, batch, seq, tile=512)
    yp = _out(op, xp, w_o_b, g_ffn, w1_b, w3_b, w2_b, g_fin, tm=512)

    q_s, kin_s, ko_s, d_s, v_s, g_s = _proj(xs, g_mix, w_in_p, w_a2p, b_ap, lb_param,
                                            tm=512, chunk=dec_seq, paired=False)
    os_, sa_s, sb_s = _mixer_sample((q_s, kin_s, ko_s, v_s, g_s, d_s), state_gla[0],
                                    state_hgrn[0], ona, onb, dec_batch, dec_seq, bt=16)
    ys = _out(os_, xs, w_o_b, g_ffn, w1_b, w3_b, w2_b, g_fin, tm=512)

    return (yp.reshape(batch, seq, D_MODEL), ys.reshape(dec_batch, dec_seq, D_MODEL),
            sa_p[None], sb_p[None], sa_s[None], sb_s[None])
```

```python
import functools

import jax
import jax.numpy as jnp
from jax import lax
from jax.experimental import pallas as pl
from jax.experimental.pallas import tpu as pltpu

D_MODEL = 1024
N_HEADS = 8
H_A = 4
DK_A = 64
HEAD = 128
GATE_RANK = 16
GATE_NORM = 16.0
D_FF = 2816
FF_CHUNK = 256
EPS = 1e-6
PROMPT_CHUNK = 32
SUBLANES = 8
BF16_ROWS = 16

KA_W = H_A * DK_A
SEC = H_A * HEAD
K_W = KA_W + SEC
OFF_LR = 0
OFF_QA = HEAD
OFF_KA = OFF_QA + KA_W
OFF_VA = OFF_KA + KA_W
OFF_GA = OFF_VA + SEC
OFF_FB = OFF_GA + SEC
OFF_QB = OFF_FB + SEC
OFF_IB = OFF_QB + SEC
OFF_GB = OFF_IB + SEC
D_P = OFF_GB + SEC
MXU_N = 256
PROJ_GROUPS = ((OFF_LR, OFF_VA), (OFF_FB, OFF_IB), (OFF_VA, OFF_FB), (OFF_IB, D_P))

VMEM_LIMIT = 56 * 1024 * 1024

f32 = jnp.float32
bf16 = jnp.bfloat16


def _const_spec(shape):
    nd = len(shape)
    return pl.BlockSpec(shape, lambda *_: (0,) * nd, pipeline_mode=pl.Buffered(1))


def _rmsnorm(x, g):
    return x * lax.rsqrt(jnp.mean(x * x, axis=-1, keepdims=True) + EPS) * g


def _sigmoid(x):
    return 0.5 * jnp.tanh(0.5 * x) + 0.5


def _silu(x):
    hx = 0.5 * x
    return hx + hx * jnp.tanh(hx)


def _dot_nt(a, b):
    return lax.dot_general(a, b, (((1,), (1,)), ((), ())), preferred_element_type=f32)


def _dot_tn(a, b):
    return lax.dot_general(a, b, (((0,), (0,)), ((), ())), preferred_element_type=f32)


def _key_lanes(h):
    return HEAD * (h // 2) if h < H_A else KA_W + HEAD * (h - H_A)


def _own_keys(h):
    return slice(DK_A * (h % 2), DK_A * (h % 2 + 1)) if h < H_A else slice(0, HEAD)


def _cumsum_rows(x, chunk):
    rows, width = x.shape
    row = lax.broadcasted_iota(jnp.int32, (SUBLANES, width), 0)
    groups_per_chunk = chunk // SUBLANES
    out = []
    carry = None
    for g in range(rows // SUBLANES):
        y = x[g * SUBLANES:(g + 1) * SUBLANES]
        for s in (1, 2, 4):
            y = y + jnp.where(row >= s, pltpu.roll(y, s, 0), 0.0)
        if g % groups_per_chunk != 0:
            y = y + carry
        carry = y[SUBLANES - 1:SUBLANES]
        out.append(y)
    return out[0] if len(out) == 1 else jnp.concatenate(out, axis=0)


def _lower_bound(lbp_ref):
    lbp = lbp_ref[...]
    m = jnp.max(lbp, axis=0, keepdims=True)
    e = jnp.exp(lbp - m)
    return e[0:1] / jnp.sum(e, axis=0, keepdims=True)


def _decays(la, chunk):
    rows = la.shape[0]
    b = _cumsum_rows(la, chunk)
    n_chunks = rows // chunk
    lasts = [b[(c + 1) * chunk - 1:(c + 1) * chunk] for c in range(n_chunks)]
    if n_chunks == 1:
        b_last = lasts[0]
        b_last_rows = b_last
    else:
        b_last = jnp.concatenate(lasts, axis=0)
        b_last_rows = jnp.concatenate(
            [jnp.broadcast_to(l, (chunk, l.shape[1])) for l in lasts], axis=0)
    return jnp.exp(b), jnp.exp(-b), jnp.exp(b_last_rows - b), jnp.exp(b_last)


def _emit_keys_paired(refs, r, q_targets, kl, q, k, la):
    q32_ref, q64_ref, kin_ref, kx_ref, ko_ref, d_ref = refs
    c = PROMPT_CHUNK
    rows = slice(2 * c * r, 2 * c * (r + 1))
    e_b, e_nb, e_out, decay = _decays(la, c)
    d0, d1 = decay[0:1], decay[1:2]
    q32 = q * e_b
    q64 = jnp.concatenate([q32[:c], q32[c:] * d0], axis=0)
    k_in = k * e_nb
    k_out = k * e_out
    for ql, own in q_targets:
        q32_ref[rows, ql] = (q32 if own is None else jnp.where(own, q32, 0.0)).astype(bf16)
        q64_ref[rows, ql] = (q64 if own is None else jnp.where(own, q64, 0.0)).astype(bf16)
    kin_ref[c * r:c * (r + 1), kl] = k_in[:c].astype(bf16)
    kx_ref[rows, kl] = jnp.concatenate([k_out[:c], k_in[c:]], axis=0).astype(bf16)
    ko_ref[rows, kl] = jnp.concatenate([k_out[:c] * d1, k_out[c:]], axis=0).astype(bf16)
    d_ref[r:r + 1, kl] = d0 * d1


def _emit_keys_single(refs, r, q_targets, kl, q, k, la, *, chunk):
    q_ref, kin_ref, ko_ref, d_ref = refs
    rows = slice(SUBLANES * chunk * r, SUBLANES * chunk * (r + 1))
    e_b, e_nb, e_out, decay = _decays(la, chunk)
    q_in = q * e_b
    for ql, own in q_targets:
        q_ref[rows, ql] = q_in if own is None else jnp.where(own, q_in, 0.0)
    kin_ref[rows, kl] = k * e_nb
    ko_ref[rows, kl] = k * e_out
    d_ref[SUBLANES * r:SUBLANES * (r + 1), kl] = decay


def _proj_kernel(x_ref, gm_ref, w_ref, w_a2_ref, b_a_ref, lbp_ref, *refs, chunk, paired):
    n_key_refs = 6 if paired else 4
    key_refs = refs[:n_key_refs]
    v_ref, g_ref = refs[n_key_refs:n_key_refs + 2]
    p_refs = refs[n_key_refs + 2:]
    tm = x_ref.shape[0]
    block = 2 * chunk if paired else SUBLANES * chunk
    emit_keys = (_emit_keys_paired if paired
                 else functools.partial(_emit_keys_single, chunk=chunk))
    blocks = [(slice(r * block, (r + 1) * block), r) for r in range(tm // block)]
    h = _rmsnorm(x_ref[...], gm_ref[...]).astype(bf16)

    w_a2 = w_a2_ref[...]
    b_a = b_a_ref[...]
    lb = _lower_bound(lbp_ref)
    lane = lax.broadcasted_iota(jnp.int32, (block, HEAD), 1)

    def p_slot(lo, hi):
        for (g_lo, g_hi), ref in zip(PROJ_GROUPS, p_refs):
            if g_lo <= lo and hi <= g_hi:
                return ref, slice(lo - g_lo, hi - g_lo)
        raise ValueError((lo, hi))

    def project(lo, hi):
        ref, cols = p_slot(lo, hi)
        ref[:, cols] = jnp.dot(h, w_ref[:, lo:hi], preferred_element_type=f32)

    def p(rows, lo, width):
        ref, cols = p_slot(lo, lo + width)
        return ref[rows, cols]

    def gla_keys(rows, r):
        lra = p(rows, OFF_LR, HEAD)
        z = jnp.dot(lra.astype(bf16), w_a2, preferred_element_type=f32) + b_a
        for pr in range(H_A // 2):
            kl = slice(pr * HEAD, (pr + 1) * HEAD)
            zz = z[:, kl]
            la = (jnp.minimum(zz, 0.0) - jnp.log(1.0 + jnp.exp(-jnp.abs(zz)))) * (1.0 / GATE_NORM)
            q = p(rows, OFF_QA + pr * HEAD, HEAD) * (DK_A ** -0.5)
            k = p(rows, OFF_KA + pr * HEAD, HEAD)
            q_targets = [(slice(hd * HEAD, (hd + 1) * HEAD), own)
                         for hd, own in ((2 * pr, lane < DK_A), (2 * pr + 1, lane >= DK_A))]
            emit_keys(key_refs, r, q_targets, kl, q, k, la)

    def hgrn_keys(rows, r):
        for hd in range(N_HEADS - H_A):
            lbh = lb[:, hd * HEAD:(hd + 1) * HEAD]
            f = lbh + (1.0 - lbh) * _sigmoid(p(rows, OFF_FB + hd * HEAD, HEAD))
            q = _silu(p(rows, OFF_QB + hd * HEAD, HEAD))
            ql = slice(SEC + hd * HEAD, SEC + (hd + 1) * HEAD)
            kl = slice(KA_W + hd * HEAD, KA_W + (hd + 1) * HEAD)
            emit_keys(key_refs, r, [(ql, None)], kl, q, 1.0 - f, jnp.log(f))

    def gla_values(rows, crow):
        v_ref[rows, 0:SEC] = p(rows, OFF_VA, SEC).astype(v_ref.dtype)
        g_ref[rows, 0:SEC] = p(rows, OFF_GA, SEC)

    def hgrn_values(rows, crow):
        v_ref[rows, SEC:2 * SEC] = p(rows, OFF_IB, SEC).astype(v_ref.dtype)
        g_ref[rows, SEC:2 * SEC] = p(rows, OFF_GB, SEC)

    def tasks(fn, which):
        return [functools.partial(fn, *blocks[r]) for r in which]

    nb = len(blocks)
    first, second = range(nb // 2), range(nb // 2, nb)
    windows = (
        [],
        tasks(gla_keys, range(nb)),
        tasks(hgrn_keys, first),
        tasks(hgrn_keys, second) + tasks(gla_values, range(nb)),
    )
    for (lo, hi), work in zip(PROJ_GROUPS, windows):
        starts = list(range(lo, hi, MXU_N))
        for i, c0 in enumerate(starts):
            project(c0, min(c0 + MXU_N, hi))
            for t in work[i * len(work) // len(starts):(i + 1) * len(work) // len(starts)]:
                t()
    for t in tasks(hgrn_values, range(nb)):
        t()


def _proj(x2d, g_mix, w_in_p, w_a2p, b_ap, lb_param, *, tm, chunk, paired):
    n = x2d.shape[0]
    kern = functools.partial(_proj_kernel, chunk=chunk, paired=paired)

    def arr(rows, width, dtype):
        return (jax.ShapeDtypeStruct((n // rows, width), dtype),
                pl.BlockSpec((tm // rows, width), lambda i: (i, 0)))

    if paired:
        keys = [arr(1, D_MODEL, bf16), arr(1, D_MODEL, bf16), arr(2, K_W, bf16),
                arr(1, K_W, bf16), arr(1, K_W, bf16), arr(2 * chunk, K_W, f32)]
        v_out = arr(1, D_MODEL, bf16)
    else:
        keys = [arr(1, D_MODEL, f32), arr(1, K_W, f32), arr(1, K_W, f32), arr(chunk, K_W, f32)]
        v_out = arr(1, D_MODEL, f32)
    outs = keys + [v_out, arr(1, D_MODEL, f32)]

    return pl.pallas_call(
        kern,
        out_shape=tuple(o[0] for o in outs),
        grid=(n // tm,),
        in_specs=[
            pl.BlockSpec((tm, D_MODEL), lambda i: (i, 0)),
            _const_spec((1, D_MODEL)),
            _const_spec((D_MODEL, D_P)),
            _const_spec(w_a2p.shape),
            _const_spec(b_ap.shape),
            _const_spec(lb_param.shape),
        ],
        out_specs=tuple(o[1] for o in outs),
        scratch_shapes=[pltpu.VMEM((tm, hi - lo), f32) for lo, hi in PROJ_GROUPS],
        compiler_params=pltpu.CompilerParams(
            dimension_semantics=("arbitrary",), vmem_limit_bytes=VMEM_LIMIT),
        name="proj",
    )(x2d, g_mix, w_in_p, w_a2p, b_ap, lb_param)


def _head_norm(o, onorm):
    return o * lax.rsqrt(jnp.mean(o * o, axis=-1, keepdims=True) + EPS) * onorm


def _tril(n):
    r = lax.broadcasted_iota(jnp.int32, (n, n), 0)
    c = lax.broadcasted_iota(jnp.int32, (n, n), 1)
    return r >= c


def _mixer_prompt_kernel(q32_ref, q64_ref, kin_ref, kx_ref, ko_ref, d_ref, v_ref, g_ref,
                         ona_ref, onb_ref, o_ref, sa_ref, sb_ref,
                         st_ref, att0_ref, att1_ref, up_ref, start_ref, *, tile, chunk):
    j = pl.program_id(1)

    @pl.when(j == 0)
    def _():
        st_ref[...] = jnp.zeros_like(st_ref)

    onorm = [ona_ref[...]] * H_A + [onb_ref[...]] * (N_HEADS - H_A)
    pair = 2 * chunk
    n_pairs = tile // pair
    tril = _tril(chunk)
    r = lax.broadcasted_iota(jnp.int32, (chunk, pair), 0)
    c = lax.broadcasted_iota(jnp.int32, (chunk, pair), 1)
    cross = (c < chunk) | (r >= c - chunk)

    for h in range(N_HEADS):
        sl = slice(h * HEAD, (h + 1) * HEAD)
        kl = slice(_key_lanes(h), _key_lanes(h) + HEAD)
        for s in range(n_pairs):
            u = h * n_pairs + s
            ra = slice(s * pair, s * pair + chunk)
            rb = slice(s * pair + chunk, (s + 1) * pair)
            rp = slice(s * pair, (s + 1) * pair)
            a_keys = kin_ref[s * chunk:(s + 1) * chunk, kl]
            att0_ref[u] = jnp.where(tril, _dot_nt(q32_ref[ra, sl], a_keys), 0.0).astype(bf16)
            att1_ref[u] = jnp.where(cross, _dot_nt(q32_ref[rb, sl], kx_ref[rp, kl]),
                                    0.0).astype(bf16)
            up_ref[u] = _dot_tn(v_ref[rp, sl], ko_ref[rp, kl])

    for h in range(N_HEADS):
        kl = slice(_key_lanes(h), _key_lanes(h) + HEAD)
        st = st_ref[h]
        for s in range(n_pairs):
            u = h * n_pairs + s
            start_ref[u] = st.astype(bf16)
            st = d_ref[s:s + 1, kl] * st + up_ref[u]
        st_ref[h] = st

    for h in range(N_HEADS):
        sl = slice(h * HEAD, (h + 1) * HEAD)
        outs = []
        for s in range(n_pairs):
            u = h * n_pairs + s
            ra = slice(s * pair, s * pair + chunk)
            rp = slice(s * pair, (s + 1) * pair)
            inter = _dot_nt(q64_ref[rp, sl], start_ref[u])
            outs.append(inter[:chunk]
                        + jnp.dot(att0_ref[u], v_ref[ra, sl], preferred_element_type=f32))
            outs.append(inter[chunk:]
                        + jnp.dot(att1_ref[u], v_ref[rp, sl], preferred_element_type=f32))
        o = jnp.concatenate(outs, axis=0)
        o_ref[:, sl] = (_head_norm(o, onorm[h]) * _silu(g_ref[:, sl])).astype(o_ref.dtype)

    @pl.when(j == pl.num_programs(1) - 1)
    def _():
        for h in range(H_A):
            sa_ref[0, h] = st_ref[h].T[_own_keys(h), :]
            sb_ref[0, h] = st_ref[H_A + h].T


def _mixer_prompt(acts, ona, onb, batch, seq, tile):
    nt = seq // tile
    chunk = PROMPT_CHUNK
    units = N_HEADS * (tile // (2 * chunk))
    kern = functools.partial(_mixer_prompt_kernel, tile=tile, chunk=chunk)

    def rows(width, per=1):
        return pl.BlockSpec((tile // per, width), lambda b, j: (b * nt + j, 0))

    return pl.pallas_call(
        kern,
        out_shape=(
            jax.ShapeDtypeStruct((batch * seq, D_MODEL), bf16),
            jax.ShapeDtypeStruct((batch, H_A, DK_A, HEAD), f32),
            jax.ShapeDtypeStruct((batch, H_A, HEAD, HEAD), f32),
        ),
        grid=(batch, nt),
        in_specs=[
            rows(D_MODEL), rows(D_MODEL), rows(K_W, 2), rows(K_W), rows(K_W),
            rows(K_W, 2 * chunk), rows(D_MODEL), rows(D_MODEL),
            _const_spec(ona.shape),
            _const_spec(onb.shape),
        ],
        out_specs=(
            rows(D_MODEL),
            pl.BlockSpec((1, H_A, DK_A, HEAD), lambda b, j: (b, 0, 0, 0)),
            pl.BlockSpec((1, H_A, HEAD, HEAD), lambda b, j: (b, 0, 0, 0)),
        ),
        scratch_shapes=[
            pltpu.VMEM((N_HEADS, HEAD, HEAD), f32),
            pltpu.VMEM((units, chunk, chunk), bf16),
            pltpu.VMEM((units, chunk, 2 * chunk), bf16),
            pltpu.VMEM((units, HEAD, HEAD), f32),
            pltpu.VMEM((units, HEAD, HEAD), bf16),
        ],
        compiler_params=pltpu.CompilerParams(
            dimension_semantics=("arbitrary", "arbitrary"), vmem_limit_bytes=VMEM_LIMIT),
        name="mixer_prompt",
    )(*acts, ona, onb)


def _mixer_sample_kernel(qi_ref, ki_ref, ko_ref, v_ref, g_ref, d_ref, sa_in_ref, sb_in_ref,
                         ona_ref, onb_ref, o_ref, sa_ref, sb_ref, att_ref, oi_ref, *, bt, seq):
    onorm = [ona_ref[...]] * H_A + [onb_ref[...]] * (N_HEADS - H_A)
    tril = _tril(seq)
    zpad = jnp.zeros((HEAD - DK_A, HEAD), f32)
    zeros = jnp.zeros((seq, HEAD), f32)
    row = lax.broadcasted_iota(jnp.int32, (seq, HEAD), 0)
    ones_rows = jnp.where(row < 2, 1.0, 0.0)
    rhs_bottom = jnp.concatenate([zeros, ones_rows], axis=1)

    for bi in range(bt):
        rows = slice(bi * seq, (bi + 1) * seq)
        decay = d_ref[bi:bi + 1, :]
        d_hi = decay.astype(bf16).astype(f32)
        d_lo = decay - d_hi
        for h in range(N_HEADS):
            sl = slice(h * HEAD, (h + 1) * HEAD)
            kl = slice(_key_lanes(h), _key_lanes(h) + HEAD)
            q_in = qi_ref[rows, sl].astype(bf16)
            k_in = ki_ref[rows, kl].astype(bf16)
            v32 = v_ref[rows, sl]
            if h >= H_A:
                s = sb_in_ref[bi, h - H_A]
            elif h % 2 == 0:
                s = jnp.concatenate([sa_in_ref[bi, h], zpad], axis=0)
            else:
                s = jnp.concatenate([zpad, sa_in_ref[bi, h]], axis=0)
            att_ref[bi * N_HEADS + h] = jnp.where(tril, _dot_nt(q_in, k_in), 0.0)
            oi_ref[rows, sl] = jnp.dot(q_in, s.astype(bf16), preferred_element_type=f32)
            d_rows = jnp.where(row == 0, d_hi[:, kl], jnp.where(row == 1, d_lo[:, kl], 0.0))
            lhs = jnp.concatenate([ko_ref[rows, kl], d_rows], axis=0).astype(bf16)
            rhs = jnp.concatenate(
                [jnp.concatenate([v32, zeros], axis=1), rhs_bottom], axis=0).astype(bf16)
            upd = _dot_tn(lhs, rhs)
            s_new = upd[:, HEAD:] * s + upd[:, :HEAD]
            if h < H_A:
                sa_ref[bi, h] = s_new[_own_keys(h), :]
            else:
                sb_ref[bi, h - H_A] = s_new

    for bi in range(bt):
        rows = slice(bi * seq, (bi + 1) * seq)
        for h in range(N_HEADS):
            sl = slice(h * HEAD, (h + 1) * HEAD)
            att = att_ref[bi * N_HEADS + h].astype(bf16)
            o = oi_ref[rows, sl] + jnp.dot(att, v_ref[rows, sl].astype(bf16),
                                           preferred_element_type=f32)
            o_ref[rows, sl] = (_head_norm(o, onorm[h]) * _silu(g_ref[rows, sl])).astype(
                o_ref.dtype)


def _mixer_sample(acts, sa_in, sb_in, ona, onb, batch, seq, bt):
    kern = functools.partial(_mixer_sample_kernel, bt=bt, seq=seq)

    def rows(width):
        return pl.BlockSpec((bt * seq, width), lambda i: (i, 0))

    sa_spec = pl.BlockSpec((bt, H_A, DK_A, HEAD), lambda i: (i, 0, 0, 0))
    sb_spec = pl.BlockSpec((bt, H_A, HEAD, HEAD), lambda i: (i, 0, 0, 0))
    return pl.pallas_call(
        kern,
        out_shape=(
            jax.ShapeDtypeStruct((batch * seq, D_MODEL), f32),
            jax.ShapeDtypeStruct((batch, H_A, DK_A, HEAD), f32),
            jax.ShapeDtypeStruct((batch, H_A, HEAD, HEAD), f32),
        ),
        grid=(batch // bt,),
        in_specs=[
            rows(D_MODEL), rows(K_W), rows(K_W), rows(D_MODEL), rows(D_MODEL),
            pl.BlockSpec((bt, K_W), lambda i: (i, 0)),
            sa_spec, sb_spec,
            _const_spec(ona.shape),
            _const_spec(onb.shape),
        ],
        out_specs=(rows(D_MODEL), sa_spec, sb_spec),
        scratch_shapes=[
            pltpu.VMEM((bt * N_HEADS, seq, seq), f32),
            pltpu.VMEM((bt * seq, D_MODEL), f32),
        ],
        compiler_params=pltpu.CompilerParams(
            dimension_semantics=("arbitrary",), vmem_limit_bytes=VMEM_LIMIT),
        name="mixer_sample",
    )(*acts, sa_in, sb_in, ona, onb)


def _out_kernel(o_ref, x_ref, w_o_ref, gf_ref, w1_ref, w3_ref, w2_ref, gl_ref, y_ref):
    x1 = x_ref[...] + jnp.dot(o_ref[...].astype(bf16), w_o_ref[...], preferred_element_type=f32)
    h = _rmsnorm(x1, gf_ref[...]).astype(bf16)
    acc = x1
    for c in range(D_FF // FF_CHUNK):
        cols = slice(c * FF_CHUNK, (c + 1) * FF_CHUNK)
        a = jnp.dot(h, w1_ref[:, cols], preferred_element_type=f32)
        b = jnp.dot(h, w3_ref[:, cols], preferred_element_type=f32)
        u = (_silu(a) * b).astype(bf16)
        acc = acc + jnp.dot(u, w2_ref[cols, :], preferred_element_type=f32)
    y_ref[...] = _rmsnorm(acc, gl_ref[...])


def _out(o2d, x2d, w_o, g_ffn, w1, w3, w2, g_final, tm):
    n = x2d.shape[0]
    row_spec = pl.BlockSpec((tm, D_MODEL), lambda i: (i, 0))
    return pl.pallas_call(
        _out_kernel,
        out_shape=jax.ShapeDtypeStruct((n, D_MODEL), f32),
        grid=(n // tm,),
        in_specs=[
            row_spec, row_spec,
            _const_spec((D_MODEL, D_MODEL)),
            _const_spec((1, D_MODEL)),
            _const_spec((D_MODEL, D_FF)),
            _const_spec((D_MODEL, D_FF)),
            _const_spec((D_FF, D_MODEL)),
            _const_spec((1, D_MODEL)),
        ],
        out_specs=row_spec,
        compiler_params=pltpu.CompilerParams(
            dimension_semantics=("arbitrary",), vmem_limit_bytes=VMEM_LIMIT),
        name="out_ffn",
    )(o2d, x2d, w_o, g_ffn, w1, w3, w2, g_final)


_W_IN_SECTIONS = []
_src = 0
for _width, _dst in ((KA_W, OFF_QA), (KA_W, OFF_KA), (SEC, OFF_VA), (SEC, OFF_GA),
                     (GATE_RANK, OFF_LR), (SEC, OFF_QB), (SEC, OFF_FB), (SEC, OFF_IB),
                     (SEC, OFF_GB)):
    _W_IN_SECTIONS.append((_src, _width, _dst))
    _src += _width
D_IN = _src


def _w_in_layout_kernel(w_ref, o_ref):
    for src, width, dst in _W_IN_SECTIONS:
        part = w_ref[:, src:src + width]
        if width % HEAD:
            part = jnp.concatenate(
                [part, jnp.zeros((part.shape[0], HEAD - width % HEAD), part.dtype)], axis=1)
        o_ref[:, dst:dst + part.shape[1]] = part.astype(o_ref.dtype)


def _layout_w_in(w_in, rows=256):
    return pl.pallas_call(
        _w_in_layout_kernel,
        out_shape=jax.ShapeDtypeStruct((D_MODEL, D_P), bf16),
        grid=(D_MODEL // rows,),
        in_specs=[pl.BlockSpec((rows, D_IN), lambda i: (i, 0))],
        out_specs=pl.BlockSpec((rows, D_P), lambda i: (i, 0)),
        compiler_params=pltpu.CompilerParams(
            dimension_semantics=("arbitrary",), vmem_limit_bytes=VMEM_LIMIT),
        name="w_in_layout",
    )(w_in)


def kernel(x_prompt, x_sample, state_gla, state_hgrn, norm_mix, w_in, w_a2, b_a, lb_param,
           onorm_a, onorm_b, w_o, norm_ffn, w1, w3, w2, norm_final):
    batch, seq, _ = x_prompt.shape
    dec_batch, dec_seq, _ = x_sample.shape

    w_in_p = _layout_w_in(w_in[0])
    w_a2p = jnp.pad(w_a2[0], ((0, HEAD - GATE_RANK), (0, 0))).astype(bf16)
    b_ap = b_a[0].reshape(1, KA_W)
    g_mix = norm_mix[0].reshape(1, D_MODEL)
    g_ffn = norm_ffn[0].reshape(1, D_MODEL)
    g_fin = norm_final.reshape(1, D_MODEL)
    ona = onorm_a[0].reshape(1, HEAD)
    onb = onorm_b[0].reshape(1, HEAD)
    w_o_b = w_o[0].astype(bf16)
    w1_b = w1[0].astype(bf16)
    w3_b = w3[0].astype(bf16)
    w2_b = w2[0].astype(bf16)

    xp = x_prompt.reshape(batch * seq, D_MODEL)
    xs = x_sample.reshape(dec_batch * dec_seq, D_MODEL)

    acts_p = _proj(xp, g_mix, w_in_p, w_a2p, b_ap, lb_param,
                   tm=512, chunk=PROMPT_CHUNK, paired=True)
    op, sa_p, sb_p = _mixer_prompt(acts_p, ona, onb, batch, seq, tile=512)
    yp = _out(op, xp, w_o_b, g_ffn, w1_b, w3_b, w2_b, g_fin, tm=512)

    q_s, kin_s, ko_s, d_s, v_s, g_s = _proj(xs, g_mix, w_in_p, w_a2p, b_ap, lb_param,
                                            tm=512, chunk=dec_seq, paired=False)
    os_, sa_s, sb_s = _mixer_sample((q_s, kin_s, ko_s, v_s, g_s, d_s), state_gla[0],
                                    state_hgrn[0], ona, onb, dec_batch, dec_seq, bt=16)
    ys = _out(os_, xs, w_o_b, g_ffn, w1_b, w3_b, w2_b, g_fin, tm=512)

    return (yp.reshape(batch, seq, D_MODEL), ys.reshape(dec_batch, dec_seq, D_MODEL),
            sa_p[None], sb_p[None], sa_s[None], sb_s[None])
```

```python
import functools

import jax
import jax.numpy as jnp
from jax import lax
from jax.experimental import pallas as pl
from jax.experimental.pallas import tpu as pltpu

D_MODEL = 1024
N_HEADS = 8
H_A = 4
DK_A = 64
HEAD = 128
GATE_RANK = 16
GATE_NORM = 16.0
D_FF = 2816
FF_CHUNK = 256
EPS = 1e-6
PROMPT_CHUNK = 32
SUBLANES = 8
BF16_ROWS = 16

KA_W = H_A * DK_A
SEC = H_A * HEAD
K_W = KA_W + SEC
OFF_LR = 0
OFF_QA = HEAD
OFF_KA = OFF_QA + KA_W
OFF_VA = OFF_KA + KA_W
OFF_GA = OFF_VA + SEC
OFF_FB = OFF_GA + SEC
OFF_QB = OFF_FB + SEC
OFF_IB = OFF_QB + SEC
OFF_GB = OFF_IB + SEC
D_P = OFF_GB + SEC
MXU_N = 256
PROJ_GROUPS = ((OFF_LR, OFF_VA), (OFF_FB, OFF_IB), (OFF_VA, OFF_FB), (OFF_IB, D_P))

VMEM_LIMIT = 56 * 1024 * 1024

f32 = jnp.float32
bf16 = jnp.bfloat16


def _const_spec(shape):
    nd = len(shape)
    return pl.BlockSpec(shape, lambda *_: (0,) * nd, pipeline_mode=pl.Buffered(1))


def _rmsnorm(x, g):
    return x * lax.rsqrt(jnp.mean(x * x, axis=-1, keepdims=True) + EPS) * g


def _sigmoid(x):
    return 0.5 * jnp.tanh(0.5 * x) + 0.5


def _silu(x):
    hx = 0.5 * x
    return hx + hx * jnp.tanh(hx)


def _dot_nt(a, b):
    return lax.dot_general(a, b, (((1,), (1,)), ((), ())), preferred_element_type=f32)


def _dot_tn(a, b):
    return lax.dot_general(a, b, (((0,), (0,)), ((), ())), preferred_element_type=f32)


def _key_lanes(h):
    return HEAD * (h // 2) if h < H_A else KA_W + HEAD * (h - H_A)


def _own_keys(h):
    return slice(DK_A * (h % 2), DK_A * (h % 2 + 1)) if h < H_A else slice(0, HEAD)


def _cumsum_rows(x, chunk):
    rows, width = x.shape
    row = lax.broadcasted_iota(jnp.int32, (SUBLANES, width), 0)
    groups_per_chunk = chunk // SUBLANES
    out = []
    carry = None
    for g in range(rows // SUBLANES):
        y = x[g * SUBLANES:(g + 1) * SUBLANES]
        for s in (1, 2, 4):
            y = y + jnp.where(row >= s, pltpu.roll(y, s, 0), 0.0)
        if g % groups_per_chunk != 0:
            y = y + carry
        carry = y[SUBLANES - 1:SUBLANES]
        out.append(y)
    return out[0] if len(out) == 1 else jnp.concatenate(out, axis=0)


def _lower_bound(lbp_ref):
    lbp = lbp_ref[...]
    m = jnp.max(lbp, axis=0, keepdims=True)
    e = jnp.exp(lbp - m)
    return e[0:1] / jnp.sum(e, axis=0, keepdims=True)


def _decays(la, chunk):
    rows = la.shape[0]
    b = _cumsum_rows(la, chunk)
    n_chunks = rows // chunk
    lasts = [b[(c + 1) * chunk - 1:(c + 1) * chunk] for c in range(n_chunks)]
    if n_chunks == 1:
        b_last = lasts[0]
        b_last_rows = b_last
    else:
        b_last = jnp.concatenate(lasts, axis=0)
        b_last_rows = jnp.concatenate(
            [jnp.broadcast_to(l, (chunk, l.shape[1])) for l in lasts], axis=0)
    return jnp.exp(b), jnp.exp(-b), jnp.exp(b_last_rows - b), jnp.exp(b_last)


def _emit_keys_paired(refs, r, q_targets, kl, q, k, la):
    q32_ref, q64_ref, kin_ref, kx_ref, ko_ref, d_ref = refs
    c = PROMPT_CHUNK
    rows = slice(2 * c * r, 2 * c * (r + 1))
    e_b, e_nb, e_out, decay = _decays(la, c)
    d0, d1 = decay[0:1], decay[1:2]
    q32 = q * e_b
    q64 = jnp.concatenate([q32[:c], q32[c:] * d0], axis=0)
    k_in = k * e_nb
    k_out = k * e_out
    for ql, own in q_targets:
        q32_ref[rows, ql] = (q32 if own is None else jnp.where(own, q32, 0.0)).astype(bf16)
        q64_ref[rows, ql] = (q64 if own is None else jnp.where(own, q64, 0.0)).astype(bf16)
    kin_ref[c * r:c * (r + 1), kl] = k_in[:c].astype(bf16)
    kx_ref[rows, kl] = jnp.concatenate([k_out[:c], k_in[c:]], axis=0).astype(bf16)
    ko_ref[rows, kl] = jnp.concatenate([k_out[:c] * d1, k_out[c:]], axis=0).astype(bf16)
    d_ref[r:r + 1, kl] = d0 * d1


def _emit_keys_single(refs, r, q_targets, kl, q, k, la, *, chunk):
    q_ref, kin_ref, ko_ref, d_ref = refs
    rows = slice(SUBLANES * chunk * r, SUBLANES * chunk * (r + 1))
    e_b, e_nb, e_out, decay = _decays(la, chunk)
    q_in = q * e_b
    for ql, own in q_targets:
        q_ref[rows, ql] = q_in if own is None else jnp.where(own, q_in, 0.0)
    kin_ref[rows, kl] = k * e_nb
    ko_ref[rows, kl] = k * e_out
    d_ref[SUBLANES * r:SUBLANES * (r + 1), kl] = decay


def _proj_kernel(x_ref, gm_ref, w_ref, w_a2_ref, b_a_ref, lbp_ref, *refs, chunk, paired):
    n_key_refs = 6 if paired else 4
    key_refs = refs[:n_key_refs]
    v_ref, g_ref = refs[n_key_refs:n_key_refs + 2]
    p_refs = refs[n_key_refs + 2:]
    tm = x_ref.shape[0]
    block = 2 * chunk if paired else SUBLANES * chunk
    emit_keys = (_emit_keys_paired if paired
                 else functools.partial(_emit_keys_single, chunk=chunk))
    blocks = [(slice(r * block, (r + 1) * block), r) for r in range(tm // block)]
    h = _rmsnorm(x_ref[...], gm_ref[...]).astype(bf16)

    w_a2 = w_a2_ref[...]
    b_a = b_a_ref[...]
    lb = _lower_bound(lbp_ref)
    lane = lax.broadcasted_iota(jnp.int32, (block, HEAD), 1)

    def p_slot(lo, hi):
        for (g_lo, g_hi), ref in zip(PROJ_GROUPS, p_refs):
            if g_lo <= lo and hi <= g_hi:
                return ref, slice(lo - g_lo, hi - g_lo)
        raise ValueError((lo, hi))

    def project(lo, hi):
        ref, cols = p_slot(lo, hi)
        ref[:, cols] = _dot_nt(h, w_ref[lo:hi, :])

    def p(rows, lo, width):
        ref, cols = p_slot(lo, lo + width)
        return ref[rows, cols]

    def gla_keys(rows, r):
        lra = p(rows, OFF_LR, HEAD)
        z = jnp.dot(lra.astype(bf16), w_a2, preferred_element_type=f32) + b_a
        for pr in range(H_A // 2):
            kl = slice(pr * HEAD, (pr + 1) * HEAD)
            zz = z[:, kl]
            la = (jnp.minimum(zz, 0.0) - jnp.log(1.0 + jnp.exp(-jnp.abs(zz)))) * (1.0 / GATE_NORM)
            q = p(rows, OFF_QA + pr * HEAD, HEAD) * (DK_A ** -0.5)
            k = p(rows, OFF_KA + pr * HEAD, HEAD)
            q_targets = [(slice(hd * HEAD, (hd + 1) * HEAD), own)
                         for hd, own in ((2 * pr, lane < DK_A), (2 * pr + 1, lane >= DK_A))]
            emit_keys(key_refs, r, q_targets, kl, q, k, la)

    def hgrn_keys(rows, r):
        for hd in range(N_HEADS - H_A):
            lbh = lb[:, hd * HEAD:(hd + 1) * HEAD]
            f = lbh + (1.0 - lbh) * _sigmoid(p(rows, OFF_FB + hd * HEAD, HEAD))
            q = _silu(p(rows, OFF_QB + hd * HEAD, HEAD))
            ql = slice(SEC + hd * HEAD, SEC + (hd + 1) * HEAD)
            kl = slice(KA_W + hd * HEAD, KA_W + (hd + 1) * HEAD)
            emit_keys(key_refs, r, [(ql, None)], kl, q, 1.0 - f, jnp.log(f))

    def gla_values(rows, crow):
        v_ref[rows, 0:SEC] = p(rows, OFF_VA, SEC).astype(v_ref.dtype)
        g_ref[rows, 0:SEC] = p(rows, OFF_GA, SEC)

    def hgrn_values(rows, crow):
        v_ref[rows, SEC:2 * SEC] = p(rows, OFF_IB, SEC).astype(v_ref.dtype)
        g_ref[rows, SEC:2 * SEC] = p(rows, OFF_GB, SEC)

    def tasks(fn, which):
        return [functools.partial(fn, *blocks[r]) for r in which]

    nb = len(blocks)
    first, second = range(nb // 2), range(nb // 2, nb)
    windows = (
        [],
        tasks(gla_keys, range(nb)),
        tasks(hgrn_keys, first),
        tasks(hgrn_keys, second) + tasks(gla_values, range(nb)),
    )
    for (lo, hi), work in zip(PROJ_GROUPS, windows):
        starts = list(range(lo, hi, MXU_N))
        for i, c0 in enumerate(starts):
            project(c0, min(c0 + MXU_N, hi))
            for t in work[i * len(work) // len(starts):(i + 1) * len(work) // len(starts)]:
                t()
    for t in tasks(hgrn_values, range(nb)):
        t()


def _proj(x2d, g_mix, w_in_p, w_a2p, b_ap, lb_param, *, tm, chunk, paired):
    n = x2d.shape[0]
    kern = functools.partial(_proj_kernel, chunk=chunk, paired=paired)

    def arr(rows, width, dtype):
        return (jax.ShapeDtypeStruct((n // rows, width), dtype),
                pl.BlockSpec((tm // rows, width), lambda i: (i, 0)))

    if paired:
        keys = [arr(1, D_MODEL, bf16), arr(1, D_MODEL, bf16), arr(2, K_W, bf16),
                arr(1, K_W, bf16), arr(1, K_W, bf16), arr(2 * chunk, K_W, f32)]
        v_out = arr(1, D_MODEL, bf16)
    else:
        keys = [arr(1, D_MODEL, f32), arr(1, K_W, f32), arr(1, K_W, f32), arr(chunk, K_W, f32)]
        v_out = arr(1, D_MODEL, f32)
    outs = keys + [v_out, arr(1, D_MODEL, f32)]

    return pl.pallas_call(
        kern,
        out_shape=tuple(o[0] for o in outs),
        grid=(n // tm,),
        in_specs=[
            pl.BlockSpec((tm, D_MODEL), lambda i: (i, 0)),
            _const_spec((1, D_MODEL)),
            _const_spec((D_P, D_MODEL)),
            _const_spec(w_a2p.shape),
            _const_spec(b_ap.shape),
            _const_spec(lb_param.shape),
        ],
        out_specs=tuple(o[1] for o in outs),
        scratch_shapes=[pltpu.VMEM((tm, hi - lo), f32) for lo, hi in PROJ_GROUPS],
        compiler_params=pltpu.CompilerParams(
            dimension_semantics=("arbitrary",), vmem_limit_bytes=VMEM_LIMIT),
        name="proj",
    )(x2d, g_mix, w_in_p, w_a2p, b_ap, lb_param)


def _head_norm(o, onorm):
    return o * lax.rsqrt(jnp.mean(o * o, axis=-1, keepdims=True) + EPS) * onorm


def _tril(n):
    r = lax.broadcasted_iota(jnp.int32, (n, n), 0)
    c = lax.broadcasted_iota(jnp.int32, (n, n), 1)
    return r >= c


def _mixer_prompt_kernel(q32_ref, q64_ref, kin_ref, kx_ref, ko_ref, d_ref, v_ref, g_ref,
                         ona_ref, onb_ref, o_ref, sa_ref, sb_ref,
                         st_ref, att0_ref, att1_ref, up_ref, start_ref, *, tile, chunk):
    j = pl.program_id(1)

    @pl.when(j == 0)
    def _():
        st_ref[...] = jnp.zeros_like(st_ref)

    onorm = [ona_ref[...]] * H_A + [onb_ref[...]] * (N_HEADS - H_A)
    pair = 2 * chunk
    n_pairs = tile // pair
    tril = _tril(chunk)
    r = lax.broadcasted_iota(jnp.int32, (chunk, pair), 0)
    c = lax.broadcasted_iota(jnp.int32, (chunk, pair), 1)
    cross = (c < chunk) | (r >= c - chunk)

    for h in range(N_HEADS):
        sl = slice(h * HEAD, (h + 1) * HEAD)
        kl = slice(_key_lanes(h), _key_lanes(h) + HEAD)
        for s in range(n_pairs):
            u = h * n_pairs + s
            ra = slice(s * pair, s * pair + chunk)
            rb = slice(s * pair + chunk, (s + 1) * pair)
            rp = slice(s * pair, (s + 1) * pair)
            a_keys = kin_ref[s * chunk:(s + 1) * chunk, kl]
            att0_ref[u] = jnp.where(tril, _dot_nt(q32_ref[ra, sl], a_keys), 0.0).astype(bf16)
            att1_ref[u] = jnp.where(cross, _dot_nt(q32_ref[rb, sl], kx_ref[rp, kl]),
                                    0.0).astype(bf16)
            up_ref[u] = _dot_tn(v_ref[rp, sl], ko_ref[rp, kl])

    for h in range(N_HEADS):
        kl = slice(_key_lanes(h), _key_lanes(h) + HEAD)
        st = st_ref[h]
        for s in range(n_pairs):
            u = h * n_pairs + s
            start_ref[u] = st.astype(bf16)
            st = d_ref[s:s + 1, kl] * st + up_ref[u]
        st_ref[h] = st

    for h in range(N_HEADS):
        sl = slice(h * HEAD, (h + 1) * HEAD)
        outs = []
        for s in range(n_pairs):
            u = h * n_pairs + s
            ra = slice(s * pair, s * pair + chunk)
            rp = slice(s * pair, (s + 1) * pair)
            inter = _dot_nt(q64_ref[rp, sl], start_ref[u])
            outs.append(inter[:chunk]
                        + jnp.dot(att0_ref[u], v_ref[ra, sl], preferred_element_type=f32))
            outs.append(inter[chunk:]
                        + jnp.dot(att1_ref[u], v_ref[rp, sl], preferred_element_type=f32))
        o = jnp.concatenate(outs, axis=0)
        o_ref[:, sl] = (_head_norm(o, onorm[h]) * _silu(g_ref[:, sl])).astype(o_ref.dtype)

    @pl.when(j == pl.num_programs(1) - 1)
    def _():
        for h in range(H_A):
            sa_ref[0, h] = st_ref[h].T[_own_keys(h), :]
            sb_ref[0, h] = st_ref[H_A + h].T


def _mixer_prompt(acts, ona, onb, batch, seq, tile):
    nt = seq // tile
    chunk = PROMPT_CHUNK
    units = N_HEADS * (tile // (2 * chunk))
    kern = functools.partial(_mixer_prompt_kernel, tile=tile, chunk=chunk)

    def rows(width, per=1):
        return pl.BlockSpec((tile // per, width), lambda b, j: (b * nt + j, 0))

    return pl.pallas_call(
        kern,
        out_shape=(
            jax.ShapeDtypeStruct((batch * seq, D_MODEL), bf16),
            jax.ShapeDtypeStruct((batch, H_A, DK_A, HEAD), f32),
            jax.ShapeDtypeStruct((batch, H_A, HEAD, HEAD), f32),
        ),
        grid=(batch, nt),
        in_specs=[
            rows(D_MODEL), rows(D_MODEL), rows(K_W, 2), rows(K_W), rows(K_W),
            rows(K_W, 2 * chunk), rows(D_MODEL), rows(D_MODEL),
            _const_spec(ona.shape),
            _const_spec(onb.shape),
        ],
        out_specs=(
            rows(D_MODEL),
            pl.BlockSpec((1, H_A, DK_A, HEAD), lambda b, j: (b, 0, 0, 0)),
            pl.BlockSpec((1, H_A, HEAD, HEAD), lambda b, j: (b, 0, 0, 0)),
        ),
        scratch_shapes=[
            pltpu.VMEM((N_HEADS, HEAD, HEAD), f32),
            pltpu.VMEM((units, chunk, chunk), bf16),
            pltpu.VMEM((units, chunk, 2 * chunk), bf16),
            pltpu.VMEM((units, HEAD, HEAD), f32),
            pltpu.VMEM((units, HEAD, HEAD), bf16),
        ],
        compiler_params=pltpu.CompilerParams(
            dimension_semantics=("arbitrary", "arbitrary"), vmem_limit_bytes=VMEM_LIMIT),
        name="mixer_prompt",
    )(*acts, ona, onb)


def _mixer_sample_kernel(qi_ref, ki_ref, ko_ref, v_ref, g_ref, d_ref, sa_in_ref, sb_in_ref,
                         ona_ref, onb_ref, o_ref, sa_ref, sb_ref, att_ref, oi_ref, *, bt, seq):
    onorm = [ona_ref[...]] * H_A + [onb_ref[...]] * (N_HEADS - H_A)
    tril = _tril(seq)
    zpad = jnp.zeros((HEAD - DK_A, HEAD), f32)
    zeros = jnp.zeros((seq, HEAD), f32)
    row = lax.broadcasted_iota(jnp.int32, (seq, HEAD), 0)
    ones_rows = jnp.where(row < 2, 1.0, 0.0)
    rhs_bottom = jnp.concatenate([zeros, ones_rows], axis=1)

    for bi in range(bt):
        rows = slice(bi * seq, (bi + 1) * seq)
        decay = d_ref[bi:bi + 1, :]
        d_hi = decay.astype(bf16).astype(f32)
        d_lo = decay - d_hi
        for h in range(N_HEADS):
            sl = slice(h * HEAD, (h + 1) * HEAD)
            kl = slice(_key_lanes(h), _key_lanes(h) + HEAD)
            q_in = qi_ref[rows, sl].astype(bf16)
            k_in = ki_ref[rows, kl].astype(bf16)
            v32 = v_ref[rows, sl]
            if h >= H_A:
                s = sb_in_ref[bi, h - H_A]
            elif h % 2 == 0:
                s = jnp.concatenate([sa_in_ref[bi, h], zpad], axis=0)
            else:
                s = jnp.concatenate([zpad, sa_in_ref[bi, h]], axis=0)
            att_ref[bi * N_HEADS + h] = jnp.where(tril, _dot_nt(q_in, k_in), 0.0)
            oi_ref[rows, sl] = jnp.dot(q_in, s.astype(bf16), preferred_element_type=f32)
            d_rows = jnp.where(row == 0, d_hi[:, kl], jnp.where(row == 1, d_lo[:, kl], 0.0))
            lhs = jnp.concatenate([ko_ref[rows, kl], d_rows], axis=0).astype(bf16)
            rhs = jnp.concatenate(
                [jnp.concatenate([v32, zeros], axis=1), rhs_bottom], axis=0).astype(bf16)
            upd = _dot_tn(lhs, rhs)
            s_new = upd[:, HEAD:] * s + upd[:, :HEAD]
            if h < H_A:
                sa_ref[bi, h] = s_new[_own_keys(h), :]
            else:
                sb_ref[bi, h - H_A] = s_new

    for bi in range(bt):
        rows = slice(bi * seq, (bi + 1) * seq)
        for h in range(N_HEADS):
            sl = slice(h * HEAD, (h + 1) * HEAD)
            att = att_ref[bi * N_HEADS + h].astype(bf16)
            o = oi_ref[rows, sl] + jnp.dot(att, v_ref[rows, sl].astype(bf16),
                                           preferred_element_type=f32)
            o_ref[rows, sl] = (_head_norm(o, onorm[h]) * _silu(g_ref[rows, sl])).astype(
                o_ref.dtype)


def _mixer_sample(acts, sa_in, sb_in, ona, onb, batch, seq, bt):
    kern = functools.partial(_mixer_sample_kernel, bt=bt, seq=seq)

    def rows(width):
        return pl.BlockSpec((bt * seq, width), lambda i: (i, 0))

    sa_spec = pl.BlockSpec((bt, H_A, DK_A, HEAD), lambda i: (i, 0, 0, 0))
    sb_spec = pl.BlockSpec((bt, H_A, HEAD, HEAD), lambda i: (i, 0, 0, 0))
    return pl.pallas_call(
        kern,
        out_shape=(
            jax.ShapeDtypeStruct((batch * seq, D_MODEL), f32),
            jax.ShapeDtypeStruct((batch, H_A, DK_A, HEAD), f32),
            jax.ShapeDtypeStruct((batch, H_A, HEAD, HEAD), f32),
        ),
        grid=(batch // bt,),
        in_specs=[
            rows(D_MODEL), rows(K_W), rows(K_W), rows(D_MODEL), rows(D_MODEL),
            pl.BlockSpec((bt, K_W), lambda i: (i, 0)),
            sa_spec, sb_spec,
            _const_spec(ona.shape),
            _const_spec(onb.shape),
        ],
        out_specs=(rows(D_MODEL), sa_spec, sb_spec),
        scratch_shapes=[
            pltpu.VMEM((bt * N_HEADS, seq, seq), f32),
            pltpu.VMEM((bt * seq, D_MODEL), f32),
        ],
        compiler_params=pltpu.CompilerParams(
            dimension_semantics=("arbitrary",), vmem_limit_bytes=VMEM_LIMIT),
        name="mixer_sample",
    )(*acts, sa_in, sb_in, ona, onb)


def _out_kernel(o_ref, x_ref, w_o_ref, gf_ref, w1_ref, w3_ref, w2_ref, gl_ref, y_ref):
    x1 = x_ref[...] + jnp.dot(o_ref[...].astype(bf16), w_o_ref[...], preferred_element_type=f32)
    h = _rmsnorm(x1, gf_ref[...]).astype(bf16)
    acc = x1
    for c in range(D_FF // FF_CHUNK):
        cols = slice(c * FF_CHUNK, (c + 1) * FF_CHUNK)
        a = jnp.dot(h, w1_ref[:, cols], preferred_element_type=f32)
        b = jnp.dot(h, w3_ref[:, cols], preferred_element_type=f32)
        u = (_silu(a) * b).astype(bf16)
        acc = acc + jnp.dot(u, w2_ref[cols, :], preferred_element_type=f32)
    y_ref[...] = _rmsnorm(acc, gl_ref[...])


def _out(o2d, x2d, w_o, g_ffn, w1, w3, w2, g_final, tm):
    n = x2d.shape[0]
    row_spec = pl.BlockSpec((tm, D_MODEL), lambda i: (i, 0))
    return pl.pallas_call(
        _out_kernel,
        out_shape=jax.ShapeDtypeStruct((n, D_MODEL), f32),
        grid=(n // tm,),
        in_specs=[
            row_spec, row_spec,
            _const_spec((D_MODEL, D_MODEL)),
            _const_spec((1, D_MODEL)),
            _const_spec((D_MODEL, D_FF)),
            _const_spec((D_MODEL, D_FF)),
            _const_spec((D_FF, D_MODEL)),
            _const_spec((1, D_MODEL)),
        ],
        out_specs=row_spec,
        compiler_params=pltpu.CompilerParams(
            dimension_semantics=("arbitrary",), vmem_limit_bytes=VMEM_LIMIT),
        name="out_ffn",
    )(o2d, x2d, w_o, g_ffn, w1, w3, w2, g_final)


_W_IN_SECTIONS = []
_src = 0
for _width, _dst in ((KA_W, OFF_QA), (KA_W, OFF_KA), (SEC, OFF_VA), (SEC, OFF_GA),
                     (GATE_RANK, OFF_LR), (SEC, OFF_QB), (SEC, OFF_FB), (SEC, OFF_IB),
                     (SEC, OFF_GB)):
    _W_IN_SECTIONS.append((_src, _width, _dst))
    _src += _width
D_IN = _src


def _w_in_layout_kernel(wt_ref, o_ref):
    for src, width, dst in _W_IN_SECTIONS:
        o_ref[dst:dst + width, :] = wt_ref[src:src + width, :].astype(o_ref.dtype)
        if width % HEAD:
            pad = HEAD - width % HEAD
            o_ref[dst + width:dst + width + pad, :] = jnp.zeros((pad, o_ref.shape[1]), o_ref.dtype)


def _layout_w_in(w_in_t, cols=256):
    return pl.pallas_call(
        _w_in_layout_kernel,
        out_shape=jax.ShapeDtypeStruct((D_P, D_MODEL), bf16),
        grid=(D_MODEL // cols,),
        in_specs=[pl.BlockSpec((D_IN, cols), lambda i: (0, i))],
        out_specs=pl.BlockSpec((D_P, cols), lambda i: (0, i)),
        compiler_params=pltpu.CompilerParams(
            dimension_semantics=("arbitrary",), vmem_limit_bytes=VMEM_LIMIT),
        name="w_in_layout",
    )(w_in_t)


def kernel(x_prompt, x_sample, state_gla, state_hgrn, norm_mix, w_in, w_a2, b_a, lb_param,
           onorm_a, onorm_b, w_o, norm_ffn, w1, w3, w2, norm_final):
    batch, seq, _ = x_prompt.shape
    dec_batch, dec_seq, _ = x_sample.shape

    w_in_p = _layout_w_in(jnp.swapaxes(w_in[0], 0, 1))
    w_a2p = jnp.pad(w_a2[0], ((0, HEAD - GATE_RANK), (0, 0))).astype(bf16)
    b_ap = b_a[0].reshape(1, KA_W)
    g_mix = norm_mix[0].reshape(1, D_MODEL)
    g_ffn = norm_ffn[0].reshape(1, D_MODEL)
    g_fin = norm_final.reshape(1, D_MODEL)
    ona = onorm_a[0].reshape(1, HEAD)
    onb = onorm_b[0].reshape(1, HEAD)
    w_o_b = w_o[0].astype(bf16)
    w1_b = w1[0].astype(bf16)
    w3_b = w3[0].astype(bf16)
    w2_b = w2[0].astype(bf16)

    xp = x_prompt.reshape(batch * seq, D_MODEL)
    xs = x_sample.reshape(dec_batch * dec_seq, D_MODEL)

    acts_p = _proj(xp, g_mix, w_in_p, w_a2p, b_ap, lb_param,
                   tm=512, chunk=PROMPT_CHUNK, paired=True)
    op, sa_p, sb_p = _mixer_prompt(acts_p, ona, onb, batch, seq, tile=512)
    yp = _out(op, xp, w_o_b, g_ffn, w1_b, w3_b, w2_b, g_fin, tm=512)

    q_s, kin_s, ko_s, d_s, v_s, g_s = _proj(xs, g_mix, w_in_p, w_a2p, b_ap, lb_param,
                                            tm=512, chunk=dec_seq, paired=False)
    os_, sa_s, sb_s = _mixer_sample((q_s, kin_s, ko_s, v_s, g_s, d_s), state_gla[0],
                                    state_hgrn[0], ona, onb, dec_batch, dec_seq, bt=16)
    ys = _out(os_, xs, w_o_b, g_ffn, w1_b, w3_b, w2_b, g_fin, tm=512)

    return (yp.reshape(batch, seq, D_MODEL), ys.reshape(dec_batch, dec_seq, D_MODEL),
            sa_p[None], sb_p[None], sa_s[None], sb_s[None])
```

```python
import functools

import jax
import jax.numpy as jnp
from jax import lax
from jax.experimental import pallas as pl
from jax.experimental.pallas import tpu as pltpu

D_MODEL = 1024
N_HEADS = 8
H_A = 4
DK_A = 64
HEAD = 128
GATE_RANK = 16
GATE_NORM = 16.0
D_FF = 2816
FF_CHUNK = 256
EPS = 1e-6
PROMPT_CHUNK = 32
SUBLANES = 8

KA_W = H_A * DK_A
SEC = H_A * HEAD
K_W = KA_W + SEC
OFF_LR = 0
OFF_QA = HEAD
OFF_KA = OFF_QA + KA_W
OFF_VA = OFF_KA + KA_W
OFF_GA = OFF_VA + SEC
OFF_FB = OFF_GA + SEC
OFF_QB = OFF_FB + SEC
OFF_IB = OFF_QB + SEC
OFF_GB = OFF_IB + SEC
D_P = OFF_GB + SEC
MXU_N = 256
PROJ_GROUPS = ((OFF_LR, OFF_VA), (OFF_FB, OFF_IB), (OFF_VA, OFF_FB), (OFF_IB, D_P))

ROW_TILE = 512
SAMPLE_SEQS = 16
VMEM_LIMIT = 56 * 1024 * 1024

f32 = jnp.float32
bf16 = jnp.bfloat16


def _const_spec(shape):
    nd = len(shape)
    return pl.BlockSpec(shape, lambda *_: (0,) * nd, pipeline_mode=pl.Buffered(1))


def _rmsnorm(x, g):
    return x * lax.rsqrt(jnp.mean(x * x, axis=-1, keepdims=True) + EPS) * g


def _sigmoid(x):
    return 0.5 * jnp.tanh(0.5 * x) + 0.5


def _silu(x):
    hx = 0.5 * x
    return hx + hx * jnp.tanh(hx)


def _dot_nt(a, b):
    return lax.dot_general(a, b, (((1,), (1,)), ((), ())), preferred_element_type=f32)


def _dot_tn(a, b):
    return lax.dot_general(a, b, (((0,), (0,)), ((), ())), preferred_element_type=f32)


def _key_lanes(h):
    return HEAD * (h // 2) if h < H_A else KA_W + HEAD * (h - H_A)


_KEY_SHARING = tuple((h, h + 1) for h in range(0, H_A, 2)) + tuple(
    (h,) for h in range(H_A, N_HEADS))


def _own_keys(h):
    return slice(DK_A * (h % 2), DK_A * (h % 2 + 1)) if h < H_A else slice(0, HEAD)


def _cumsum_rows(x, chunk):
    rows, width = x.shape
    row = lax.broadcasted_iota(jnp.int32, (SUBLANES, width), 0)
    groups_per_chunk = chunk // SUBLANES
    out = []
    carry = None
    for g in range(rows // SUBLANES):
        y = x[g * SUBLANES:(g + 1) * SUBLANES]
        for s in (1, 2, 4):
            y = y + jnp.where(row >= s, pltpu.roll(y, s, 0), 0.0)
        if g % groups_per_chunk != 0:
            y = y + carry
        carry = y[SUBLANES - 1:SUBLANES]
        out.append(y)
    return out[0] if len(out) == 1 else jnp.concatenate(out, axis=0)


def _lower_bound(lbp_ref):
    lbp = lbp_ref[...]
    m = jnp.max(lbp, axis=0, keepdims=True)
    e = jnp.exp(lbp - m)
    return e[0:1] / jnp.sum(e, axis=0, keepdims=True)


def _decays(la, chunk):
    rows = la.shape[0]
    b = _cumsum_rows(la, chunk)
    n_chunks = rows // chunk
    lasts = [b[(c + 1) * chunk - 1:(c + 1) * chunk] for c in range(n_chunks)]
    if n_chunks == 1:
        b_last = lasts[0]
        b_last_rows = b_last
    else:
        b_last = jnp.concatenate(lasts, axis=0)
        b_last_rows = jnp.concatenate(
            [jnp.broadcast_to(l, (chunk, l.shape[1])) for l in lasts], axis=0)
    return jnp.exp(b), jnp.exp(-b), jnp.exp(b_last_rows - b), jnp.exp(b_last)


def _emit_keys_paired(refs, r, q_targets, kl, q, k, la):
    q32_ref, q64_ref, kin_ref, kx_ref, ko_ref, d_ref = refs
    c = PROMPT_CHUNK
    rows = slice(2 * c * r, 2 * c * (r + 1))
    e_b, e_nb, e_out, decay = _decays(la, c)
    d0, d1 = decay[0:1], decay[1:2]
    q32 = q * e_b
    q64 = jnp.concatenate([q32[:c], q32[c:] * d0], axis=0)
    k_in = k * e_nb
    k_out = k * e_out
    for ql, own in q_targets:
        q32_ref[rows, ql] = (q32 if own is None else jnp.where(own, q32, 0.0)).astype(bf16)
        q64_ref[rows, ql] = (q64 if own is None else jnp.where(own, q64, 0.0)).astype(bf16)
    kin_ref[c * r:c * (r + 1), kl] = k_in[:c].astype(bf16)
    kx_ref[rows, kl] = jnp.concatenate([k_out[:c], k_in[c:]], axis=0).astype(bf16)
    ko_ref[rows, kl] = jnp.concatenate([k_out[:c] * d1, k_out[c:]], axis=0).astype(bf16)
    d_ref[r:r + 1, kl] = d0 * d1


def _emit_keys_single(refs, r, q_targets, kl, q, k, la, *, chunk):
    q_ref, kin_ref, ko_ref, d_ref = refs
    rows = slice(SUBLANES * chunk * r, SUBLANES * chunk * (r + 1))
    e_b, e_nb, e_out, decay = _decays(la, chunk)
    q_in = q * e_b
    for ql, own in q_targets:
        q_ref[rows, ql] = q_in if own is None else jnp.where(own, q_in, 0.0)
    kin_ref[rows, kl] = k * e_nb
    ko_ref[rows, kl] = k * e_out
    d_ref[SUBLANES * r:SUBLANES * (r + 1), kl] = decay


def _proj_kernel(x_ref, gm_ref, w_ref, w_a2_ref, b_a_ref, lbp_ref, *refs, chunk, paired):
    n_key_refs = 6 if paired else 4
    key_refs = refs[:n_key_refs]
    v_ref, g_ref = refs[n_key_refs:n_key_refs + 2]
    p_refs = refs[n_key_refs + 2:]
    tm = x_ref.shape[0]
    block = 2 * chunk if paired else SUBLANES * chunk
    emit_keys = (_emit_keys_paired if paired
                 else functools.partial(_emit_keys_single, chunk=chunk))
    blocks = [(slice(r * block, (r + 1) * block), r) for r in range(tm // block)]
    h = _rmsnorm(x_ref[...], gm_ref[...]).astype(bf16)

    w_a2 = w_a2_ref[...]
    b_a = b_a_ref[...]
    lb = _lower_bound(lbp_ref)
    lane = lax.broadcasted_iota(jnp.int32, (block, HEAD), 1)

    def p_slot(lo, hi):
        for (g_lo, g_hi), ref in zip(PROJ_GROUPS, p_refs):
            if g_lo <= lo and hi <= g_hi:
                return ref, slice(lo - g_lo, hi - g_lo)
        raise ValueError((lo, hi))

    def project(lo, hi):
        ref, cols = p_slot(lo, hi)
        ref[:, cols] = _dot_nt(h, w_ref[lo:hi, :])

    def p(rows, lo, width):
        ref, cols = p_slot(lo, lo + width)
        return ref[rows, cols]

    def gla_keys(rows, r):
        lra = p(rows, OFF_LR, HEAD)
        z = jnp.dot(lra.astype(bf16), w_a2, preferred_element_type=f32) + b_a
        for pr in range(H_A // 2):
            kl = slice(pr * HEAD, (pr + 1) * HEAD)
            zz = z[:, kl]
            la = (jnp.minimum(zz, 0.0) - jnp.log(1.0 + jnp.exp(-jnp.abs(zz)))) * (1.0 / GATE_NORM)
            q = p(rows, OFF_QA + pr * HEAD, HEAD) * (DK_A ** -0.5)
            k = p(rows, OFF_KA + pr * HEAD, HEAD)
            q_targets = [(slice(hd * HEAD, (hd + 1) * HEAD), own)
                         for hd, own in ((2 * pr, lane < DK_A), (2 * pr + 1, lane >= DK_A))]
            emit_keys(key_refs, r, q_targets, kl, q, k, la)

    def hgrn_keys(rows, r):
        for hd in range(N_HEADS - H_A):
            lbh = lb[:, hd * HEAD:(hd + 1) * HEAD]
            f = lbh + (1.0 - lbh) * _sigmoid(p(rows, OFF_FB + hd * HEAD, HEAD))
            q = _silu(p(rows, OFF_QB + hd * HEAD, HEAD))
            ql = slice(SEC + hd * HEAD, SEC + (hd + 1) * HEAD)
            kl = slice(KA_W + hd * HEAD, KA_W + (hd + 1) * HEAD)
            emit_keys(key_refs, r, [(ql, None)], kl, q, 1.0 - f, jnp.log(f))

    def gla_values(rows, crow):
        v_ref[rows, 0:SEC] = p(rows, OFF_VA, SEC).astype(v_ref.dtype)
        g_ref[rows, 0:SEC] = p(rows, OFF_GA, SEC)

    def hgrn_values(rows, crow):
        v_ref[rows, SEC:2 * SEC] = p(rows, OFF_IB, SEC).astype(v_ref.dtype)
        g_ref[rows, SEC:2 * SEC] = p(rows, OFF_GB, SEC)

    def tasks(fn, which):
        return [functools.partial(fn, *blocks[r]) for r in which]

    nb = len(blocks)
    first, second = range(nb // 2), range(nb // 2, nb)
    windows = (
        [],
        tasks(gla_keys, range(nb)),
        tasks(hgrn_keys, first),
        tasks(hgrn_keys, second) + tasks(gla_values, range(nb)),
    )
    for (lo, hi), work in zip(PROJ_GROUPS, windows):
        starts = list(range(lo, hi, MXU_N))
        for i, c0 in enumerate(starts):
            project(c0, min(c0 + MXU_N, hi))
            for t in work[i * len(work) // len(starts):(i + 1) * len(work) // len(starts)]:
                t()
    for t in tasks(hgrn_values, range(nb)):
        t()


def _proj(x2d, g_mix, w_in_p, w_a2p, b_ap, lb_param, *, chunk, paired):
    n = x2d.shape[0]
    tm = ROW_TILE
    kern = functools.partial(_proj_kernel, chunk=chunk, paired=paired)

    def arr(rows, width, dtype):
        return (jax.ShapeDtypeStruct((n // rows, width), dtype),
                pl.BlockSpec((tm // rows, width), lambda i: (i, 0)))

    if paired:
        keys = [arr(1, D_MODEL, bf16), arr(1, D_MODEL, bf16), arr(2, K_W, bf16),
                arr(1, K_W, bf16), arr(1, K_W, bf16), arr(2 * chunk, K_W, f32)]
        v_out = arr(1, D_MODEL, bf16)
    else:
        keys = [arr(1, D_MODEL, f32), arr(1, K_W, f32), arr(1, K_W, f32), arr(chunk, K_W, f32)]
        v_out = arr(1, D_MODEL, f32)
    outs = keys + [v_out, arr(1, D_MODEL, f32)]

    return pl.pallas_call(
        kern,
        out_shape=tuple(o[0] for o in outs),
        grid=(n // tm,),
        in_specs=[
            pl.BlockSpec((tm, D_MODEL), lambda i: (i, 0)),
            _const_spec((1, D_MODEL)),
            _const_spec((D_P, D_MODEL)),
            _const_spec(w_a2p.shape),
            _const_spec(b_ap.shape),
            _const_spec(lb_param.shape),
        ],
        out_specs=tuple(o[1] for o in outs),
        scratch_shapes=[pltpu.VMEM((tm, hi - lo), f32) for lo, hi in PROJ_GROUPS],
        compiler_params=pltpu.CompilerParams(
            dimension_semantics=("arbitrary",), vmem_limit_bytes=VMEM_LIMIT),
        name="proj",
    )(x2d, g_mix, w_in_p, w_a2p, b_ap, lb_param)


def _head_norm(o, onorm):
    return o * lax.rsqrt(jnp.mean(o * o, axis=-1, keepdims=True) + EPS) * onorm


def _tril(n):
    r = lax.broadcasted_iota(jnp.int32, (n, n), 0)
    c = lax.broadcasted_iota(jnp.int32, (n, n), 1)
    return r >= c


def _mixer_prompt_kernel(q32_ref, q64_ref, kin_ref, kx_ref, ko_ref, d_ref, v_ref, g_ref,
                         ona_ref, onb_ref, o_ref, sa_ref, sb_ref,
                         st_ref, att_ref, up_ref, start_ref, *, tile, chunk):
    j = pl.program_id(1)

    @pl.when(j == 0)
    def _():
        st_ref[...] = jnp.zeros_like(st_ref)

    onorm = [ona_ref[...]] * H_A + [onb_ref[...]] * (N_HEADS - H_A)
    pair = 2 * chunk
    n_pairs = tile // pair
    stack = max(len(heads) for heads in _KEY_SHARING) * chunk
    r = lax.broadcasted_iota(jnp.int32, (stack, pair), 0) % chunk
    c = lax.broadcasted_iota(jnp.int32, (stack, pair), 1)
    tril = (r >= c)[:, :chunk]
    cross = (c < chunk) | (r >= c - chunk)

    for heads in _KEY_SHARING:
        n = len(heads)
        kl = slice(_key_lanes(heads[0]), _key_lanes(heads[0]) + HEAD)
        vl = slice(heads[0] * HEAD, (heads[-1] + 1) * HEAD)
        for s in range(n_pairs):
            ra = slice(s * pair, s * pair + chunk)
            rb = slice(s * pair + chunk, (s + 1) * pair)
            rp = slice(s * pair, (s + 1) * pair)
            qa = [q32_ref[ra, h * HEAD:(h + 1) * HEAD] for h in heads]
            qb = [q32_ref[rb, h * HEAD:(h + 1) * HEAD] for h in heads]
            qa = qa[0] if n == 1 else jnp.concatenate(qa, axis=0)
            qb = qb[0] if n == 1 else jnp.concatenate(qb, axis=0)
            a_keys = kin_ref[s * chunk:(s + 1) * chunk, kl]
            att0 = jnp.where(tril[:n * chunk], _dot_nt(qa, a_keys), 0.0).astype(bf16)
            att1 = jnp.where(cross[:n * chunk], _dot_nt(qb, kx_ref[rp, kl]), 0.0).astype(bf16)
            up = _dot_tn(v_ref[rp, vl], ko_ref[rp, kl])
            att0 = jnp.concatenate([att0, jnp.zeros_like(att0)], axis=1)
            for i, h in enumerate(heads):
                u = h * n_pairs + s
                att_ref[u, :chunk] = att0[i * chunk:(i + 1) * chunk]
                att_ref[u, chunk:] = att1[i * chunk:(i + 1) * chunk]
                up_ref[u] = up[i * HEAD:(i + 1) * HEAD]

    for h in range(N_HEADS):
        kl = slice(_key_lanes(h), _key_lanes(h) + HEAD)
        st = st_ref[h]
        for s in range(n_pairs):
            u = h * n_pairs + s
            start_ref[u] = st.astype(bf16)
            st = d_ref[s:s + 1, kl] * st + up_ref[u]
        st_ref[h] = st

    for h in range(N_HEADS):
        sl = slice(h * HEAD, (h + 1) * HEAD)
        outs = []
        for s in range(n_pairs):
            u = h * n_pairs + s
            rp = slice(s * pair, (s + 1) * pair)
            outs.append(_dot_nt(q64_ref[rp, sl], start_ref[u])
                        + jnp.dot(att_ref[u], v_ref[rp, sl], preferred_element_type=f32))
        o = jnp.concatenate(outs, axis=0)
        o_ref[:, sl] = (_head_norm(o, onorm[h]) * _silu(g_ref[:, sl])).astype(o_ref.dtype)

    @pl.when(j == pl.num_programs(1) - 1)
    def _():
        for h in range(H_A):
            sa_ref[0, h] = st_ref[h].T[_own_keys(h), :]
            sb_ref[0, h] = st_ref[H_A + h].T


def _mixer_prompt(acts, ona, onb, batch, seq):
    tile = ROW_TILE
    nt = seq // tile
    chunk = PROMPT_CHUNK
    units = N_HEADS * (tile // (2 * chunk))
    kern = functools.partial(_mixer_prompt_kernel, tile=tile, chunk=chunk)

    def rows(width, per=1):
        return pl.BlockSpec((tile // per, width), lambda b, j: (b * nt + j, 0))

    return pl.pallas_call(
        kern,
        out_shape=(
            jax.ShapeDtypeStruct((batch * seq, D_MODEL), bf16),
            jax.ShapeDtypeStruct((batch, H_A, DK_A, HEAD), f32),
            jax.ShapeDtypeStruct((batch, H_A, HEAD, HEAD), f32),
        ),
        grid=(batch, nt),
        in_specs=[
            rows(D_MODEL), rows(D_MODEL), rows(K_W, 2), rows(K_W), rows(K_W),
            rows(K_W, 2 * chunk), rows(D_MODEL), rows(D_MODEL),
            _const_spec(ona.shape),
            _const_spec(onb.shape),
        ],
        out_specs=(
            rows(D_MODEL),
            pl.BlockSpec((1, H_A, DK_A, HEAD), lambda b, j: (b, 0, 0, 0)),
            pl.BlockSpec((1, H_A, HEAD, HEAD), lambda b, j: (b, 0, 0, 0)),
        ),
        scratch_shapes=[
            pltpu.VMEM((N_HEADS, HEAD, HEAD), f32),
            pltpu.VMEM((units, 2 * chunk, 2 * chunk), bf16),
            pltpu.VMEM((units, HEAD, HEAD), f32),
            pltpu.VMEM((units, HEAD, HEAD), bf16),
        ],
        compiler_params=pltpu.CompilerParams(
            dimension_semantics=("arbitrary", "arbitrary"), vmem_limit_bytes=VMEM_LIMIT),
        name="mixer_prompt",
    )(*acts, ona, onb)


def _mixer_sample_kernel(qi_ref, ki_ref, ko_ref, v_ref, g_ref, d_ref, sa_in_ref, sb_in_ref,
                         ona_ref, onb_ref, o_ref, sa_ref, sb_ref, att_ref, oi_ref, *, bt, seq):
    onorm = [ona_ref[...]] * H_A + [onb_ref[...]] * (N_HEADS - H_A)
    tril = _tril(seq)
    zpad = jnp.zeros((HEAD - DK_A, HEAD), f32)
    zeros = jnp.zeros((seq, HEAD), f32)
    row = lax.broadcasted_iota(jnp.int32, (seq, HEAD), 0)
    ones_rows = jnp.where(row < 2, 1.0, 0.0)
    rhs_bottom = jnp.concatenate([zeros, ones_rows], axis=1)

    for bi in range(bt):
        rows = slice(bi * seq, (bi + 1) * seq)
        decay = d_ref[bi:bi + 1, :]
        d_hi = decay.astype(bf16).astype(f32)
        d_lo = decay - d_hi
        for h in range(N_HEADS):
            sl = slice(h * HEAD, (h + 1) * HEAD)
            kl = slice(_key_lanes(h), _key_lanes(h) + HEAD)
            q_in = qi_ref[rows, sl].astype(bf16)
            k_in = ki_ref[rows, kl].astype(bf16)
            v32 = v_ref[rows, sl]
            if h >= H_A:
                s = sb_in_ref[bi, h - H_A]
            elif h % 2 == 0:
                s = jnp.concatenate([sa_in_ref[bi, h], zpad], axis=0)
            else:
                s = jnp.concatenate([zpad, sa_in_ref[bi, h]], axis=0)
            att_ref[bi * N_HEADS + h] = jnp.where(tril, _dot_nt(q_in, k_in), 0.0)
            oi_ref[rows, sl] = jnp.dot(q_in, s.astype(bf16), preferred_element_type=f32)
            d_rows = jnp.where(row == 0, d_hi[:, kl], jnp.where(row == 1, d_lo[:, kl], 0.0))
            lhs = jnp.concatenate([ko_ref[rows, kl], d_rows], axis=0).astype(bf16)
            rhs = jnp.concatenate(
                [jnp.concatenate([v32, zeros], axis=1), rhs_bottom], axis=0).astype(bf16)
            upd = _dot_tn(lhs, rhs)
            s_new = upd[:, HEAD:] * s + upd[:, :HEAD]
            if h < H_A:
                sa_ref[bi, h] = s_new[_own_keys(h), :]
            else:
                sb_ref[bi, h - H_A] = s_new

    for bi in range(bt):
        rows = slice(bi * seq, (bi + 1) * seq)
        for h in range(N_HEADS):
            sl = slice(h * HEAD, (h + 1) * HEAD)
            att = att_ref[bi * N_HEADS + h].astype(bf16)
            o = oi_ref[rows, sl] + jnp.dot(att, v_ref[rows, sl].astype(bf16),
                                           preferred_element_type=f32)
            o_ref[rows, sl] = (_head_norm(o, onorm[h]) * _silu(g_ref[rows, sl])).astype(
                o_ref.dtype)


def _mixer_sample(acts, sa_in, sb_in, ona, onb, batch, seq):
    bt = SAMPLE_SEQS
    kern = functools.partial(_mixer_sample_kernel, bt=bt, seq=seq)

    def rows(width):
        return pl.BlockSpec((bt * seq, width), lambda i: (i, 0))

    sa_spec = pl.BlockSpec((bt, H_A, DK_A, HEAD), lambda i: (i, 0, 0, 0))
    sb_spec = pl.BlockSpec((bt, H_A, HEAD, HEAD), lambda i: (i, 0, 0, 0))
    return pl.pallas_call(
        kern,
        out_shape=(
            jax.ShapeDtypeStruct((batch * seq, D_MODEL), f32),
            jax.ShapeDtypeStruct((batch, H_A, DK_A, HEAD), f32),
            jax.ShapeDtypeStruct((batch, H_A, HEAD, HEAD), f32),
        ),
        grid=(batch // bt,),
        in_specs=[
            rows(D_MODEL), rows(K_W), rows(K_W), rows(D_MODEL), rows(D_MODEL),
            pl.BlockSpec((bt, K_W), lambda i: (i, 0)),
            sa_spec, sb_spec,
            _const_spec(ona.shape),
            _const_spec(onb.shape),
        ],
        out_specs=(rows(D_MODEL), sa_spec, sb_spec),
        scratch_shapes=[
            pltpu.VMEM((bt * N_HEADS, seq, seq), f32),
            pltpu.VMEM((bt * seq, D_MODEL), f32),
        ],
        compiler_params=pltpu.CompilerParams(
            dimension_semantics=("arbitrary",), vmem_limit_bytes=VMEM_LIMIT),
        name="mixer_sample",
    )(*acts, sa_in, sb_in, ona, onb)


def _out_kernel(o_ref, x_ref, w_o_ref, gf_ref, w1_ref, w3_ref, w2_ref, gl_ref, y_ref):
    x1 = x_ref[...] + jnp.dot(o_ref[...].astype(bf16), w_o_ref[...], preferred_element_type=f32)
    h = _rmsnorm(x1, gf_ref[...]).astype(bf16)
    acc = x1
    for c in range(D_FF // FF_CHUNK):
        cols = slice(c * FF_CHUNK, (c + 1) * FF_CHUNK)
        a = jnp.dot(h, w1_ref[:, cols], preferred_element_type=f32)
        b = jnp.dot(h, w3_ref[:, cols], preferred_element_type=f32)
        u = (_silu(a) * b).astype(bf16)
        acc = acc + jnp.dot(u, w2_ref[cols, :], preferred_element_type=f32)
    y_ref[...] = _rmsnorm(acc, gl_ref[...])


def _out(o2d, x2d, w_o, g_ffn, w1, w3, w2, g_final):
    n = x2d.shape[0]
    row_spec = pl.BlockSpec((ROW_TILE, D_MODEL), lambda i: (i, 0))
    return pl.pallas_call(
        _out_kernel,
        out_shape=jax.ShapeDtypeStruct((n, D_MODEL), f32),
        grid=(n // ROW_TILE,),
        in_specs=[
            row_spec, row_spec,
            _const_spec((D_MODEL, D_MODEL)),
            _const_spec((1, D_MODEL)),
            _const_spec((D_MODEL, D_FF)),
            _const_spec((D_MODEL, D_FF)),
            _const_spec((D_FF, D_MODEL)),
            _const_spec((1, D_MODEL)),
        ],
        out_specs=row_spec,
        compiler_params=pltpu.CompilerParams(
            dimension_semantics=("arbitrary",), vmem_limit_bytes=VMEM_LIMIT),
        name="out_ffn",
    )(o2d, x2d, w_o, g_ffn, w1, w3, w2, g_final)


def _w_in_sections():
    sections, src = [], 0
    for width, dst in ((KA_W, OFF_QA), (KA_W, OFF_KA), (SEC, OFF_VA), (SEC, OFF_GA),
                       (GATE_RANK, OFF_LR), (SEC, OFF_QB), (SEC, OFF_FB), (SEC, OFF_IB),
                       (SEC, OFF_GB)):
        sections.append((src, width, dst))
        src += width
    return tuple(sections), src


_W_IN_SECTIONS, D_IN = _w_in_sections()


def _w_in_layout_kernel(wt_ref, o_ref):
    for src, width, dst in _W_IN_SECTIONS:
        o_ref[dst:dst + width, :] = wt_ref[src:src + width, :].astype(o_ref.dtype)
        if width % HEAD:
            pad = HEAD - width % HEAD
            o_ref[dst + width:dst + width + pad, :] = jnp.zeros((pad, o_ref.shape[1]), o_ref.dtype)


def _layout_w_in(w_in_t, cols=256):
    return pl.pallas_call(
        _w_in_layout_kernel,
        out_shape=jax.ShapeDtypeStruct((D_P, D_MODEL), bf16),
        grid=(D_MODEL // cols,),
        in_specs=[pl.BlockSpec((D_IN, cols), lambda i: (0, i))],
        out_specs=pl.BlockSpec((D_P, cols), lambda i: (0, i)),
        compiler_params=pltpu.CompilerParams(
            dimension_semantics=("arbitrary",), vmem_limit_bytes=VMEM_LIMIT),
        name="w_in_layout",
    )(w_in_t)


def kernel(x_prompt, x_sample, state_gla, state_hgrn, norm_mix, w_in, w_a2, b_a, lb_param,
           onorm_a, onorm_b, w_o, norm_ffn, w1, w3, w2, norm_final):
    batch, seq, _ = x_prompt.shape
    dec_batch, dec_seq, _ = x_sample.shape

    w_in_p = _layout_w_in(jnp.swapaxes(w_in[0], 0, 1))
    w_a2p = jnp.pad(w_a2[0], ((0, HEAD - GATE_RANK), (0, 0))).astype(bf16)
    b_ap = b_a[0].reshape(1, KA_W)
    g_mix = norm_mix[0].reshape(1, D_MODEL)
    g_ffn = norm_ffn[0].reshape(1, D_MODEL)
    g_fin = norm_final.reshape(1, D_MODEL)
    ona = onorm_a[0].reshape(1, HEAD)
    onb = onorm_b[0].reshape(1, HEAD)
    w_o_b = w_o[0].astype(bf16)
    w1_b = w1[0].astype(bf16)
    w3_b = w3[0].astype(bf16)
    w2_b = w2[0].astype(bf16)

    xp = x_prompt.reshape(batch * seq, D_MODEL)
    xs = x_sample.reshape(dec_batch * dec_seq, D_MODEL)

    acts_p = _proj(xp, g_mix, w_in_p, w_a2p, b_ap, lb_param, chunk=PROMPT_CHUNK, paired=True)
    op, sa_p, sb_p = _mixer_prompt(acts_p, ona, onb, batch, seq)
    yp = _out(op, xp, w_o_b, g_ffn, w1_b, w3_b, w2_b, g_fin)

    q_s, kin_s, ko_s, d_s, v_s, g_s = _proj(xs, g_mix, w_in_p, w_a2p, b_ap, lb_param,
                                            chunk=dec_seq, paired=False)
    os_, sa_s, sb_s = _mixer_sample((q_s, kin_s, ko_s, v_s, g_s, d_s), state_gla[0],
                                    state_hgrn[0], ona, onb, dec_batch, dec_seq)
    ys = _out(os_, xs, w_o_b, g_ffn, w1_b, w3_b, w2_b, g_fin)

    return (yp.reshape(batch, seq, D_MODEL), ys.reshape(dec_batch, dec_seq, D_MODEL),
            sa_p[None], sb_p[None], sa_s[None], sb_s[None])
```

```python
import functools

import jax
import jax.numpy as jnp
from jax import lax
from jax.experimental import pallas as pl
from jax.experimental.pallas import tpu as pltpu

D_MODEL = 1024
N_HEADS = 8
H_A = 4
DK_A = 64
HEAD = 128
GATE_RANK = 16
GATE_NORM = 16.0
D_FF = 2816
FF_CHUNK = 256
EPS = 1e-6
PROMPT_CHUNK = 32
SUBLANES = 8

KA_W = H_A * DK_A
SEC = H_A * HEAD
K_W = KA_W + SEC
OFF_LR = 0
OFF_QA = HEAD
OFF_KA = OFF_QA + KA_W
OFF_VA = OFF_KA + KA_W
OFF_GA = OFF_VA + SEC
OFF_FB = OFF_GA + SEC
OFF_QB = OFF_FB + SEC
OFF_IB = OFF_QB + SEC
OFF_GB = OFF_IB + SEC
D_P = OFF_GB + SEC
MXU_N = 256
PROJ_GROUPS = ((OFF_LR, OFF_VA), (OFF_FB, OFF_IB), (OFF_VA, OFF_FB), (OFF_IB, D_P))

ROW_TILE = 512
SAMPLE_SEQS = 16
VMEM_LIMIT = 56 * 1024 * 1024

f32 = jnp.float32
bf16 = jnp.bfloat16


def _const_spec(shape):
    nd = len(shape)
    return pl.BlockSpec(shape, lambda *_: (0,) * nd, pipeline_mode=pl.Buffered(1))


def _rmsnorm(x, g):
    return x * lax.rsqrt(jnp.mean(x * x, axis=-1, keepdims=True) + EPS) * g


def _sigmoid(x):
    return 0.5 * jnp.tanh(0.5 * x) + 0.5


def _silu(x):
    hx = 0.5 * x
    return hx + hx * jnp.tanh(hx)


def _dot_nt(a, b):
    return lax.dot_general(a, b, (((1,), (1,)), ((), ())), preferred_element_type=f32)


def _dot_tn(a, b):
    return lax.dot_general(a, b, (((0,), (0,)), ((), ())), preferred_element_type=f32)


def _key_lanes(h):
    return HEAD * (h // 2) if h < H_A else KA_W + HEAD * (h - H_A)


_KEY_SHARING = tuple((h, h + 1) for h in range(0, H_A, 2)) + tuple(
    (h,) for h in range(H_A, N_HEADS))


def _own_keys(h):
    return slice(DK_A * (h % 2), DK_A * (h % 2 + 1)) if h < H_A else slice(0, HEAD)


def _cumsum_rows(x, chunk):
    rows, width = x.shape
    row = lax.broadcasted_iota(jnp.int32, (SUBLANES, width), 0)
    groups_per_chunk = chunk // SUBLANES
    out = []
    carry = None
    for g in range(rows // SUBLANES):
        y = x[g * SUBLANES:(g + 1) * SUBLANES]
        for s in (1, 2, 4):
            y = y + jnp.where(row >= s, pltpu.roll(y, s, 0), 0.0)
        if g % groups_per_chunk != 0:
            y = y + carry
        carry = y[SUBLANES - 1:SUBLANES]
        out.append(y)
    return out[0] if len(out) == 1 else jnp.concatenate(out, axis=0)


def _lower_bound(lbp_ref):
    lbp = lbp_ref[...]
    m = jnp.max(lbp, axis=0, keepdims=True)
    e = jnp.exp(lbp - m)
    return e[0:1] / jnp.sum(e, axis=0, keepdims=True)


def _decays(la, chunk):
    rows = la.shape[0]
    b = _cumsum_rows(la, chunk)
    n_chunks = rows // chunk
    lasts = [b[(c + 1) * chunk - 1:(c + 1) * chunk] for c in range(n_chunks)]
    if n_chunks == 1:
        b_last = lasts[0]
        b_last_rows = b_last
    else:
        b_last = jnp.concatenate(lasts, axis=0)
        b_last_rows = jnp.concatenate(
            [jnp.broadcast_to(l, (chunk, l.shape[1])) for l in lasts], axis=0)
    return jnp.exp(b), jnp.exp(-b), jnp.exp(b_last_rows - b), jnp.exp(b_last)


def _emit_keys_paired(refs, r, q_targets, kl, q, k, la):
    q32_ref, q64_ref, kin_ref, kx_ref, ko_ref, d_ref = refs
    c = PROMPT_CHUNK
    rows = slice(2 * c * r, 2 * c * (r + 1))
    e_b, e_nb, e_out, decay = _decays(la, c)
    d0, d1 = decay[0:1], decay[1:2]
    q32 = q * e_b
    q64 = jnp.concatenate([q32[:c], q32[c:] * d0], axis=0)
    k_in = k * e_nb
    k_out = k * e_out
    for ql, own in q_targets:
        q32_ref[rows, ql] = (q32 if own is None else jnp.where(own, q32, 0.0)).astype(bf16)
        q64_ref[rows, ql] = (q64 if own is None else jnp.where(own, q64, 0.0)).astype(bf16)
    kin_ref[c * r:c * (r + 1), kl] = k_in[:c].astype(bf16)
    kx_ref[rows, kl] = jnp.concatenate([k_out[:c], k_in[c:]], axis=0).astype(bf16)
    ko_ref[rows, kl] = jnp.concatenate([k_out[:c] * d1, k_out[c:]], axis=0).astype(bf16)
    d_ref[r:r + 1, kl] = d0 * d1


def _emit_keys_single(refs, r, q_targets, kl, q, k, la, *, chunk):
    q_ref, kin_ref, ko_ref, d_ref = refs
    rows = slice(SUBLANES * chunk * r, SUBLANES * chunk * (r + 1))
    e_b, e_nb, e_out, decay = _decays(la, chunk)
    q_in = q * e_b
    for ql, own in q_targets:
        q_ref[rows, ql] = q_in if own is None else jnp.where(own, q_in, 0.0)
    kin_ref[rows, kl] = k * e_nb
    ko_ref[rows, kl] = k * e_out
    d_ref[SUBLANES * r:SUBLANES * (r + 1), kl] = decay


def _proj_kernel(x_ref, gm_ref, w_ref, w_a2_ref, b_a_ref, lbp_ref, *refs, chunk, paired):
    n_key_refs = 6 if paired else 4
    key_refs = refs[:n_key_refs]
    v_ref, g_ref = refs[n_key_refs:n_key_refs + 2]
    p_refs = refs[n_key_refs + 2:]
    tm = x_ref.shape[0]
    block = 2 * chunk if paired else SUBLANES * chunk
    emit_keys = (_emit_keys_paired if paired
                 else functools.partial(_emit_keys_single, chunk=chunk))
    blocks = [(slice(r * block, (r + 1) * block), r) for r in range(tm // block)]
    h = _rmsnorm(x_ref[...], gm_ref[...]).astype(bf16)

    w_a2 = w_a2_ref[...]
    b_a = b_a_ref[...]
    lb = _lower_bound(lbp_ref)
    lane = lax.broadcasted_iota(jnp.int32, (block, HEAD), 1)

    def p_slot(lo, hi):
        for (g_lo, g_hi), ref in zip(PROJ_GROUPS, p_refs):
            if g_lo <= lo and hi <= g_hi:
                return ref, slice(lo - g_lo, hi - g_lo)
        raise ValueError((lo, hi))

    def project(lo, hi):
        ref, cols = p_slot(lo, hi)
        ref[:, cols] = jnp.dot(h, w_ref[:, lo:hi], preferred_element_type=f32)

    def p(rows, lo, width):
        ref, cols = p_slot(lo, lo + width)
        return ref[rows, cols]

    def gla_keys(rows, r):
        lra = p(rows, OFF_LR, HEAD)
        z = jnp.dot(lra.astype(bf16), w_a2, preferred_element_type=f32) + b_a
        for pr in range(H_A // 2):
            kl = slice(pr * HEAD, (pr + 1) * HEAD)
            zz = z[:, kl]
            la = (jnp.minimum(zz, 0.0) - jnp.log(1.0 + jnp.exp(-jnp.abs(zz)))) * (1.0 / GATE_NORM)
            q = p(rows, OFF_QA + pr * HEAD, HEAD) * (DK_A ** -0.5)
            k = p(rows, OFF_KA + pr * HEAD, HEAD)
            q_targets = [(slice(hd * HEAD, (hd + 1) * HEAD), own)
                         for hd, own in ((2 * pr, lane < DK_A), (2 * pr + 1, lane >= DK_A))]
            emit_keys(key_refs, r, q_targets, kl, q, k, la)

    def hgrn_keys(rows, r):
        for hd in range(N_HEADS - H_A):
            lbh = lb[:, hd * HEAD:(hd + 1) * HEAD]
            f = lbh + (1.0 - lbh) * _sigmoid(p(rows, OFF_FB + hd * HEAD, HEAD))
            q = _silu(p(rows, OFF_QB + hd * HEAD, HEAD))
            ql = slice(SEC + hd * HEAD, SEC + (hd + 1) * HEAD)
            kl = slice(KA_W + hd * HEAD, KA_W + (hd + 1) * HEAD)
            emit_keys(key_refs, r, [(ql, None)], kl, q, 1.0 - f, jnp.log(f))

    def gla_values(rows, crow):
        v_ref[rows, 0:SEC] = p(rows, OFF_VA, SEC).astype(v_ref.dtype)
        g_ref[rows, 0:SEC] = p(rows, OFF_GA, SEC)

    def hgrn_values(rows, crow):
        v_ref[rows, SEC:2 * SEC] = p(rows, OFF_IB, SEC).astype(v_ref.dtype)
        g_ref[rows, SEC:2 * SEC] = p(rows, OFF_GB, SEC)

    def tasks(fn, which):
        return [functools.partial(fn, *blocks[r]) for r in which]

    nb = len(blocks)
    first, second = range(nb // 2), range(nb // 2, nb)
    windows = (
        [],
        tasks(gla_keys, range(nb)),
        tasks(hgrn_keys, first),
        tasks(hgrn_keys, second) + tasks(gla_values, range(nb)),
    )
    for (lo, hi), work in zip(PROJ_GROUPS, windows):
        starts = list(range(lo, hi, MXU_N))
        for i, c0 in enumerate(starts):
            project(c0, min(c0 + MXU_N, hi))
            for t in work[i * len(work) // len(starts):(i + 1) * len(work) // len(starts)]:
                t()
    for t in tasks(hgrn_values, range(nb)):
        t()


def _proj(x2d, g_mix, w_in_p, w_a2p, b_ap, lb_param, *, chunk, paired):
    n = x2d.shape[0]
    tm = ROW_TILE
    kern = functools.partial(_proj_kernel, chunk=chunk, paired=paired)

    def arr(rows, width, dtype):
        return (jax.ShapeDtypeStruct((n // rows, width), dtype),
                pl.BlockSpec((tm // rows, width), lambda i: (i, 0)))

    if paired:
        keys = [arr(1, D_MODEL, bf16), arr(1, D_MODEL, bf16), arr(2, K_W, bf16),
                arr(1, K_W, bf16), arr(1, K_W, bf16), arr(2 * chunk, K_W, f32)]
        v_out = arr(1, D_MODEL, bf16)
    else:
        keys = [arr(1, D_MODEL, f32), arr(1, K_W, f32), arr(1, K_W, f32), arr(chunk, K_W, f32)]
        v_out = arr(1, D_MODEL, f32)
    outs = keys + [v_out, arr(1, D_MODEL, f32)]

    return pl.pallas_call(
        kern,
        out_shape=tuple(o[0] for o in outs),
        grid=(n // tm,),
        in_specs=[
            pl.BlockSpec((tm, D_MODEL), lambda i: (i, 0)),
            _const_spec((1, D_MODEL)),
            _const_spec((D_MODEL, D_P)),
            _const_spec(w_a2p.shape),
            _const_spec(b_ap.shape),
            _const_spec(lb_param.shape),
        ],
        out_specs=tuple(o[1] for o in outs),
        scratch_shapes=[pltpu.VMEM((tm, hi - lo), f32) for lo, hi in PROJ_GROUPS],
        compiler_params=pltpu.CompilerParams(
            dimension_semantics=("arbitrary",), vmem_limit_bytes=VMEM_LIMIT),
        name="proj",
    )(x2d, g_mix, w_in_p, w_a2p, b_ap, lb_param)


def _head_norm(o, onorm):
    return o * lax.rsqrt(jnp.mean(o * o, axis=-1, keepdims=True) + EPS) * onorm


def _tril(n):
    r = lax.broadcasted_iota(jnp.int32, (n, n), 0)
    c = lax.broadcasted_iota(jnp.int32, (n, n), 1)
    return r >= c


def _mixer_prompt_kernel(q32_ref, q64_ref, kin_ref, kx_ref, ko_ref, d_ref, v_ref, g_ref,
                         ona_ref, onb_ref, o_ref, sa_ref, sb_ref,
                         st_ref, att_ref, up_ref, start_ref, *, tile, chunk):
    j = pl.program_id(1)

    @pl.when(j == 0)
    def _():
        st_ref[...] = jnp.zeros_like(st_ref)

    onorm = [ona_ref[...]] * H_A + [onb_ref[...]] * (N_HEADS - H_A)
    pair = 2 * chunk
    n_pairs = tile // pair
    stack = max(len(heads) for heads in _KEY_SHARING) * chunk
    r = lax.broadcasted_iota(jnp.int32, (stack, pair), 0) % chunk
    c = lax.broadcasted_iota(jnp.int32, (stack, pair), 1)
    tril = (r >= c)[:, :chunk]
    cross = (c < chunk) | (r >= c - chunk)

    for heads in _KEY_SHARING:
        n = len(heads)
        kl = slice(_key_lanes(heads[0]), _key_lanes(heads[0]) + HEAD)
        vl = slice(heads[0] * HEAD, (heads[-1] + 1) * HEAD)
        for s in range(n_pairs):
            ra = slice(s * pair, s * pair + chunk)
            rb = slice(s * pair + chunk, (s + 1) * pair)
            rp = slice(s * pair, (s + 1) * pair)
            qa = [q32_ref[ra, h * HEAD:(h + 1) * HEAD] for h in heads]
            qb = [q32_ref[rb, h * HEAD:(h + 1) * HEAD] for h in heads]
            qa = qa[0] if n == 1 else jnp.concatenate(qa, axis=0)
            qb = qb[0] if n == 1 else jnp.concatenate(qb, axis=0)
            a_keys = kin_ref[s * chunk:(s + 1) * chunk, kl]
            att0 = jnp.where(tril[:n * chunk], _dot_nt(qa, a_keys), 0.0).astype(bf16)
            att1 = jnp.where(cross[:n * chunk], _dot_nt(qb, kx_ref[rp, kl]), 0.0).astype(bf16)
            up = _dot_tn(v_ref[rp, vl], ko_ref[rp, kl])
            att0 = jnp.concatenate([att0, jnp.zeros_like(att0)], axis=1)
            for i, h in enumerate(heads):
                u = h * n_pairs + s
                att_ref[u, :chunk] = att0[i * chunk:(i + 1) * chunk]
                att_ref[u, chunk:] = att1[i * chunk:(i + 1) * chunk]
                up_ref[u] = up[i * HEAD:(i + 1) * HEAD]

    for h in range(N_HEADS):
        kl = slice(_key_lanes(h), _key_lanes(h) + HEAD)
        st = st_ref[h]
        for s in range(n_pairs):
            u = h * n_pairs + s
            start_ref[u] = st.astype(bf16)
            st = d_ref[s:s + 1, kl] * st + up_ref[u]
        st_ref[h] = st

    for h in range(N_HEADS):
        sl = slice(h * HEAD, (h + 1) * HEAD)
        outs = []
        for s in range(n_pairs):
            u = h * n_pairs + s
            rp = slice(s * pair, (s + 1) * pair)
            outs.append(_dot_nt(q64_ref[rp, sl], start_ref[u])
                        + jnp.dot(att_ref[u], v_ref[rp, sl], preferred_element_type=f32))
        o = jnp.concatenate(outs, axis=0)
        o_ref[:, sl] = (_head_norm(o, onorm[h]) * _silu(g_ref[:, sl])).astype(o_ref.dtype)

    @pl.when(j == pl.num_programs(1) - 1)
    def _():
        for h in range(H_A):
            sa_ref[0, h] = st_ref[h].T[_own_keys(h), :]
            sb_ref[0, h] = st_ref[H_A + h].T


def _mixer_prompt(acts, ona, onb, batch, seq):
    tile = ROW_TILE
    nt = seq // tile
    chunk = PROMPT_CHUNK
    units = N_HEADS * (tile // (2 * chunk))
    kern = functools.partial(_mixer_prompt_kernel, tile=tile, chunk=chunk)

    def rows(width, per=1):
        return pl.BlockSpec((tile // per, width), lambda b, j: (b * nt + j, 0))

    return pl.pallas_call(
        kern,
        out_shape=(
            jax.ShapeDtypeStruct((batch * seq, D_MODEL), bf16),
            jax.ShapeDtypeStruct((batch, H_A, DK_A, HEAD), f32),
            jax.ShapeDtypeStruct((batch, H_A, HEAD, HEAD), f32),
        ),
        grid=(batch, nt),
        in_specs=[
            rows(D_MODEL), rows(D_MODEL), rows(K_W, 2), rows(K_W), rows(K_W),
            rows(K_W, 2 * chunk), rows(D_MODEL), rows(D_MODEL),
            _const_spec(ona.shape),
            _const_spec(onb.shape),
        ],
        out_specs=(
            rows(D_MODEL),
            pl.BlockSpec((1, H_A, DK_A, HEAD), lambda b, j: (b, 0, 0, 0)),
            pl.BlockSpec((1, H_A, HEAD, HEAD), lambda b, j: (b, 0, 0, 0)),
        ),
        scratch_shapes=[
            pltpu.VMEM((N_HEADS, HEAD, HEAD), f32),
            pltpu.VMEM((units, 2 * chunk, 2 * chunk), bf16),
            pltpu.VMEM((units, HEAD, HEAD), f32),
            pltpu.VMEM((units, HEAD, HEAD), bf16),
        ],
        compiler_params=pltpu.CompilerParams(
            dimension_semantics=("arbitrary", "arbitrary"), vmem_limit_bytes=VMEM_LIMIT),
        name="mixer_prompt",
    )(*acts, ona, onb)


def _mixer_sample_kernel(qi_ref, ki_ref, ko_ref, v_ref, g_ref, d_ref, sa_in_ref, sb_in_ref,
                         ona_ref, onb_ref, o_ref, sa_ref, sb_ref, att_ref, oi_ref, *, bt, seq):
    onorm = [ona_ref[...]] * H_A + [onb_ref[...]] * (N_HEADS - H_A)
    tril = _tril(seq)
    zpad = jnp.zeros((HEAD - DK_A, HEAD), f32)
    zeros = jnp.zeros((seq, HEAD), f32)
    row = lax.broadcasted_iota(jnp.int32, (seq, HEAD), 0)
    ones_rows = jnp.where(row < 2, 1.0, 0.0)
    rhs_bottom = jnp.concatenate([zeros, ones_rows], axis=1)

    for bi in range(bt):
        rows = slice(bi * seq, (bi + 1) * seq)
        decay = d_ref[bi:bi + 1, :]
        d_hi = decay.astype(bf16).astype(f32)
        d_lo = decay - d_hi
        for h in range(N_HEADS):
            sl = slice(h * HEAD, (h + 1) * HEAD)
            kl = slice(_key_lanes(h), _key_lanes(h) + HEAD)
            q_in = qi_ref[rows, sl].astype(bf16)
            k_in = ki_ref[rows, kl].astype(bf16)
            v32 = v_ref[rows, sl]
            if h >= H_A:
                s = sb_in_ref[bi, h - H_A]
            elif h % 2 == 0:
                s = jnp.concatenate([sa_in_ref[bi, h], zpad], axis=0)
            else:
                s = jnp.concatenate([zpad, sa_in_ref[bi, h]], axis=0)
            att_ref[bi * N_HEADS + h] = jnp.where(tril, _dot_nt(q_in, k_in), 0.0)
            oi_ref[rows, sl] = jnp.dot(q_in, s.astype(bf16), preferred_element_type=f32)
            d_rows = jnp.where(row == 0, d_hi[:, kl], jnp.where(row == 1, d_lo[:, kl], 0.0))
            lhs = jnp.concatenate([ko_ref[rows, kl], d_rows], axis=0).astype(bf16)
            rhs = jnp.concatenate(
                [jnp.concatenate([v32, zeros], axis=1), rhs_bottom], axis=0).astype(bf16)
            upd = _dot_tn(lhs, rhs)
            s_new = upd[:, HEAD:] * s + upd[:, :HEAD]
            if h < H_A:
                sa_ref[bi, h] = s_new[_own_keys(h), :]
            else:
                sb_ref[bi, h - H_A] = s_new

    for bi in range(bt):
        rows = slice(bi * seq, (bi + 1) * seq)
        for h in range(N_HEADS):
            sl = slice(h * HEAD, (h + 1) * HEAD)
            att = att_ref[bi * N_HEADS + h].astype(bf16)
            o = oi_ref[rows, sl] + jnp.dot(att, v_ref[rows, sl].astype(bf16),
                                           preferred_element_type=f32)
            o_ref[rows, sl] = (_head_norm(o, onorm[h]) * _silu(g_ref[rows, sl])).astype(
                o_ref.dtype)


def _mixer_sample(acts, sa_in, sb_in, ona, onb, batch, seq):
    bt = SAMPLE_SEQS
    kern = functools.partial(_mixer_sample_kernel, bt=bt, seq=seq)

    def rows(width):
        return pl.BlockSpec((bt * seq, width), lambda i: (i, 0))

    sa_spec = pl.BlockSpec((bt, H_A, DK_A, HEAD), lambda i: (i, 0, 0, 0))
    sb_spec = pl.BlockSpec((bt, H_A, HEAD, HEAD), lambda i: (i, 0, 0, 0))
    return pl.pallas_call(
        kern,
        out_shape=(
            jax.ShapeDtypeStruct((batch * seq, D_MODEL), f32),
            jax.ShapeDtypeStruct((batch, H_A, DK_A, HEAD), f32),
            jax.ShapeDtypeStruct((batch, H_A, HEAD, HEAD), f32),
        ),
        grid=(batch // bt,),
        in_specs=[
            rows(D_MODEL), rows(K_W), rows(K_W), rows(D_MODEL), rows(D_MODEL),
            pl.BlockSpec((bt, K_W), lambda i: (i, 0)),
            sa_spec, sb_spec,
            _const_spec(ona.shape),
            _const_spec(onb.shape),
        ],
        out_specs=(rows(D_MODEL), sa_spec, sb_spec),
        scratch_shapes=[
            pltpu.VMEM((bt * N_HEADS, seq, seq), f32),
            pltpu.VMEM((bt * seq, D_MODEL), f32),
        ],
        compiler_params=pltpu.CompilerParams(
            dimension_semantics=("arbitrary",), vmem_limit_bytes=VMEM_LIMIT),
        name="mixer_sample",
    )(*acts, sa_in, sb_in, ona, onb)


def _out_kernel(o_ref, x_ref, w_o_ref, gf_ref, w1_ref, w3_ref, w2_ref, gl_ref, y_ref):
    x1 = x_ref[...] + jnp.dot(o_ref[...].astype(bf16), w_o_ref[...], preferred_element_type=f32)
    h = _rmsnorm(x1, gf_ref[...]).astype(bf16)
    acc = x1
    for c in range(D_FF // FF_CHUNK):
        cols = slice(c * FF_CHUNK, (c + 1) * FF_CHUNK)
        a = jnp.dot(h, w1_ref[:, cols], preferred_element_type=f32)
        b = jnp.dot(h, w3_ref[:, cols], preferred_element_type=f32)
        u = (_silu(a) * b).astype(bf16)
        acc = acc + jnp.dot(u, w2_ref[cols, :], preferred_element_type=f32)
    y_ref[...] = _rmsnorm(acc, gl_ref[...])


def _out(o2d, x2d, w_o, g_ffn, w1, w3, w2, g_final):
    n = x2d.shape[0]
    row_spec = pl.BlockSpec((ROW_TILE, D_MODEL), lambda i: (i, 0))
    return pl.pallas_call(
        _out_kernel,
        out_shape=jax.ShapeDtypeStruct((n, D_MODEL), f32),
        grid=(n // ROW_TILE,),
        in_specs=[
            row_spec, row_spec,
            _const_spec((D_MODEL, D_MODEL)),
            _const_spec((1, D_MODEL)),
            _const_spec((D_MODEL, D_FF)),
            _const_spec((D_MODEL, D_FF)),
            _const_spec((D_FF, D_MODEL)),
            _const_spec((1, D_MODEL)),
        ],
        out_specs=row_spec,
        compiler_params=pltpu.CompilerParams(
            dimension_semantics=("arbitrary",), vmem_limit_bytes=VMEM_LIMIT),
        name="out_ffn",
    )(o2d, x2d, w_o, g_ffn, w1, w3, w2, g_final)


def _w_in_sections():
    sections, src = [], 0
    for width, dst in ((KA_W, OFF_QA), (KA_W, OFF_KA), (SEC, OFF_VA), (SEC, OFF_GA),
                       (GATE_RANK, OFF_LR), (SEC, OFF_QB), (SEC, OFF_FB), (SEC, OFF_IB),
                       (SEC, OFF_GB)):
        sections.append((src, width, dst))
        src += width
    return tuple(sections), src


_W_IN_SECTIONS, D_IN = _w_in_sections()


def _w_in_layout_kernel(wt_ref, o_ref):
    for src, width, dst in _W_IN_SECTIONS:
        part = wt_ref[src:src + width, :]
        if width % HEAD:
            pad = jnp.zeros((HEAD - width % HEAD, part.shape[1]), part.dtype)
            part = jnp.concatenate([part, pad], axis=0)
        o_ref[:, dst:dst + part.shape[0]] = part.T.astype(o_ref.dtype)


def _layout_w_in(w_in_t, cols=256):
    return pl.pallas_call(
        _w_in_layout_kernel,
        out_shape=jax.ShapeDtypeStruct((D_MODEL, D_P), bf16),
        grid=(D_MODEL // cols,),
        in_specs=[pl.BlockSpec((D_IN, cols), lambda i: (0, i))],
        out_specs=pl.BlockSpec((cols, D_P), lambda i: (i, 0)),
        compiler_params=pltpu.CompilerParams(
            dimension_semantics=("arbitrary",), vmem_limit_bytes=VMEM_LIMIT),
        name="w_in_layout",
    )(w_in_t)


def kernel(x_prompt, x_sample, state_gla, state_hgrn, norm_mix, w_in, w_a2, b_a, lb_param,
           onorm_a, onorm_b, w_o, norm_ffn, w1, w3, w2, norm_final):
    batch, seq, _ = x_prompt.shape
    dec_batch, dec_seq, _ = x_sample.shape

    w_in_p = _layout_w_in(jnp.swapaxes(w_in[0], 0, 1))
    w_a2p = jnp.pad(w_a2[0], ((0, HEAD - GATE_RANK), (0, 0))).astype(bf16)
    b_ap = b_a[0].reshape(1, KA_W)
    g_mix = norm_mix[0].reshape(1, D_MODEL)
    g_ffn = norm_ffn[0].reshape(1, D_MODEL)
    g_fin = norm_final.reshape(1, D_MODEL)
    ona = onorm_a[0].reshape(1, HEAD)
    onb = onorm_b[0].reshape(1, HEAD)
    w_o_b = w_o[0].astype(bf16)
    w1_b = w1[0].astype(bf16)
    w3_b = w3[0].astype(bf16)
    w2_b = w2[0].astype(bf16)

    xp = x_prompt.reshape(batch * seq, D_MODEL)
    xs = x_sample.reshape(dec_batch * dec_seq, D_MODEL)

    acts_p = _proj(xp, g_mix, w_in_p, w_a2p, b_ap, lb_param, chunk=PROMPT_CHUNK, paired=True)
    op, sa_p, sb_p = _mixer_prompt(acts_p, ona, onb, batch, seq)
    yp = _out(op, xp, w_o_b, g_ffn, w1_b, w3_b, w2_b, g_fin)

    q_s, kin_s, ko_s, d_s, v_s, g_s = _proj(xs, g_mix, w_in_p, w_a2p, b_ap, lb_param,
                                            chunk=dec_seq, paired=False)
    os_, sa_s, sb_s = _mixer_sample((q_s, kin_s, ko_s, v_s, g_s, d_s), state_gla[0],
                                    state_hgrn[0], ona, onb, dec_batch, dec_seq)
    ys = _out(os_, xs, w_o_b, g_ffn, w1_b, w3_b, w2_b, g_fin)

    return (yp.reshape(batch, seq, D_MODEL), ys.reshape(dec_batch, dec_seq, D_MODEL),
            sa_p[None], sb_p[None], sa_s[None], sb_s[None])
```

```python
import functools

import jax
import jax.numpy as jnp
from jax import lax
from jax.experimental import pallas as pl
from jax.experimental.pallas import tpu as pltpu

D_MODEL = 1024
N_HEADS = 8
H_A = 4
DK_A = 64
HEAD = 128
GATE_RANK = 16
GATE_NORM = 16.0
D_FF = 2816
FF_CHUNK = 256
EPS = 1e-6
PROMPT_CHUNK = 32
SUBLANES = 8

KA_W = H_A * DK_A
SEC = H_A * HEAD
K_W = KA_W + SEC
OFF_LR = 0
OFF_QA = HEAD
OFF_KA = OFF_QA + KA_W
OFF_VA = OFF_KA + KA_W
OFF_GA = OFF_VA + SEC
OFF_FB = OFF_GA + SEC
OFF_QB = OFF_FB + SEC
OFF_IB = OFF_QB + SEC
OFF_GB = OFF_IB + SEC
D_P = OFF_GB + SEC
MXU_N = 256
PROJ_GROUPS = ((OFF_LR, OFF_VA), (OFF_FB, OFF_IB), (OFF_VA, OFF_FB), (OFF_IB, D_P))

ROW_TILE = 512
SAMPLE_SEQS = 16
VMEM_LIMIT = 56 * 1024 * 1024

f32 = jnp.float32
bf16 = jnp.bfloat16


def _const_spec(shape):
    nd = len(shape)
    return pl.BlockSpec(shape, lambda *_: (0,) * nd, pipeline_mode=pl.Buffered(1))


def _rmsnorm(x, g):
    return x * lax.rsqrt(jnp.mean(x * x, axis=-1, keepdims=True) + EPS) * g


def _sigmoid(x):
    return 0.5 * jnp.tanh(0.5 * x) + 0.5


def _silu(x):
    hx = 0.5 * x
    return hx + hx * jnp.tanh(hx)


def _dot_nt(a, b):
    return lax.dot_general(a, b, (((1,), (1,)), ((), ())), preferred_element_type=f32)


def _dot_tn(a, b):
    return lax.dot_general(a, b, (((0,), (0,)), ((), ())), preferred_element_type=f32)


def _key_lanes(h):
    return HEAD * (h // 2) if h < H_A else KA_W + HEAD * (h - H_A)


_KEY_SHARING = tuple((h, h + 1) for h in range(0, H_A, 2)) + tuple(
    (h,) for h in range(H_A, N_HEADS))


def _own_keys(h):
    return slice(DK_A * (h % 2), DK_A * (h % 2 + 1)) if h < H_A else slice(0, HEAD)


def _cumsum_rows(x, chunk):
    rows, width = x.shape
    row = lax.broadcasted_iota(jnp.int32, (SUBLANES, width), 0)
    groups_per_chunk = chunk // SUBLANES
    out = []
    carry = None
    for g in range(rows // SUBLANES):
        y = x[g * SUBLANES:(g + 1) * SUBLANES]
        for s in (1, 2, 4):
            y = y + jnp.where(row >= s, pltpu.roll(y, s, 0), 0.0)
        if g % groups_per_chunk != 0:
            y = y + carry
        carry = y[SUBLANES - 1:SUBLANES]
        out.append(y)
    return out[0] if len(out) == 1 else jnp.concatenate(out, axis=0)


def _lower_bound(lbp_ref):
    lbp = lbp_ref[...]
    m = jnp.max(lbp, axis=0, keepdims=True)
    e = jnp.exp(lbp - m)
    return e[0:1] / jnp.sum(e, axis=0, keepdims=True)


def _decays(la, chunk):
    rows = la.shape[0]
    b = _cumsum_rows(la, chunk)
    n_chunks = rows // chunk
    lasts = [b[(c + 1) * chunk - 1:(c + 1) * chunk] for c in range(n_chunks)]
    if n_chunks == 1:
        b_last = lasts[0]
        b_last_rows = b_last
    else:
        b_last = jnp.concatenate(lasts, axis=0)
        b_last_rows = jnp.concatenate(
            [jnp.broadcast_to(l, (chunk, l.shape[1])) for l in lasts], axis=0)
    return jnp.exp(b), jnp.exp(-b), jnp.exp(b_last_rows - b), jnp.exp(b_last)


def _emit_keys_paired(refs, r, q_targets, kl, q, k, la):
    q32_ref, q64_ref, kin_ref, kx_ref, ko_ref, d_ref = refs
    c = PROMPT_CHUNK
    rows = slice(2 * c * r, 2 * c * (r + 1))
    e_b, e_nb, e_out, decay = _decays(la, c)
    d0, d1 = decay[0:1], decay[1:2]
    q32 = q * e_b
    q64 = jnp.concatenate([q32[:c], q32[c:] * d0], axis=0)
    k_in = k * e_nb
    k_out = k * e_out
    for ql, own in q_targets:
        q32_ref[rows, ql] = (q32 if own is None else jnp.where(own, q32, 0.0)).astype(bf16)
        q64_ref[rows, ql] = (q64 if own is None else jnp.where(own, q64, 0.0)).astype(bf16)
    kin_ref[c * r:c * (r + 1), kl] = k_in[:c].astype(bf16)
    kx_ref[rows, kl] = jnp.concatenate([k_out[:c], k_in[c:]], axis=0).astype(bf16)
    ko_ref[rows, kl] = jnp.concatenate([k_out[:c] * d1, k_out[c:]], axis=0).astype(bf16)
    d_ref[r:r + 1, kl] = d0 * d1


def _emit_keys_single(refs, r, q_targets, kl, q, k, la, *, chunk):
    q_ref, kin_ref, ko_ref, d_ref = refs
    rows = slice(SUBLANES * chunk * r, SUBLANES * chunk * (r + 1))
    e_b, e_nb, e_out, decay = _decays(la, chunk)
    q_in = q * e_b
    for ql, own in q_targets:
        q_ref[rows, ql] = q_in if own is None else jnp.where(own, q_in, 0.0)
    kin_ref[rows, kl] = k * e_nb
    ko_ref[rows, kl] = k * e_out
    d_ref[SUBLANES * r:SUBLANES * (r + 1), kl] = decay


def _proj_kernel(x_ref, gm_ref, w_ref, w_a2_ref, b_a_ref, lbp_ref, *refs, chunk, paired, n_cast):
    cast_in, refs = refs[:n_cast], refs[n_cast:]
    n_key_refs = 6 if paired else 4
    key_refs = refs[:n_key_refs]
    v_ref, g_ref = refs[n_key_refs:n_key_refs + 2]
    cast_out = refs[n_key_refs + 2:n_key_refs + 2 + n_cast]
    p_refs = refs[n_key_refs + 2 + n_cast:]
    tm = x_ref.shape[0]
    block = 2 * chunk if paired else SUBLANES * chunk
    emit_keys = (_emit_keys_paired if paired
                 else functools.partial(_emit_keys_single, chunk=chunk))
    blocks = [(slice(r * block, (r + 1) * block), r) for r in range(tm // block)]
    h = _rmsnorm(x_ref[...], gm_ref[...]).astype(bf16)

    w_a2 = w_a2_ref[...]
    b_a = b_a_ref[...]
    lb = _lower_bound(lbp_ref)
    lane = lax.broadcasted_iota(jnp.int32, (block, HEAD), 1)

    def p_slot(lo, hi):
        for (g_lo, g_hi), ref in zip(PROJ_GROUPS, p_refs):
            if g_lo <= lo and hi <= g_hi:
                return ref, slice(lo - g_lo, hi - g_lo)
        raise ValueError((lo, hi))

    def project(lo, hi):
        ref, cols = p_slot(lo, hi)
        ref[:, cols] = jnp.dot(h, w_ref[:, lo:hi], preferred_element_type=f32)

    def p(rows, lo, width):
        ref, cols = p_slot(lo, lo + width)
        return ref[rows, cols]

    def gla_keys(rows, r):
        lra = p(rows, OFF_LR, HEAD)
        z = jnp.dot(lra.astype(bf16), w_a2, preferred_element_type=f32) + b_a
        for pr in range(H_A // 2):
            kl = slice(pr * HEAD, (pr + 1) * HEAD)
            zz = z[:, kl]
            la = (jnp.minimum(zz, 0.0) - jnp.log(1.0 + jnp.exp(-jnp.abs(zz)))) * (1.0 / GATE_NORM)
            q = p(rows, OFF_QA + pr * HEAD, HEAD) * (DK_A ** -0.5)
            k = p(rows, OFF_KA + pr * HEAD, HEAD)
            q_targets = [(slice(hd * HEAD, (hd + 1) * HEAD), own)
                         for hd, own in ((2 * pr, lane < DK_A), (2 * pr + 1, lane >= DK_A))]
            emit_keys(key_refs, r, q_targets, kl, q, k, la)

    def hgrn_keys(rows, r):
        for hd in range(N_HEADS - H_A):
            lbh = lb[:, hd * HEAD:(hd + 1) * HEAD]
            f = lbh + (1.0 - lbh) * _sigmoid(p(rows, OFF_FB + hd * HEAD, HEAD))
            q = _silu(p(rows, OFF_QB + hd * HEAD, HEAD))
            ql = slice(SEC + hd * HEAD, SEC + (hd + 1) * HEAD)
            kl = slice(KA_W + hd * HEAD, KA_W + (hd + 1) * HEAD)
            emit_keys(key_refs, r, [(ql, None)], kl, q, 1.0 - f, jnp.log(f))

    def gla_values(rows, crow):
        v_ref[rows, 0:SEC] = p(rows, OFF_VA, SEC).astype(v_ref.dtype)
        g_ref[rows, 0:SEC] = p(rows, OFF_GA, SEC)

    def hgrn_values(rows, crow):
        v_ref[rows, SEC:2 * SEC] = p(rows, OFF_IB, SEC).astype(v_ref.dtype)
        g_ref[rows, SEC:2 * SEC] = p(rows, OFF_GB, SEC)

    def tasks(fn, which):
        return [functools.partial(fn, *blocks[r]) for r in which]

    def cast(src, dst):
        dst[...] = src[...].astype(dst.dtype)

    nb = len(blocks)
    first, second = range(nb // 2), range(nb // 2, nb)
    windows = (
        [functools.partial(cast, s, d) for s, d in zip(cast_in, cast_out)],
        tasks(gla_keys, range(nb)),
        tasks(hgrn_keys, first),
        tasks(hgrn_keys, second) + tasks(gla_values, range(nb)),
    )
    for (lo, hi), work in zip(PROJ_GROUPS, windows):
        starts = list(range(lo, hi, MXU_N))
        for i, c0 in enumerate(starts):
            project(c0, min(c0 + MXU_N, hi))
            for t in work[i * len(work) // len(starts):(i + 1) * len(work) // len(starts)]:
                t()
    for t in tasks(hgrn_values, range(nb)):
        t()


def _proj(x2d, g_mix, w_in_p, w_a2p, b_ap, lb_param, *, chunk, paired, cast=()):
    n = x2d.shape[0]
    tm = ROW_TILE
    steps = n // tm
    kern = functools.partial(_proj_kernel, chunk=chunk, paired=paired, n_cast=len(cast))

    def arr(rows, width, dtype):
        return (jax.ShapeDtypeStruct((n // rows, width), dtype),
                pl.BlockSpec((tm // rows, width), lambda i: (i, 0)))

    if paired:
        keys = [arr(1, D_MODEL, bf16), arr(1, D_MODEL, bf16), arr(2, K_W, bf16),
                arr(1, K_W, bf16), arr(1, K_W, bf16), arr(2 * chunk, K_W, f32)]
        v_out = arr(1, D_MODEL, bf16)
    else:
        keys = [arr(1, D_MODEL, f32), arr(1, K_W, f32), arr(1, K_W, f32), arr(chunk, K_W, f32)]
        v_out = arr(1, D_MODEL, f32)
    outs = keys + [v_out, arr(1, D_MODEL, f32)]
    cast_specs = [pl.BlockSpec((w.shape[0] // steps, w.shape[1]), lambda i: (i, 0)) for w in cast]
    outs += [(jax.ShapeDtypeStruct(w.shape, bf16), spec) for w, spec in zip(cast, cast_specs)]

    return pl.pallas_call(
        kern,
        out_shape=tuple(o[0] for o in outs),
        grid=(steps,),
        in_specs=[
            pl.BlockSpec((tm, D_MODEL), lambda i: (i, 0)),
            _const_spec((1, D_MODEL)),
            _const_spec((D_MODEL, D_P)),
            _const_spec(w_a2p.shape),
            _const_spec(b_ap.shape),
            _const_spec(lb_param.shape),
        ] + cast_specs,
        out_specs=tuple(o[1] for o in outs),
        scratch_shapes=[pltpu.VMEM((tm, hi - lo), f32) for lo, hi in PROJ_GROUPS],
        compiler_params=pltpu.CompilerParams(
            dimension_semantics=("arbitrary",), vmem_limit_bytes=VMEM_LIMIT),
        name="proj",
    )(x2d, g_mix, w_in_p, w_a2p, b_ap, lb_param, *cast)


def _head_norm(o, onorm):
    return o * lax.rsqrt(jnp.mean(o * o, axis=-1, keepdims=True) + EPS) * onorm


def _tril(n):
    r = lax.broadcasted_iota(jnp.int32, (n, n), 0)
    c = lax.broadcasted_iota(jnp.int32, (n, n), 1)
    return r >= c


def _mixer_prompt_kernel(q32_ref, q64_ref, kin_ref, kx_ref, ko_ref, d_ref, v_ref, g_ref,
                         ona_ref, onb_ref, o_ref, sa_ref, sb_ref,
                         st_ref, att_ref, up_ref, start_ref, *, tile, chunk):
    j = pl.program_id(1)

    @pl.when(j == 0)
    def _():
        st_ref[...] = jnp.zeros_like(st_ref)

    onorm = [ona_ref[...]] * H_A + [onb_ref[...]] * (N_HEADS - H_A)
    pair = 2 * chunk
    n_pairs = tile // pair
    stack = max(len(heads) for heads in _KEY_SHARING) * chunk
    r = lax.broadcasted_iota(jnp.int32, (stack, pair), 0) % chunk
    c = lax.broadcasted_iota(jnp.int32, (stack, pair), 1)
    tril = (r >= c)[:, :chunk]
    cross = (c < chunk) | (r >= c - chunk)

    for heads in _KEY_SHARING:
        n = len(heads)
        kl = slice(_key_lanes(heads[0]), _key_lanes(heads[0]) + HEAD)
        vl = slice(heads[0] * HEAD, (heads[-1] + 1) * HEAD)
        for s in range(n_pairs):
            ra = slice(s * pair, s * pair + chunk)
            rb = slice(s * pair + chunk, (s + 1) * pair)
            rp = slice(s * pair, (s + 1) * pair)
            qa = [q32_ref[ra, h * HEAD:(h + 1) * HEAD] for h in heads]
            qb = [q32_ref[rb, h * HEAD:(h + 1) * HEAD] for h in heads]
            qa = qa[0] if n == 1 else jnp.concatenate(qa, axis=0)
            qb = qb[0] if n == 1 else jnp.concatenate(qb, axis=0)
            a_keys = kin_ref[s * chunk:(s + 1) * chunk, kl]
            att0 = jnp.where(tril[:n * chunk], _dot_nt(qa, a_keys), 0.0).astype(bf16)
            att1 = jnp.where(cross[:n * chunk], _dot_nt(qb, kx_ref[rp, kl]), 0.0).astype(bf16)
            up = _dot_tn(v_ref[rp, vl], ko_ref[rp, kl])
            att0 = jnp.concatenate([att0, jnp.zeros_like(att0)], axis=1)
            for i, h in enumerate(heads):
                u = h * n_pairs + s
                att_ref[u, :chunk] = att0[i * chunk:(i + 1) * chunk]
                att_ref[u, chunk:] = att1[i * chunk:(i + 1) * chunk]
                up_ref[u] = up[i * HEAD:(i + 1) * HEAD]

    for h in range(N_HEADS):
        kl = slice(_key_lanes(h), _key_lanes(h) + HEAD)
        st = st_ref[h]
        for s in range(n_pairs):
            u = h * n_pairs + s
            start_ref[u] = st.astype(bf16)
            st = d_ref[s:s + 1, kl] * st + up_ref[u]
        st_ref[h] = st

    for h in range(N_HEADS):
        sl = slice(h * HEAD, (h + 1) * HEAD)
        outs = []
        for s in range(n_pairs):
            u = h * n_pairs + s
            rp = slice(s * pair, (s + 1) * pair)
            outs.append(_dot_nt(q64_ref[rp, sl], start_ref[u])
                        + jnp.dot(att_ref[u], v_ref[rp, sl], preferred_element_type=f32))
        o = jnp.concatenate(outs, axis=0)
        o_ref[:, sl] = (_head_norm(o, onorm[h]) * _silu(g_ref[:, sl])).astype(o_ref.dtype)

    @pl.when(j == pl.num_programs(1) - 1)
    def _():
        for h in range(H_A):
            sa_ref[0, h] = st_ref[h].T[_own_keys(h), :]
            sb_ref[0, h] = st_ref[H_A + h].T


def _mixer_prompt(acts, ona, onb, batch, seq):
    tile = ROW_TILE
    nt = seq // tile
    chunk = PROMPT_CHUNK
    units = N_HEADS * (tile // (2 * chunk))
    kern = functools.partial(_mixer_prompt_kernel, tile=tile, chunk=chunk)

    def rows(width, per=1):
        return pl.BlockSpec((tile // per, width), lambda b, j: (b * nt + j, 0))

    return pl.pallas_call(
        kern,
        out_shape=(
            jax.ShapeDtypeStruct((batch * seq, D_MODEL), bf16),
            jax.ShapeDtypeStruct((batch, H_A, DK_A, HEAD), f32),
            jax.ShapeDtypeStruct((batch, H_A, HEAD, HEAD), f32),
        ),
        grid=(batch, nt),
        in_specs=[
            rows(D_MODEL), rows(D_MODEL), rows(K_W, 2), rows(K_W), rows(K_W),
            rows(K_W, 2 * chunk), rows(D_MODEL), rows(D_MODEL),
            _const_spec(ona.shape),
            _const_spec(onb.shape),
        ],
        out_specs=(
            rows(D_MODEL),
            pl.BlockSpec((1, H_A, DK_A, HEAD), lambda b, j: (b, 0, 0, 0)),
            pl.BlockSpec((1, H_A, HEAD, HEAD), lambda b, j: (b, 0, 0, 0)),
        ),
        scratch_shapes=[
            pltpu.VMEM((N_HEADS, HEAD, HEAD), f32),
            pltpu.VMEM((units, 2 * chunk, 2 * chunk), bf16),
            pltpu.VMEM((units, HEAD, HEAD), f32),
            pltpu.VMEM((units, HEAD, HEAD), bf16),
        ],
        compiler_params=pltpu.CompilerParams(
            dimension_semantics=("arbitrary", "arbitrary"), vmem_limit_bytes=VMEM_LIMIT),
        name="mixer_prompt",
    )(*acts, ona, onb)


def _mixer_sample_kernel(qi_ref, ki_ref, ko_ref, v_ref, g_ref, d_ref, sa_in_ref, sb_in_ref,
                         ona_ref, onb_ref, o_ref, sa_ref, sb_ref, att_ref, oi_ref, *, bt, seq):
    onorm = [ona_ref[...]] * H_A + [onb_ref[...]] * (N_HEADS - H_A)
    tril = _tril(seq)
    zpad = jnp.zeros((HEAD - DK_A, HEAD), f32)
    zeros = jnp.zeros((seq, HEAD), f32)
    row = lax.broadcasted_iota(jnp.int32, (seq, HEAD), 0)
    ones_rows = jnp.where(row < 2, 1.0, 0.0)
    rhs_bottom = jnp.concatenate([zeros, ones_rows], axis=1)

    for bi in range(bt):
        rows = slice(bi * seq, (bi + 1) * seq)
        decay = d_ref[bi:bi + 1, :]
        d_hi = decay.astype(bf16).astype(f32)
        d_lo = decay - d_hi
        for h in range(N_HEADS):
            sl = slice(h * HEAD, (h + 1) * HEAD)
            kl = slice(_key_lanes(h), _key_lanes(h) + HEAD)
            q_in = qi_ref[rows, sl].astype(bf16)
            k_in = ki_ref[rows, kl].astype(bf16)
            v32 = v_ref[rows, sl]
            if h >= H_A:
                s = sb_in_ref[bi, h - H_A]
            elif h % 2 == 0:
                s = jnp.concatenate([sa_in_ref[bi, h], zpad], axis=0)
            else:
                s = jnp.concatenate([zpad, sa_in_ref[bi, h]], axis=0)
            att_ref[bi * N_HEADS + h] = jnp.where(tril, _dot_nt(q_in, k_in), 0.0)
            oi_ref[rows, sl] = jnp.dot(q_in, s.astype(bf16), preferred_element_type=f32)
            d_rows = jnp.where(row == 0, d_hi[:, kl], jnp.where(row == 1, d_lo[:, kl], 0.0))
            lhs = jnp.concatenate([ko_ref[rows, kl], d_rows], axis=0).astype(bf16)
            rhs = jnp.concatenate(
                [jnp.concatenate([v32, zeros], axis=1), rhs_bottom], axis=0).astype(bf16)
            upd = _dot_tn(lhs, rhs)
            s_new = upd[:, HEAD:] * s + upd[:, :HEAD]
            if h < H_A:
                sa_ref[bi, h] = s_new[_own_keys(h), :]
            else:
                sb_ref[bi, h - H_A] = s_new

    for bi in range(bt):
        rows = slice(bi * seq, (bi + 1) * seq)
        for h in range(N_HEADS):
            sl = slice(h * HEAD, (h + 1) * HEAD)
            att = att_ref[bi * N_HEADS + h].astype(bf16)
            o = oi_ref[rows, sl] + jnp.dot(att, v_ref[rows, sl].astype(bf16),
                                           preferred_element_type=f32)
            o_ref[rows, sl] = (_head_norm(o, onorm[h]) * _silu(g_ref[rows, sl])).astype(
                o_ref.dtype)


def _mixer_sample(acts, sa_in, sb_in, ona, onb, batch, seq):
    bt = SAMPLE_SEQS
    kern = functools.partial(_mixer_sample_kernel, bt=bt, seq=seq)

    def rows(width):
        return pl.BlockSpec((bt * seq, width), lambda i: (i, 0))

    sa_spec = pl.BlockSpec((bt, H_A, DK_A, HEAD), lambda i: (i, 0, 0, 0))
    sb_spec = pl.BlockSpec((bt, H_A, HEAD, HEAD), lambda i: (i, 0, 0, 0))
    return pl.pallas_call(
        kern,
        out_shape=(
            jax.ShapeDtypeStruct((batch * seq, D_MODEL), f32),
            jax.ShapeDtypeStruct((batch, H_A, DK_A, HEAD), f32),
            jax.ShapeDtypeStruct((batch, H_A, HEAD, HEAD), f32),
        ),
        grid=(batch // bt,),
        in_specs=[
            rows(D_MODEL), rows(K_W), rows(K_W), rows(D_MODEL), rows(D_MODEL),
            pl.BlockSpec((bt, K_W), lambda i: (i, 0)),
            sa_spec, sb_spec,
            _const_spec(ona.shape),
            _const_spec(onb.shape),
        ],
        out_specs=(rows(D_MODEL), sa_spec, sb_spec),
        scratch_shapes=[
            pltpu.VMEM((bt * N_HEADS, seq, seq), f32),
            pltpu.VMEM((bt * seq, D_MODEL), f32),
        ],
        compiler_params=pltpu.CompilerParams(
            dimension_semantics=("arbitrary",), vmem_limit_bytes=VMEM_LIMIT),
        name="mixer_sample",
    )(*acts, sa_in, sb_in, ona, onb)


def _out_kernel(o_ref, x_ref, w_o_ref, gf_ref, w1_ref, w3_ref, w2_ref, gl_ref, y_ref):
    x1 = x_ref[...] + jnp.dot(o_ref[...].astype(bf16), w_o_ref[...], preferred_element_type=f32)
    h = _rmsnorm(x1, gf_ref[...]).astype(bf16)
    acc = x1
    for c in range(D_FF // FF_CHUNK):
        cols = slice(c * FF_CHUNK, (c + 1) * FF_CHUNK)
        a = jnp.dot(h, w1_ref[:, cols], preferred_element_type=f32)
        b = jnp.dot(h, w3_ref[:, cols], preferred_element_type=f32)
        u = (_silu(a) * b).astype(bf16)
        acc = acc + jnp.dot(u, w2_ref[cols, :], preferred_element_type=f32)
    y_ref[...] = _rmsnorm(acc, gl_ref[...])


def _out(o2d, x2d, w_o, g_ffn, w1, w3, w2, g_final):
    n = x2d.shape[0]
    row_spec = pl.BlockSpec((ROW_TILE, D_MODEL), lambda i: (i, 0))
    return pl.pallas_call(
        _out_kernel,
        out_shape=jax.ShapeDtypeStruct((n, D_MODEL), f32),
        grid=(n // ROW_TILE,),
        in_specs=[
            row_spec, row_spec,
            _const_spec((D_MODEL, D_MODEL)),
            _const_spec((1, D_MODEL)),
            _const_spec((D_MODEL, D_FF)),
            _const_spec((D_MODEL, D_FF)),
            _const_spec((D_FF, D_MODEL)),
            _const_spec((1, D_MODEL)),
        ],
        out_specs=row_spec,
        compiler_params=pltpu.CompilerParams(
            dimension_semantics=("arbitrary",), vmem_limit_bytes=VMEM_LIMIT),
        name="out_ffn",
    )(o2d, x2d, w_o, g_ffn, w1, w3, w2, g_final)


def _w_in_sections():
    sections, src = [], 0
    for width, dst in ((KA_W, OFF_QA), (KA_W, OFF_KA), (SEC, OFF_VA), (SEC, OFF_GA),
                       (GATE_RANK, OFF_LR), (SEC, OFF_QB), (SEC, OFF_FB), (SEC, OFF_IB),
                       (SEC, OFF_GB)):
        sections.append((src, width, dst))
        src += width
    return tuple(sections), src


_W_IN_SECTIONS, D_IN = _w_in_sections()


def _w_in_layout_kernel(wt_ref, o_ref):
    for src, width, dst in _W_IN_SECTIONS:
        part = wt_ref[src:src + width, :]
        if width % HEAD:
            pad = jnp.zeros((HEAD - width % HEAD, part.shape[1]), part.dtype)
            part = jnp.concatenate([part, pad], axis=0)
        o_ref[:, dst:dst + part.shape[0]] = part.T.astype(o_ref.dtype)


def _layout_w_in(w_in_t, cols=256):
    return pl.pallas_call(
        _w_in_layout_kernel,
        out_shape=jax.ShapeDtypeStruct((D_MODEL, D_P), bf16),
        grid=(D_MODEL // cols,),
        in_specs=[pl.BlockSpec((D_IN, cols), lambda i: (0, i))],
        out_specs=pl.BlockSpec((cols, D_P), lambda i: (i, 0)),
        compiler_params=pltpu.CompilerParams(
            dimension_semantics=("arbitrary",), vmem_limit_bytes=VMEM_LIMIT),
        name="w_in_layout",
    )(w_in_t)


def kernel(x_prompt, x_sample, state_gla, state_hgrn, norm_mix, w_in, w_a2, b_a, lb_param,
           onorm_a, onorm_b, w_o, norm_ffn, w1, w3, w2, norm_final):
    batch, seq, _ = x_prompt.shape
    dec_batch, dec_seq, _ = x_sample.shape

    w_in_p = _layout_w_in(jnp.swapaxes(w_in[0], 0, 1))
    w_a2p = jnp.pad(w_a2[0], ((0, HEAD - GATE_RANK), (0, 0))).astype(bf16)
    b_ap = b_a[0].reshape(1, KA_W)
    g_mix = norm_mix[0].reshape(1, D_MODEL)
    g_ffn = norm_ffn[0].reshape(1, D_MODEL)
    g_fin = norm_final.reshape(1, D_MODEL)
    ona = onorm_a[0].reshape(1, HEAD)
    onb = onorm_b[0].reshape(1, HEAD)

    xp = x_prompt.reshape(batch * seq, D_MODEL)
    xs = x_sample.reshape(dec_batch * dec_seq, D_MODEL)

    later = (w_o[0], w1[0], w3[0], w2[0].reshape(D_MODEL, D_FF))
    *acts_p, w_o_b, w1_b, w3_b, w2_b = _proj(xp, g_mix, w_in_p, w_a2p, b_ap, lb_param,
                                             chunk=PROMPT_CHUNK, paired=True, cast=later)
    w2_b = w2_b.reshape(D_FF, D_MODEL)
    op, sa_p, sb_p = _mixer_prompt(acts_p, ona, onb, batch, seq)
    yp = _out(op, xp, w_o_b, g_ffn, w1_b, w3_b, w2_b, g_fin)

    q_s, kin_s, ko_s, d_s, v_s, g_s = _proj(xs, g_mix, w_in_p, w_a2p, b_ap, lb_param,
                                            chunk=dec_seq, paired=False)
    os_, sa_s, sb_s = _mixer_sample((q_s, kin_s, ko_s, v_s, g_s, d_s), state_gla[0],
                                    state_hgrn[0], ona, onb, dec_batch, dec_seq)
    ys = _out(os_, xs, w_o_b, g_ffn, w1_b, w3_b, w2_b, g_fin)

    return (yp.reshape(batch, seq, D_MODEL), ys.reshape(dec_batch, dec_seq, D_MODEL),
            sa_p[None], sb_p[None], sa_s[None], sb_s[None])
```

```python
import functools

import jax
import jax.numpy as jnp
from jax import lax
from jax.experimental import pallas as pl
from jax.experimental.pallas import tpu as pltpu

D_MODEL = 1024
N_HEADS = 8
H_A = 4
DK_A = 64
HEAD = 128
GATE_RANK = 16
GATE_NORM = 16.0
D_FF = 2816
FF_CHUNK = 256
EPS = 1e-6
PROMPT_CHUNK = 32
SUBLANES = 8
BF16_ROWS = 16

KA_W = H_A * DK_A
SEC = H_A * HEAD
K_W = KA_W + SEC
OFF_LR = 0
OFF_QA = HEAD
OFF_KA = OFF_QA + KA_W
OFF_VA = OFF_KA + KA_W
OFF_GA = OFF_VA + SEC
OFF_FB = OFF_GA + SEC
OFF_QB = OFF_FB + SEC
OFF_IB = OFF_QB + SEC
OFF_GB = OFF_IB + SEC
D_P = OFF_GB + SEC
MXU_N = 256
PROJ_GROUPS = ((OFF_LR, OFF_VA), (OFF_FB, OFF_IB), (OFF_VA, OFF_FB), (OFF_IB, D_P))

ROW_TILE = 512
SAMPLE_SEQS = 16
VMEM_LIMIT = 56 * 1024 * 1024

f32 = jnp.float32
bf16 = jnp.bfloat16


def _const_spec(shape):
    nd = len(shape)
    return pl.BlockSpec(shape, lambda *_: (0,) * nd, pipeline_mode=pl.Buffered(1))


def _rmsnorm(x, g):
    return x * lax.rsqrt(jnp.mean(x * x, axis=-1, keepdims=True) + EPS) * g


def _sigmoid(x):
    return 0.5 * jnp.tanh(0.5 * x) + 0.5


def _silu(x):
    hx = 0.5 * x
    return hx + hx * jnp.tanh(hx)


def _dot_nt(a, b):
    return lax.dot_general(a, b, (((1,), (1,)), ((), ())), preferred_element_type=f32)


def _dot_tn(a, b):
    return lax.dot_general(a, b, (((0,), (0,)), ((), ())), preferred_element_type=f32)


def _key_lanes(h):
    return HEAD * (h // 2) if h < H_A else KA_W + HEAD * (h - H_A)


_KEY_SHARING = tuple((h, h + 1) for h in range(0, H_A, 2)) + tuple(
    (h,) for h in range(H_A, N_HEADS))


def _own_keys(h):
    return slice(DK_A * (h % 2), DK_A * (h % 2 + 1)) if h < H_A else slice(0, HEAD)


def _cumsum_rows(x, chunk):
    rows, width = x.shape
    row = lax.broadcasted_iota(jnp.int32, (SUBLANES, width), 0)
    groups_per_chunk = chunk // SUBLANES
    out = []
    carry = None
    for g in range(rows // SUBLANES):
        y = x[g * SUBLANES:(g + 1) * SUBLANES]
        for s in (1, 2, 4):
            y = y + jnp.where(row >= s, pltpu.roll(y, s, 0), 0.0)
        if g % groups_per_chunk != 0:
            y = y + carry
        carry = y[SUBLANES - 1:SUBLANES]
        out.append(y)
    return out[0] if len(out) == 1 else jnp.concatenate(out, axis=0)


def _lower_bound(lbp_ref):
    lbp = lbp_ref[...]
    m = jnp.max(lbp, axis=0, keepdims=True)
    e = jnp.exp(lbp - m)
    return e[0:1] / jnp.sum(e, axis=0, keepdims=True)


def _decays(la, chunk):
    rows = la.shape[0]
    b = _cumsum_rows(la, chunk)
    n_chunks = rows // chunk
    lasts = [b[(c + 1) * chunk - 1:(c + 1) * chunk] for c in range(n_chunks)]
    if n_chunks == 1:
        b_last = lasts[0]
        b_last_rows = b_last
    else:
        b_last = jnp.concatenate(lasts, axis=0)
        b_last_rows = jnp.concatenate(
            [jnp.broadcast_to(l, (chunk, l.shape[1])) for l in lasts], axis=0)
    return jnp.exp(b), jnp.exp(-b), jnp.exp(b_last_rows - b), jnp.exp(b_last)


def _emit_keys_paired(refs, r, q_targets, kl, q, k, la):
    q32_ref, q64_ref, kin_ref, kx_ref, ko_ref, d_ref = refs
    c = PROMPT_CHUNK
    rows = slice(2 * c * r, 2 * c * (r + 1))
    e_b, e_nb, e_out, decay = _decays(la, c)
    d0, d1 = decay[0:1], decay[1:2]
    q32 = q * e_b
    q64 = jnp.concatenate([q32[:c], q32[c:] * d0], axis=0)
    k_in = k * e_nb
    k_out = k * e_out
    for ql, own in q_targets:
        q32_ref[rows, ql] = (q32 if own is None else jnp.where(own, q32, 0.0)).astype(bf16)
        q64_ref[rows, ql] = (q64 if own is None else jnp.where(own, q64, 0.0)).astype(bf16)
    kin_ref[c * r:c * (r + 1), kl] = k_in[:c].astype(bf16)
    kx_ref[rows, kl] = jnp.concatenate([k_out[:c], k_in[c:]], axis=0).astype(bf16)
    ko_ref[rows, kl] = jnp.concatenate([k_out[:c] * d1, k_out[c:]], axis=0).astype(bf16)
    d_ref[r:r + 1, kl] = d0 * d1


def _emit_keys_single(refs, r, q_targets, kl, q, k, la, *, chunk):
    q_ref, kin_ref, ko_ref, d_ref = refs
    rows = slice(SUBLANES * chunk * r, SUBLANES * chunk * (r + 1))
    e_b, e_nb, e_out, decay = _decays(la, chunk)
    q_in = q * e_b
    for ql, own in q_targets:
        q_ref[rows, ql] = q_in if own is None else jnp.where(own, q_in, 0.0)
    kin_ref[rows, kl] = k * e_nb
    ko_ref[rows, kl] = k * e_out
    d_ref[SUBLANES * r:SUBLANES * (r + 1), kl] = decay


def _proj_kernel(x_ref, gm_ref, w_ref, w_a2_ref, b_a_ref, lbp_ref, *refs, chunk, paired, n_cast):
    cast_in, refs = refs[:n_cast], refs[n_cast:]
    n_key_refs = 6 if paired else 4
    key_refs = refs[:n_key_refs]
    v_ref, g_ref = refs[n_key_refs:n_key_refs + 2]
    cast_out = refs[n_key_refs + 2:n_key_refs + 2 + n_cast]
    p_refs = refs[n_key_refs + 2 + n_cast:]
    tm = x_ref.shape[0]
    block = 2 * chunk if paired else SUBLANES * chunk
    emit_keys = (_emit_keys_paired if paired
                 else functools.partial(_emit_keys_single, chunk=chunk))
    blocks = [(slice(r * block, (r + 1) * block), r) for r in range(tm // block)]
    h = _rmsnorm(x_ref[...], gm_ref[...]).astype(bf16)

    w_a2 = w_a2_ref[...]
    b_a = b_a_ref[...]
    lb = _lower_bound(lbp_ref)
    lane = lax.broadcasted_iota(jnp.int32, (block, HEAD), 1)

    def p_slot(lo, hi):
        for (g_lo, g_hi), ref in zip(PROJ_GROUPS, p_refs):
            if g_lo <= lo and hi <= g_hi:
                return ref, slice(lo - g_lo, hi - g_lo)
        raise ValueError((lo, hi))

    def project(lo, hi):
        ref, cols = p_slot(lo, hi)
        ref[:, cols] = jnp.dot(h, w_ref[:, lo:hi], preferred_element_type=f32)

    def p(rows, lo, width):
        ref, cols = p_slot(lo, lo + width)
        return ref[rows, cols]

    def gla_keys(rows, r):
        lra = p(rows, OFF_LR, HEAD)
        z = jnp.dot(lra.astype(bf16), w_a2, preferred_element_type=f32) + b_a
        for pr in range(H_A // 2):
            kl = slice(pr * HEAD, (pr + 1) * HEAD)
            zz = z[:, kl]
            la = (jnp.minimum(zz, 0.0) - jnp.log(1.0 + jnp.exp(-jnp.abs(zz)))) * (1.0 / GATE_NORM)
            q = p(rows, OFF_QA + pr * HEAD, HEAD) * (DK_A ** -0.5)
            k = p(rows, OFF_KA + pr * HEAD, HEAD)
            q_targets = [(slice(hd * HEAD, (hd + 1) * HEAD), own)
                         for hd, own in ((2 * pr, lane < DK_A), (2 * pr + 1, lane >= DK_A))]
            emit_keys(key_refs, r, q_targets, kl, q, k, la)

    def hgrn_keys(rows, r):
        for hd in range(N_HEADS - H_A):
            lbh = lb[:, hd * HEAD:(hd + 1) * HEAD]
            f = lbh + (1.0 - lbh) * _sigmoid(p(rows, OFF_FB + hd * HEAD, HEAD))
            q = _silu(p(rows, OFF_QB + hd * HEAD, HEAD))
            ql = slice(SEC + hd * HEAD, SEC + (hd + 1) * HEAD)
            kl = slice(KA_W + hd * HEAD, KA_W + (hd + 1) * HEAD)
            emit_keys(key_refs, r, [(ql, None)], kl, q, 1.0 - f, jnp.log(f))

    def gla_values(rows, crow):
        v_ref[rows, 0:SEC] = p(rows, OFF_VA, SEC).astype(v_ref.dtype)
        g_ref[rows, 0:SEC] = p(rows, OFF_GA, SEC)

    def hgrn_values(rows, crow):
        v_ref[rows, SEC:2 * SEC] = p(rows, OFF_IB, SEC).astype(v_ref.dtype)
        g_ref[rows, SEC:2 * SEC] = p(rows, OFF_GB, SEC)

    def tasks(fn, which):
        return [functools.partial(fn, *blocks[r]) for r in which]

    def cast(src, dst):
        dst[...] = src[...].astype(dst.dtype)

    nb = len(blocks)
    first, second = range(nb // 2), range(nb // 2, nb)
    windows = (
        [functools.partial(cast, s, d) for s, d in zip(cast_in, cast_out)],
        tasks(gla_keys, range(nb)),
        tasks(hgrn_keys, first),
        tasks(hgrn_keys, second) + tasks(gla_values, range(nb)),
    )
    for (lo, hi), work in zip(PROJ_GROUPS, windows):
        starts = list(range(lo, hi, MXU_N))
        for i, c0 in enumerate(starts):
            project(c0, min(c0 + MXU_N, hi))
            for t in work[i * len(work) // len(starts):(i + 1) * len(work) // len(starts)]:
                t()
    for t in tasks(hgrn_values, range(nb)):
        t()


def _slice_spec(shape, steps):
    rows, width = shape
    height = next(r for r in range(BF16_ROWS, rows + 1, BF16_ROWS)
                  if rows % r == 0 and rows // r <= steps)
    last = rows // height - 1
    return pl.BlockSpec((height, width), lambda i: (jnp.minimum(i, last), 0))


def _proj(x2d, g_mix, w_in_p, w_a2p, b_ap, lb_param, *, chunk, paired, cast=()):
    n = x2d.shape[0]
    tm = ROW_TILE
    steps = n // tm
    kern = functools.partial(_proj_kernel, chunk=chunk, paired=paired, n_cast=len(cast))

    def arr(rows, width, dtype):
        return (jax.ShapeDtypeStruct((n // rows, width), dtype),
                pl.BlockSpec((tm // rows, width), lambda i: (i, 0)))

    if paired:
        keys = [arr(1, D_MODEL, bf16), arr(1, D_MODEL, bf16), arr(2, K_W, bf16),
                arr(1, K_W, bf16), arr(1, K_W, bf16), arr(2 * chunk, K_W, f32)]
        v_out = arr(1, D_MODEL, bf16)
    else:
        keys = [arr(1, D_MODEL, f32), arr(1, K_W, f32), arr(1, K_W, f32), arr(chunk, K_W, f32)]
        v_out = arr(1, D_MODEL, f32)
    outs = keys + [v_out, arr(1, D_MODEL, f32)]
    cast_specs = [_slice_spec(w.shape, steps) for w in cast]
    outs += [(jax.ShapeDtypeStruct(w.shape, bf16), spec) for w, spec in zip(cast, cast_specs)]

    return pl.pallas_call(
        kern,
        out_shape=tuple(o[0] for o in outs),
        grid=(steps,),
        in_specs=[
            pl.BlockSpec((tm, D_MODEL), lambda i: (i, 0)),
            _const_spec((1, D_MODEL)),
            _const_spec((D_MODEL, D_P)),
            _const_spec(w_a2p.shape),
            _const_spec(b_ap.shape),
            _const_spec(lb_param.shape),
        ] + cast_specs,
        out_specs=tuple(o[1] for o in outs),
        scratch_shapes=[pltpu.VMEM((tm, hi - lo), f32) for lo, hi in PROJ_GROUPS],
        compiler_params=pltpu.CompilerParams(
            dimension_semantics=("arbitrary",), vmem_limit_bytes=VMEM_LIMIT),
        name="proj",
    )(x2d, g_mix, w_in_p, w_a2p, b_ap, lb_param, *cast)


def _head_norm(o, onorm):
    return o * lax.rsqrt(jnp.mean(o * o, axis=-1, keepdims=True) + EPS) * onorm


def _tril(n):
    r = lax.broadcasted_iota(jnp.int32, (n, n), 0)
    c = lax.broadcasted_iota(jnp.int32, (n, n), 1)
    return r >= c


def _mixer_prompt_kernel(q32_ref, q64_ref, kin_ref, kx_ref, ko_ref, d_ref, v_ref, g_ref,
                         ona_ref, onb_ref, o_ref, sa_ref, sb_ref,
                         st_ref, att_ref, up_ref, start_ref, *, tile, chunk):
    j = pl.program_id(1)

    @pl.when(j == 0)
    def _():
        st_ref[...] = jnp.zeros_like(st_ref)

    onorm = [ona_ref[...]] * H_A + [onb_ref[...]] * (N_HEADS - H_A)
    pair = 2 * chunk
    n_pairs = tile // pair
    stack = max(len(heads) for heads in _KEY_SHARING) * chunk
    r = lax.broadcasted_iota(jnp.int32, (stack, pair), 0) % chunk
    c = lax.broadcasted_iota(jnp.int32, (stack, pair), 1)
    tril = (r >= c)[:, :chunk]
    cross = (c < chunk) | (r >= c - chunk)

    for heads in _KEY_SHARING:
        n = len(heads)
        kl = slice(_key_lanes(heads[0]), _key_lanes(heads[0]) + HEAD)
        vl = slice(heads[0] * HEAD, (heads[-1] + 1) * HEAD)
        for s in range(n_pairs):
            ra = slice(s * pair, s * pair + chunk)
            rb = slice(s * pair + chunk, (s + 1) * pair)
            rp = slice(s * pair, (s + 1) * pair)
            qa = [q32_ref[ra, h * HEAD:(h + 1) * HEAD] for h in heads]
            qb = [q32_ref[rb, h * HEAD:(h + 1) * HEAD] for h in heads]
            qa = qa[0] if n == 1 else jnp.concatenate(qa, axis=0)
            qb = qb[0] if n == 1 else jnp.concatenate(qb, axis=0)
            a_keys = kin_ref[s * chunk:(s + 1) * chunk, kl]
            att0 = jnp.where(tril[:n * chunk], _dot_nt(qa, a_keys), 0.0).astype(bf16)
            att1 = jnp.where(cross[:n * chunk], _dot_nt(qb, kx_ref[rp, kl]), 0.0).astype(bf16)
            up = _dot_tn(v_ref[rp, vl], ko_ref[rp, kl])
            att0 = jnp.concatenate([att0, jnp.zeros_like(att0)], axis=1)
            for i, h in enumerate(heads):
                u = h * n_pairs + s
                att_ref[u, :chunk] = att0[i * chunk:(i + 1) * chunk]
                att_ref[u, chunk:] = att1[i * chunk:(i + 1) * chunk]
                up_ref[u] = up[i * HEAD:(i + 1) * HEAD]

    for h in range(N_HEADS):
        kl = slice(_key_lanes(h), _key_lanes(h) + HEAD)
        st = st_ref[h]
        for s in range(n_pairs):
            u = h * n_pairs + s
            start_ref[u] = st.astype(bf16)
            st = d_ref[s:s + 1, kl] * st + up_ref[u]
        st_ref[h] = st

    for h in range(N_HEADS):
        sl = slice(h * HEAD, (h + 1) * HEAD)
        outs = []
        for s in range(n_pairs):
            u = h * n_pairs + s
            rp = slice(s * pair, (s + 1) * pair)
            outs.append(_dot_nt(q64_ref[rp, sl], start_ref[u])
                        + jnp.dot(att_ref[u], v_ref[rp, sl], preferred_element_type=f32))
        o = jnp.concatenate(outs, axis=0)
        o_ref[:, sl] = (_head_norm(o, onorm[h]) * _silu(g_ref[:, sl])).astype(o_ref.dtype)

    @pl.when(j == pl.num_programs(1) - 1)
    def _():
        for h in range(H_A):
            sa_ref[0, h] = st_ref[h].T[_own_keys(h), :]
            sb_ref[0, h] = st_ref[H_A + h].T


def _mixer_prompt(acts, ona, onb, batch, seq):
    tile = ROW_TILE
    nt = seq // tile
    chunk = PROMPT_CHUNK
    units = N_HEADS * (tile // (2 * chunk))
    kern = functools.partial(_mixer_prompt_kernel, tile=tile, chunk=chunk)

    def rows(width, per=1):
        return pl.BlockSpec((tile // per, width), lambda b, j: (b * nt + j, 0))

    return pl.pallas_call(
        kern,
        out_shape=(
            jax.ShapeDtypeStruct((batch * seq, D_MODEL), bf16),
            jax.ShapeDtypeStruct((batch, H_A, DK_A, HEAD), f32),
            jax.ShapeDtypeStruct((batch, H_A, HEAD, HEAD), f32),
        ),
        grid=(batch, nt),
        in_specs=[
            rows(D_MODEL), rows(D_MODEL), rows(K_W, 2), rows(K_W), rows(K_W),
            rows(K_W, 2 * chunk), rows(D_MODEL), rows(D_MODEL),
            _const_spec(ona.shape),
            _const_spec(onb.shape),
        ],
        out_specs=(
            rows(D_MODEL),
            pl.BlockSpec((1, H_A, DK_A, HEAD), lambda b, j: (b, 0, 0, 0)),
            pl.BlockSpec((1, H_A, HEAD, HEAD), lambda b, j: (b, 0, 0, 0)),
        ),
        scratch_shapes=[
            pltpu.VMEM((N_HEADS, HEAD, HEAD), f32),
            pltpu.VMEM((units, 2 * chunk, 2 * chunk), bf16),
            pltpu.VMEM((units, HEAD, HEAD), f32),
            pltpu.VMEM((units, HEAD, HEAD), bf16),
        ],
        compiler_params=pltpu.CompilerParams(
            dimension_semantics=("arbitrary", "arbitrary"), vmem_limit_bytes=VMEM_LIMIT),
        name="mixer_prompt",
    )(*acts, ona, onb)


def _mixer_sample_kernel(qi_ref, ki_ref, ko_ref, v_ref, g_ref, d_ref, sa_in_ref, sb_in_ref,
                         ona_ref, onb_ref, o_ref, sa_ref, sb_ref, att_ref, oi_ref, *, bt, seq):
    onorm = [ona_ref[...]] * H_A + [onb_ref[...]] * (N_HEADS - H_A)
    tril = _tril(seq)
    zpad = jnp.zeros((HEAD - DK_A, HEAD), f32)
    zeros = jnp.zeros((seq, HEAD), f32)
    row = lax.broadcasted_iota(jnp.int32, (seq, HEAD), 0)
    ones_rows = jnp.where(row < 2, 1.0, 0.0)
    rhs_bottom = jnp.concatenate([zeros, ones_rows], axis=1)

    for bi in range(bt):
        rows = slice(bi * seq, (bi + 1) * seq)
        decay = d_ref[bi:bi + 1, :]
        d_hi = decay.astype(bf16).astype(f32)
        d_lo = decay - d_hi
        for h in range(N_HEADS):
            sl = slice(h * HEAD, (h + 1) * HEAD)
            kl = slice(_key_lanes(h), _key_lanes(h) + HEAD)
            q_in = qi_ref[rows, sl].astype(bf16)
            k_in = ki_ref[rows, kl].astype(bf16)
            v32 = v_ref[rows, sl]
            if h >= H_A:
                s = sb_in_ref[bi, h - H_A]
            elif h % 2 == 0:
                s = jnp.concatenate([sa_in_ref[bi, h], zpad], axis=0)
            else:
                s = jnp.concatenate([zpad, sa_in_ref[bi, h]], axis=0)
            att_ref[bi * N_HEADS + h] = jnp.where(tril, _dot_nt(q_in, k_in), 0.0)
            oi_ref[rows, sl] = jnp.dot(q_in, s.astype(bf16), preferred_element_type=f32)
            d_rows = jnp.where(row == 0, d_hi[:, kl], jnp.where(row == 1, d_lo[:, kl], 0.0))
            lhs = jnp.concatenate([ko_ref[rows, kl], d_rows], axis=0).astype(bf16)
            rhs = jnp.concatenate(
                [jnp.concatenate([v32, zeros], axis=1), rhs_bottom], axis=0).astype(bf16)
            upd = _dot_tn(lhs, rhs)
            s_new = upd[:, HEAD:] * s + upd[:, :HEAD]
            if h < H_A:
                sa_ref[bi, h] = s_new[_own_keys(h), :]
            else:
                sb_ref[bi, h - H_A] = s_new

    for bi in range(bt):
        rows = slice(bi * seq, (bi + 1) * seq)
        for h in range(N_HEADS):
            sl = slice(h * HEAD, (h + 1) * HEAD)
            att = att_ref[bi * N_HEADS + h].astype(bf16)
            o = oi_ref[rows, sl] + jnp.dot(att, v_ref[rows, sl].astype(bf16),
                                           preferred_element_type=f32)
            o_ref[rows, sl] = (_head_norm(o, onorm[h]) * _silu(g_ref[rows, sl])).astype(
                o_ref.dtype)


def _mixer_sample(acts, sa_in, sb_in, ona, onb, batch, seq):
    bt = SAMPLE_SEQS
    kern = functools.partial(_mixer_sample_kernel, bt=bt, seq=seq)

    def rows(width):
        return pl.BlockSpec((bt * seq, width), lambda i: (i, 0))

    sa_spec = pl.BlockSpec((bt, H_A, DK_A, HEAD), lambda i: (i, 0, 0, 0))
    sb_spec = pl.BlockSpec((bt, H_A, HEAD, HEAD), lambda i: (i, 0, 0, 0))
    return pl.pallas_call(
        kern,
        out_shape=(
            jax.ShapeDtypeStruct((batch * seq, D_MODEL), f32),
            jax.ShapeDtypeStruct((batch, H_A, DK_A, HEAD), f32),
            jax.ShapeDtypeStruct((batch, H_A, HEAD, HEAD), f32),
        ),
        grid=(batch // bt,),
        in_specs=[
            rows(D_MODEL), rows(K_W), rows(K_W), rows(D_MODEL), rows(D_MODEL),
            pl.BlockSpec((bt, K_W), lambda i: (i, 0)),
            sa_spec, sb_spec,
            _const_spec(ona.shape),
            _const_spec(onb.shape),
        ],
        out_specs=(rows(D_MODEL), sa_spec, sb_spec),
        scratch_shapes=[
            pltpu.VMEM((bt * N_HEADS, seq, seq), f32),
            pltpu.VMEM((bt * seq, D_MODEL), f32),
        ],
        compiler_params=pltpu.CompilerParams(
            dimension_semantics=("arbitrary",), vmem_limit_bytes=VMEM_LIMIT),
        name="mixer_sample",
    )(*acts, sa_in, sb_in, ona, onb)


def _out_kernel(o_ref, x_ref, w_o_ref, gf_ref, w1_ref, w3_ref, w2_ref, gl_ref, y_ref):
    x1 = x_ref[...] + jnp.dot(o_ref[...].astype(bf16), w_o_ref[...], preferred_element_type=f32)
    h = _rmsnorm(x1, gf_ref[...]).astype(bf16)
    acc = x1
    for c in range(D_FF // FF_CHUNK):
        cols = slice(c * FF_CHUNK, (c + 1) * FF_CHUNK)
        a = jnp.dot(h, w1_ref[:, cols], preferred_element_type=f32)
        b = jnp.dot(h, w3_ref[:, cols], preferred_element_type=f32)
        u = (_silu(a) * b).astype(bf16)
        acc = acc + jnp.dot(u, w2_ref[cols, :], preferred_element_type=f32)
    y_ref[...] = _rmsnorm(acc, gl_ref[...])


def _out(o2d, x2d, w_o, g_ffn, w1, w3, w2, g_final):
    n = x2d.shape[0]
    row_spec = pl.BlockSpec((ROW_TILE, D_MODEL), lambda i: (i, 0))
    return pl.pallas_call(
        _out_kernel,
        out_shape=jax.ShapeDtypeStruct((n, D_MODEL), f32),
        grid=(n // ROW_TILE,),
        in_specs=[
            row_spec, row_spec,
            _const_spec((D_MODEL, D_MODEL)),
            _const_spec((1, D_MODEL)),
            _const_spec((D_MODEL, D_FF)),
            _const_spec((D_MODEL, D_FF)),
            _const_spec((D_FF, D_MODEL)),
            _const_spec((1, D_MODEL)),
        ],
        out_specs=row_spec,
        compiler_params=pltpu.CompilerParams(
            dimension_semantics=("arbitrary",), vmem_limit_bytes=VMEM_LIMIT),
        name="out_ffn",
    )(o2d, x2d, w_o, g_ffn, w1, w3, w2, g_final)


def _w_in_sections():
    sections, src = [], 0
    for width, dst in ((KA_W, OFF_QA), (KA_W, OFF_KA), (SEC, OFF_VA), (SEC, OFF_GA),
                       (GATE_RANK, OFF_LR), (SEC, OFF_QB), (SEC, OFF_FB), (SEC, OFF_IB),
                       (SEC, OFF_GB)):
        sections.append((src, width, dst))
        src += width
    return tuple(sections), src


_W_IN_SECTIONS, D_IN = _w_in_sections()


def _w_in_layout_kernel(wt_ref, o_ref):
    for src, width, dst in _W_IN_SECTIONS:
        part = wt_ref[src:src + width, :]
        if width % HEAD:
            pad = jnp.zeros((HEAD - width % HEAD, part.shape[1]), part.dtype)
            part = jnp.concatenate([part, pad], axis=0)
        o_ref[:, dst:dst + part.shape[0]] = part.T.astype(o_ref.dtype)


def _layout_w_in(w_in_t, cols=256):
    return pl.pallas_call(
        _w_in_layout_kernel,
        out_shape=jax.ShapeDtypeStruct((D_MODEL, D_P), bf16),
        grid=(D_MODEL // cols,),
        in_specs=[pl.BlockSpec((D_IN, cols), lambda i: (0, i))],
        out_specs=pl.BlockSpec((cols, D_P), lambda i: (i, 0)),
        compiler_params=pltpu.CompilerParams(
            dimension_semantics=("arbitrary",), vmem_limit_bytes=VMEM_LIMIT),
        name="w_in_layout",
    )(w_in_t)


def kernel(x_prompt, x_sample, state_gla, state_hgrn, norm_mix, w_in, w_a2, b_a, lb_param,
           onorm_a, onorm_b, w_o, norm_ffn, w1, w3, w2, norm_final):
    batch, seq, _ = x_prompt.shape
    dec_batch, dec_seq, _ = x_sample.shape

    w_in_p = _layout_w_in(jnp.swapaxes(w_in[0], 0, 1))
    w_a2p = jnp.pad(w_a2[0], ((0, HEAD - GATE_RANK), (0, 0))).astype(bf16)
    b_ap = b_a[0].reshape(1, KA_W)
    g_mix = norm_mix[0].reshape(1, D_MODEL)
    g_ffn = norm_ffn[0].reshape(1, D_MODEL)
    g_fin = norm_final.reshape(1, D_MODEL)
    ona = onorm_a[0].reshape(1, HEAD)
    onb = onorm_b[0].reshape(1, HEAD)

    xp = x_prompt.reshape(batch * seq, D_MODEL)
    xs = x_sample.reshape(dec_batch * dec_seq, D_MODEL)

    later = (w_o[0], w1[0], w3[0], w2[0])
    *acts_p, w_o_b, w1_b, w3_b, w2_b = _proj(xp, g_mix, w_in_p, w_a2p, b_ap, lb_param,
                                             chunk=PROMPT_CHUNK, paired=True, cast=later)
    op, sa_p, sb_p = _mixer_prompt(acts_p, ona, onb, batch, seq)
    yp = _out(op, xp, w_o_b, g_ffn, w1_b, w3_b, w2_b, g_fin)

    q_s, kin_s, ko_s, d_s, v_s, g_s = _proj(xs, g_mix, w_in_p, w_a2p, b_ap, lb_param,
                                            chunk=dec_seq, paired=False)
    os_, sa_s, sb_s = _mixer_sample((q_s, kin_s, ko_s, v_s, g_s, d_s), state_gla[0],
                                    state_hgrn[0], ona, onb, dec_batch, dec_seq)
    ys = _out(os_, xs, w_o_b, g_ffn, w1_b, w3_b, w2_b, g_fin)

    return (yp.reshape(batch, seq, D_MODEL), ys.reshape(dec_batch, dec_seq, D_MODEL),
            sa_p[None], sb_p[None], sa_s[None], sb_s[None])
```

```python
import functools

import jax
import jax.numpy as jnp
from jax import lax
from jax.experimental import pallas as pl
from jax.experimental.pallas import tpu as pltpu

D_MODEL = 1024
N_HEADS = 8
H_A = 4
DK_A = 64
HEAD = 128
GATE_RANK = 16
GATE_NORM = 16.0
D_FF = 2816
FF_CHUNK = 256
EPS = 1e-6
PROMPT_CHUNK = 32
SUBLANES = 8
BF16_ROWS = 16

KA_W = H_A * DK_A
SEC = H_A * HEAD
K_W = KA_W + SEC
OFF_LR = 0
OFF_QA = HEAD
OFF_KA = OFF_QA + KA_W
OFF_VA = OFF_KA + KA_W
OFF_GA = OFF_VA + SEC
OFF_FB = OFF_GA + SEC
OFF_QB = OFF_FB + SEC
OFF_IB = OFF_QB + SEC
OFF_GB = OFF_IB + SEC
D_P = OFF_GB + SEC
MXU_N = 256
PROJ_GROUPS = ((OFF_LR, OFF_VA), (OFF_FB, OFF_IB), (OFF_VA, OFF_FB), (OFF_IB, D_P))

ROW_TILE = 512
SAMPLE_SEQS = 16
VMEM_LIMIT = 56 * 1024 * 1024

f32 = jnp.float32
bf16 = jnp.bfloat16


def _const_spec(shape):
    nd = len(shape)
    return pl.BlockSpec(shape, lambda *_: (0,) * nd, pipeline_mode=pl.Buffered(1))


def _rmsnorm(x, g):
    return x * lax.rsqrt(jnp.mean(x * x, axis=-1, keepdims=True) + EPS) * g


def _sigmoid(x):
    return 0.5 * jnp.tanh(0.5 * x) + 0.5


def _silu(x):
    hx = 0.5 * x
    return hx + hx * jnp.tanh(hx)


def _dot_nt(a, b):
    return lax.dot_general(a, b, (((1,), (1,)), ((), ())), preferred_element_type=f32)


def _dot_tn(a, b):
    return lax.dot_general(a, b, (((0,), (0,)), ((), ())), preferred_element_type=f32)


def _key_lanes(h):
    return HEAD * (h // 2) if h < H_A else KA_W + HEAD * (h - H_A)


_KEY_SHARING = tuple((h, h + 1) for h in range(0, H_A, 2)) + tuple(
    (h,) for h in range(H_A, N_HEADS))


def _own_keys(h):
    return slice(DK_A * (h % 2), DK_A * (h % 2 + 1)) if h < H_A else slice(0, HEAD)


def _cumsum_rows(x, chunk):
    rows, width = x.shape
    row = lax.broadcasted_iota(jnp.int32, (SUBLANES, width), 0)
    groups_per_chunk = chunk // SUBLANES
    out = []
    carry = None
    for g in range(rows // SUBLANES):
        y = x[g * SUBLANES:(g + 1) * SUBLANES]
        for s in (1, 2, 4):
            y = y + jnp.where(row >= s, pltpu.roll(y, s, 0), 0.0)
        if g % groups_per_chunk != 0:
            y = y + carry
        carry = y[SUBLANES - 1:SUBLANES]
        out.append(y)
    return out[0] if len(out) == 1 else jnp.concatenate(out, axis=0)


def _lower_bound(lbp_ref):
    lbp = lbp_ref[...]
    m = jnp.max(lbp, axis=0, keepdims=True)
    e = jnp.exp(lbp - m)
    return e[0:1] / jnp.sum(e, axis=0, keepdims=True)


def _decays(la, chunk):
    rows = la.shape[0]
    b = _cumsum_rows(la, chunk)
    n_chunks = rows // chunk
    lasts = [b[(c + 1) * chunk - 1:(c + 1) * chunk] for c in range(n_chunks)]
    if n_chunks == 1:
        b_last = lasts[0]
        b_last_rows = b_last
    else:
        b_last = jnp.concatenate(lasts, axis=0)
        b_last_rows = jnp.concatenate(
            [jnp.broadcast_to(l, (chunk, l.shape[1])) for l in lasts], axis=0)
    return jnp.exp(b), jnp.exp(-b), jnp.exp(b_last_rows - b), jnp.exp(b_last)


def _emit_keys_paired(refs, r, q_targets, kl, q, k, la):
    q32_ref, q64_ref, kin_ref, kx_ref, ko_ref, d_ref = refs
    c = PROMPT_CHUNK
    rows = slice(2 * c * r, 2 * c * (r + 1))
    e_b, e_nb, e_out, decay = _decays(la, c)
    d0, d1 = decay[0:1], decay[1:2]
    q32 = q * e_b
    q64 = jnp.concatenate([q32[:c], q32[c:] * d0], axis=0)
    k_in = k * e_nb
    k_out = k * e_out
    for ql, own in q_targets:
        q32_ref[rows, ql] = (q32 if own is None else jnp.where(own, q32, 0.0)).astype(bf16)
        q64_ref[rows, ql] = (q64 if own is None else jnp.where(own, q64, 0.0)).astype(bf16)
    kin_ref[c * r:c * (r + 1), kl] = k_in[:c].astype(bf16)
    kx_ref[rows, kl] = jnp.concatenate([k_out[:c], k_in[c:]], axis=0).astype(bf16)
    ko_ref[rows, kl] = jnp.concatenate([k_out[:c] * d1, k_out[c:]], axis=0).astype(bf16)
    d_ref[r:r + 1, kl] = d0 * d1


def _emit_keys_single(refs, r, q_targets, kl, q, k, la, *, chunk):
    q_ref, kin_ref, ko_ref, d_ref = refs
    rows = slice(SUBLANES * chunk * r, SUBLANES * chunk * (r + 1))
    e_b, e_nb, e_out, decay = _decays(la, chunk)
    q_in = q * e_b
    for ql, own in q_targets:
        q_ref[rows, ql] = q_in if own is None else jnp.where(own, q_in, 0.0)
    kin_ref[rows, kl] = k * e_nb
    ko_ref[rows, kl] = k * e_out
    d_ref[SUBLANES * r:SUBLANES * (r + 1), kl] = decay


def _proj_kernel(x_ref, gm_ref, w_ref, w_a2_ref, b_a_ref, lbp_ref, *refs, chunk, paired, n_cast):
    cast_in, refs = refs[:n_cast], refs[n_cast:]
    n_key_refs = 6 if paired else 4
    key_refs = refs[:n_key_refs]
    v_ref, g_ref = refs[n_key_refs:n_key_refs + 2]
    cast_out = refs[n_key_refs + 2:n_key_refs + 2 + n_cast]
    p_refs = refs[n_key_refs + 2 + n_cast:]
    tm = x_ref.shape[0]
    block = 2 * chunk if paired else SUBLANES * chunk
    emit_keys = (_emit_keys_paired if paired
                 else functools.partial(_emit_keys_single, chunk=chunk))
    blocks = [(slice(r * block, (r + 1) * block), r) for r in range(tm // block)]
    h = _rmsnorm(x_ref[...], gm_ref[...]).astype(bf16)

    w_a2 = w_a2_ref[...]
    b_a = b_a_ref[...]
    lb = _lower_bound(lbp_ref)
    lane = lax.broadcasted_iota(jnp.int32, (block, HEAD), 1)

    def p_slot(lo, hi):
        for (g_lo, g_hi), ref in zip(PROJ_GROUPS, p_refs):
            if g_lo <= lo and hi <= g_hi:
                return ref, slice(lo - g_lo, hi - g_lo)
        raise ValueError((lo, hi))

    def project(lo, hi):
        ref, cols = p_slot(lo, hi)
        ref[:, cols] = jnp.dot(h, w_ref[:, lo:hi], preferred_element_type=f32)

    def p(rows, lo, width):
        ref, cols = p_slot(lo, lo + width)
        return ref[rows, cols]

    def gla_keys(rows, r):
        lra = p(rows, OFF_LR, HEAD)
        z = jnp.dot(lra.astype(bf16), w_a2, preferred_element_type=f32) + b_a
        for pr in range(H_A // 2):
            kl = slice(pr * HEAD, (pr + 1) * HEAD)
            zz = z[:, kl]
            la = (jnp.minimum(zz, 0.0) - jnp.log(1.0 + jnp.exp(-jnp.abs(zz)))) * (1.0 / GATE_NORM)
            q = p(rows, OFF_QA + pr * HEAD, HEAD) * (DK_A ** -0.5)
            k = p(rows, OFF_KA + pr * HEAD, HEAD)
            q_targets = [(slice(hd * HEAD, (hd + 1) * HEAD), own)
                         for hd, own in ((2 * pr, lane < DK_A), (2 * pr + 1, lane >= DK_A))]
            emit_keys(key_refs, r, q_targets, kl, q, k, la)

    def hgrn_keys(rows, r):
        for hd in range(N_HEADS - H_A):
            lbh = lb[:, hd * HEAD:(hd + 1) * HEAD]
            f = lbh + (1.0 - lbh) * _sigmoid(p(rows, OFF_FB + hd * HEAD, HEAD))
            q = _silu(p(rows, OFF_QB + hd * HEAD, HEAD))
            ql = slice(SEC + hd * HEAD, SEC + (hd + 1) * HEAD)
            kl = slice(KA_W + hd * HEAD, KA_W + (hd + 1) * HEAD)
            emit_keys(key_refs, r, [(ql, None)], kl, q, 1.0 - f, jnp.log(f))

    def gla_values(rows, crow):
        v_ref[rows, 0:SEC] = p(rows, OFF_VA, SEC).astype(v_ref.dtype)
        g_ref[rows, 0:SEC] = p(rows, OFF_GA, SEC)

    def hgrn_values(rows, crow):
        v_ref[rows, SEC:2 * SEC] = p(rows, OFF_IB, SEC).astype(v_ref.dtype)
        g_ref[rows, SEC:2 * SEC] = p(rows, OFF_GB, SEC)

    def tasks(fn, which):
        return [functools.partial(fn, *blocks[r]) for r in which]

    def cast(src, dst):
        dst[...] = src[...].astype(dst.dtype)

    nb = len(blocks)
    first, second = range(nb // 2), range(nb // 2, nb)
    windows = (
        [functools.partial(cast, s, d) for s, d in zip(cast_in, cast_out)],
        tasks(gla_keys, range(nb)),
        tasks(hgrn_keys, first),
        tasks(hgrn_keys, second) + tasks(gla_values, range(nb)),
    )
    for (lo, hi), work in zip(PROJ_GROUPS, windows):
        starts = list(range(lo, hi, MXU_N))
        for i, c0 in enumerate(starts):
            project(c0, min(c0 + MXU_N, hi))
            for t in work[i * len(work) // len(starts):(i + 1) * len(work) // len(starts)]:
                t()
    for t in tasks(hgrn_values, range(nb)):
        t()


def _slice_spec(shape, steps):
    rows, width = shape
    height = next(r for r in range(BF16_ROWS, rows + 1, BF16_ROWS)
                  if rows % r == 0 and rows // r <= steps)
    last = rows // height - 1
    return pl.BlockSpec((height, width), lambda i: (jnp.minimum(i, last), 0))


def _proj(x2d, g_mix, w_in_p, w_a2p, b_ap, lb_param, *, chunk, paired, cast=()):
    n = x2d.shape[0]
    tm = ROW_TILE
    steps = n // tm
    kern = functools.partial(_proj_kernel, chunk=chunk, paired=paired, n_cast=len(cast))

    def arr(rows, width, dtype):
        return (jax.ShapeDtypeStruct((n // rows, width), dtype),
                pl.BlockSpec((tm // rows, width), lambda i: (i, 0)))

    if paired:
        keys = [arr(1, D_MODEL, bf16), arr(1, D_MODEL, bf16), arr(2, K_W, bf16),
                arr(1, K_W, bf16), arr(1, K_W, bf16), arr(2 * chunk, K_W, f32)]
        v_out = arr(1, D_MODEL, bf16)
    else:
        keys = [arr(1, D_MODEL, f32), arr(1, K_W, f32), arr(1, K_W, f32), arr(chunk, K_W, f32)]
        v_out = arr(1, D_MODEL, f32)
    outs = keys + [v_out, arr(1, D_MODEL, f32)]
    cast_specs = [_slice_spec(w.shape, steps) for w in cast]
    outs += [(jax.ShapeDtypeStruct(w.shape, bf16), spec) for w, spec in zip(cast, cast_specs)]

    return pl.pallas_call(
        kern,
        out_shape=tuple(o[0] for o in outs),
        grid=(steps,),
        in_specs=[
            pl.BlockSpec((tm, D_MODEL), lambda i: (i, 0)),
            _const_spec((1, D_MODEL)),
            _const_spec((D_MODEL, D_P)),
            _const_spec(w_a2p.shape),
            _const_spec(b_ap.shape),
            _const_spec(lb_param.shape),
        ] + cast_specs,
        out_specs=tuple(o[1] for o in outs),
        scratch_shapes=[pltpu.VMEM((tm, hi - lo), f32) for lo, hi in PROJ_GROUPS],
        compiler_params=pltpu.CompilerParams(
            dimension_semantics=("arbitrary",), vmem_limit_bytes=VMEM_LIMIT),
        name="proj",
    )(x2d, g_mix, w_in_p, w_a2p, b_ap, lb_param, *cast)


def _head_norm(o, onorm):
    return o * lax.rsqrt(jnp.mean(o * o, axis=-1, keepdims=True) + EPS) * onorm


def _tril(n):
    r = lax.broadcasted_iota(jnp.int32, (n, n), 0)
    c = lax.broadcasted_iota(jnp.int32, (n, n), 1)
    return r >= c


def _mixer_prompt_kernel(q32_ref, q64_ref, kin_ref, kx_ref, ko_ref, d_ref, v_ref, g_ref,
                         ona_ref, onb_ref, o_ref, sa_ref, sb_ref,
                         st_ref, att_ref, up_ref, start_ref, *, tile, chunk):
    j = pl.program_id(1)

    @pl.when(j == 0)
    def _():
        st_ref[...] = jnp.zeros_like(st_ref)

    onorm = [ona_ref[...]] * H_A + [onb_ref[...]] * (N_HEADS - H_A)
    pair = 2 * chunk
    n_pairs = tile // pair
    stack = max(len(heads) for heads in _KEY_SHARING) * chunk
    r = lax.broadcasted_iota(jnp.int32, (stack, pair), 0) % chunk
    c = lax.broadcasted_iota(jnp.int32, (stack, pair), 1)
    tril = (r >= c)[:, :chunk]
    cross = (c < chunk) | (r >= c - chunk)

    for heads in _KEY_SHARING:
        n = len(heads)
        kl = slice(_key_lanes(heads[0]), _key_lanes(heads[0]) + HEAD)
        vl = slice(heads[0] * HEAD, (heads[-1] + 1) * HEAD)
        for s in range(n_pairs):
            ra = slice(s * pair, s * pair + chunk)
            rb = slice(s * pair + chunk, (s + 1) * pair)
            rp = slice(s * pair, (s + 1) * pair)
            qa = [q32_ref[ra, h * HEAD:(h + 1) * HEAD] for h in heads]
            qb = [q32_ref[rb, h * HEAD:(h + 1) * HEAD] for h in heads]
            qa = qa[0] if n == 1 else jnp.concatenate(qa, axis=0)
            qb = qb[0] if n == 1 else jnp.concatenate(qb, axis=0)
            a_keys = kin_ref[s * chunk:(s + 1) * chunk, kl]
            att0 = jnp.where(tril[:n * chunk], _dot_nt(qa, a_keys), 0.0).astype(bf16)
            att1 = jnp.where(cross[:n * chunk], _dot_nt(qb, kx_ref[rp, kl]), 0.0).astype(bf16)
            up = _dot_tn(v_ref[rp, vl], ko_ref[rp, kl])
            att0 = jnp.concatenate([att0, jnp.zeros_like(att0)], axis=1)
            for i, h in enumerate(heads):
                u = h * n_pairs + s
                att_ref[u, :chunk] = att0[i * chunk:(i + 1) * chunk]
                att_ref[u, chunk:] = att1[i * chunk:(i + 1) * chunk]
                up_ref[u] = up[i * HEAD:(i + 1) * HEAD]

    for h in range(N_HEADS):
        kl = slice(_key_lanes(h), _key_lanes(h) + HEAD)
        st = st_ref[h]
        for s in range(n_pairs):
            u = h * n_pairs + s
            start_ref[u] = st.T.astype(bf16)
            st = d_ref[s:s + 1, kl] * st + up_ref[u]
        st_ref[h] = st

    for h in range(N_HEADS):
        sl = slice(h * HEAD, (h + 1) * HEAD)
        outs = []
        for s in range(n_pairs):
            u = h * n_pairs + s
            rp = slice(s * pair, (s + 1) * pair)
            outs.append(jnp.dot(q64_ref[rp, sl], start_ref[u], preferred_element_type=f32)
                        + jnp.dot(att_ref[u], v_ref[rp, sl], preferred_element_type=f32))
        o = jnp.concatenate(outs, axis=0)
        o_ref[:, sl] = (_head_norm(o, onorm[h]) * _silu(g_ref[:, sl])).astype(o_ref.dtype)

    @pl.when(j == pl.num_programs(1) - 1)
    def _():
        for h in range(H_A):
            sa_ref[0, h] = st_ref[h].T[_own_keys(h), :]
            sb_ref[0, h] = st_ref[H_A + h].T


def _mixer_prompt(acts, ona, onb, batch, seq):
    tile = ROW_TILE
    nt = seq // tile
    chunk = PROMPT_CHUNK
    units = N_HEADS * (tile // (2 * chunk))
    kern = functools.partial(_mixer_prompt_kernel, tile=tile, chunk=chunk)

    def rows(width, per=1):
        return pl.BlockSpec((tile // per, width), lambda b, j: (b * nt + j, 0))

    return pl.pallas_call(
        kern,
        out_shape=(
            jax.ShapeDtypeStruct((batch * seq, D_MODEL), bf16),
            jax.ShapeDtypeStruct((batch, H_A, DK_A, HEAD), f32),
            jax.ShapeDtypeStruct((batch, H_A, HEAD, HEAD), f32),
        ),
        grid=(batch, nt),
        in_specs=[
            rows(D_MODEL), rows(D_MODEL), rows(K_W, 2), rows(K_W), rows(K_W),
            rows(K_W, 2 * chunk), rows(D_MODEL), rows(D_MODEL),
            _const_spec(ona.shape),
            _const_spec(onb.shape),
        ],
        out_specs=(
            rows(D_MODEL),
            pl.BlockSpec((1, H_A, DK_A, HEAD), lambda b, j: (b, 0, 0, 0)),
            pl.BlockSpec((1, H_A, HEAD, HEAD), lambda b, j: (b, 0, 0, 0)),
        ),
        scratch_shapes=[
            pltpu.VMEM((N_HEADS, HEAD, HEAD), f32),
            pltpu.VMEM((units, 2 * chunk, 2 * chunk), bf16),
            pltpu.VMEM((units, HEAD, HEAD), f32),
            pltpu.VMEM((units, HEAD, HEAD), bf16),
        ],
        compiler_params=pltpu.CompilerParams(
            dimension_semantics=("arbitrary", "arbitrary"), vmem_limit_bytes=VMEM_LIMIT),
        name="mixer_prompt",
    )(*acts, ona, onb)


def _mixer_sample_kernel(qi_ref, ki_ref, ko_ref, v_ref, g_ref, d_ref, sa_in_ref, sb_in_ref,
                         ona_ref, onb_ref, o_ref, sa_ref, sb_ref, att_ref, oi_ref, *, bt, seq):
    onorm = [ona_ref[...]] * H_A + [onb_ref[...]] * (N_HEADS - H_A)
    tril = _tril(seq)
    zpad = jnp.zeros((HEAD - DK_A, HEAD), f32)
    zeros = jnp.zeros((seq, HEAD), f32)
    row = lax.broadcasted_iota(jnp.int32, (seq, HEAD), 0)
    ones_rows = jnp.where(row < 2, 1.0, 0.0)
    rhs_bottom = jnp.concatenate([zeros, ones_rows], axis=1)

    for bi in range(bt):
        rows = slice(bi * seq, (bi + 1) * seq)
        decay = d_ref[bi:bi + 1, :]
        d_hi = decay.astype(bf16).astype(f32)
        d_lo = decay - d_hi
        for h in range(N_HEADS):
            sl = slice(h * HEAD, (h + 1) * HEAD)
            kl = slice(_key_lanes(h), _key_lanes(h) + HEAD)
            q_in = qi_ref[rows, sl].astype(bf16)
            k_in = ki_ref[rows, kl].astype(bf16)
            v32 = v_ref[rows, sl]
            if h >= H_A:
                s = sb_in_ref[bi, h - H_A]
            elif h % 2 == 0:
                s = jnp.concatenate([sa_in_ref[bi, h], zpad], axis=0)
            else:
                s = jnp.concatenate([zpad, sa_in_ref[bi, h]], axis=0)
            att_ref[bi * N_HEADS + h] = jnp.where(tril, _dot_nt(q_in, k_in), 0.0)
            oi_ref[rows, sl] = jnp.dot(q_in, s.astype(bf16), preferred_element_type=f32)
            d_rows = jnp.where(row == 0, d_hi[:, kl], jnp.where(row == 1, d_lo[:, kl], 0.0))
            lhs = jnp.concatenate([ko_ref[rows, kl], d_rows], axis=0).astype(bf16)
            rhs = jnp.concatenate(
                [jnp.concatenate([v32, zeros], axis=1), rhs_bottom], axis=0).astype(bf16)
            upd = _dot_tn(lhs, rhs)
            s_new = upd[:, HEAD:] * s + upd[:, :HEAD]
            if h < H_A:
                sa_ref[bi, h] = s_new[_own_keys(h), :]
            else:
                sb_ref[bi, h - H_A] = s_new

    for bi in range(bt):
        rows = slice(bi * seq, (bi + 1) * seq)
        for h in range(N_HEADS):
            sl = slice(h * HEAD, (h + 1) * HEAD)
            att = att_ref[bi * N_HEADS + h].astype(bf16)
            o = oi_ref[rows, sl] + jnp.dot(att, v_ref[rows, sl].astype(bf16),
                                           preferred_element_type=f32)
            o_ref[rows, sl] = (_head_norm(o, onorm[h]) * _silu(g_ref[rows, sl])).astype(
                o_ref.dtype)


def _mixer_sample(acts, sa_in, sb_in, ona, onb, batch, seq):
    bt = SAMPLE_SEQS
    kern = functools.partial(_mixer_sample_kernel, bt=bt, seq=seq)

    def rows(width):
        return pl.BlockSpec((bt * seq, width), lambda i: (i, 0))

    sa_spec = pl.BlockSpec((bt, H_A, DK_A, HEAD), lambda i: (i, 0, 0, 0))
    sb_spec = pl.BlockSpec((bt, H_A, HEAD, HEAD), lambda i: (i, 0, 0, 0))
    return pl.pallas_call(
        kern,
        out_shape=(
            jax.ShapeDtypeStruct((batch * seq, D_MODEL), f32),
            jax.ShapeDtypeStruct((batch, H_A, DK_A, HEAD), f32),
            jax.ShapeDtypeStruct((batch, H_A, HEAD, HEAD), f32),
        ),
        grid=(batch // bt,),
        in_specs=[
            rows(D_MODEL), rows(K_W), rows(K_W), rows(D_MODEL), rows(D_MODEL),
            pl.BlockSpec((bt, K_W), lambda i: (i, 0)),
            sa_spec, sb_spec,
            _const_spec(ona.shape),
            _const_spec(onb.shape),
        ],
        out_specs=(rows(D_MODEL), sa_spec, sb_spec),
        scratch_shapes=[
            pltpu.VMEM((bt * N_HEADS, seq, seq), f32),
            pltpu.VMEM((bt * seq, D_MODEL), f32),
        ],
        compiler_params=pltpu.CompilerParams(
            dimension_semantics=("arbitrary",), vmem_limit_bytes=VMEM_LIMIT),
        name="mixer_sample",
    )(*acts, sa_in, sb_in, ona, onb)


def _out_kernel(o_ref, x_ref, w_o_ref, gf_ref, w1_ref, w3_ref, w2_ref, gl_ref, y_ref):
    x1 = x_ref[...] + jnp.dot(o_ref[...].astype(bf16), w_o_ref[...], preferred_element_type=f32)
    h = _rmsnorm(x1, gf_ref[...]).astype(bf16)
    acc = x1
    for c in range(D_FF // FF_CHUNK):
        cols = slice(c * FF_CHUNK, (c + 1) * FF_CHUNK)
        a = jnp.dot(h, w1_ref[:, cols], preferred_element_type=f32)
        b = jnp.dot(h, w3_ref[:, cols], preferred_element_type=f32)
        u = (_silu(a) * b).astype(bf16)
        acc = acc + jnp.dot(u, w2_ref[cols, :], preferred_element_type=f32)
    y_ref[...] = _rmsnorm(acc, gl_ref[...])


def _out(o2d, x2d, w_o, g_ffn, w1, w3, w2, g_final):
    n = x2d.shape[0]
    row_spec = pl.BlockSpec((ROW_TILE, D_MODEL), lambda i: (i, 0))
    return pl.pallas_call(
        _out_kernel,
        out_shape=jax.ShapeDtypeStruct((n, D_MODEL), f32),
        grid=(n // ROW_TILE,),
        in_specs=[
            row_spec, row_spec,
            _const_spec((D_MODEL, D_MODEL)),
            _const_spec((1, D_MODEL)),
            _const_spec((D_MODEL, D_FF)),
            _const_spec((D_MODEL, D_FF)),
            _const_spec((D_FF, D_MODEL)),
            _const_spec((1, D_MODEL)),
        ],
        out_specs=row_spec,
        compiler_params=pltpu.CompilerParams(
            dimension_semantics=("arbitrary",), vmem_limit_bytes=VMEM_LIMIT),
        name="out_ffn",
    )(o2d, x2d, w_o, g_ffn, w1, w3, w2, g_final)


def _w_in_sections():
    sections, src = [], 0
    for width, dst in ((KA_W, OFF_QA), (KA_W, OFF_KA), (SEC, OFF_VA), (SEC, OFF_GA),
                       (GATE_RANK, OFF_LR), (SEC, OFF_QB), (SEC, OFF_FB), (SEC, OFF_IB),
                       (SEC, OFF_GB)):
        sections.append((src, width, dst))
        src += width
    return tuple(sections), src


_W_IN_SECTIONS, D_IN = _w_in_sections()


def _w_in_layout_kernel(wt_ref, o_ref):
    for src, width, dst in _W_IN_SECTIONS:
        part = wt_ref[src:src + width, :]
        if width % HEAD:
            pad = jnp.zeros((HEAD - width % HEAD, part.shape[1]), part.dtype)
            part = jnp.concatenate([part, pad], axis=0)
        o_ref[:, dst:dst + part.shape[0]] = part.T.astype(o_ref.dtype)


def _layout_w_in(w_in_t, cols=256):
    return pl.pallas_call(
        _w_in_layout_kernel,
        out_shape=jax.ShapeDtypeStruct((D_MODEL, D_P), bf16),
        grid=(D_MODEL // cols,),
        in_specs=[pl.BlockSpec((D_IN, cols), lambda i: (0, i))],
        out_specs=pl.BlockSpec((cols, D_P), lambda i: (i, 0)),
        compiler_params=pltpu.CompilerParams(
            dimension_semantics=("arbitrary",), vmem_limit_bytes=VMEM_LIMIT),
        name="w_in_layout",
    )(w_in_t)


def kernel(x_prompt, x_sample, state_gla, state_hgrn, norm_mix, w_in, w_a2, b_a, lb_param,
           onorm_a, onorm_b, w_o, norm_ffn, w1, w3, w2, norm_final):
    batch, seq, _ = x_prompt.shape
    dec_batch, dec_seq, _ = x_sample.shape

    w_in_p = _layout_w_in(jnp.swapaxes(w_in[0], 0, 1))
    w_a2p = jnp.pad(w_a2[0], ((0, HEAD - GATE_RANK), (0, 0))).astype(bf16)
    b_ap = b_a[0].reshape(1, KA_W)
    g_mix = norm_mix[0].reshape(1, D_MODEL)
    g_ffn = norm_ffn[0].reshape(1, D_MODEL)
    g_fin = norm_final.reshape(1, D_MODEL)
    ona = onorm_a[0].reshape(1, HEAD)
    onb = onorm_b[0].reshape(1, HEAD)

    xp = x_prompt.reshape(batch * seq, D_MODEL)
    xs = x_sample.reshape(dec_batch * dec_seq, D_MODEL)

    later = (w_o[0], w1[0], w3[0], w2[0])
    *acts_p, w_o_b, w1_b, w3_b, w2_b = _proj(xp, g_mix, w_in_p, w_a2p, b_ap, lb_param,
                                             chunk=PROMPT_CHUNK, paired=True, cast=later)
    op, sa_p, sb_p = _mixer_prompt(acts_p, ona, onb, batch, seq)
    yp = _out(op, xp, w_o_b, g_ffn, w1_b, w3_b, w2_b, g_fin)

    q_s, kin_s, ko_s, d_s, v_s, g_s = _proj(xs, g_mix, w_in_p, w_a2p, b_ap, lb_param,
                                            chunk=dec_seq, paired=False)
    os_, sa_s, sb_s = _mixer_sample((q_s, kin_s, ko_s, v_s, g_s, d_s), state_gla[0],
                                    state_hgrn[0], ona, onb, dec_batch, dec_seq)
    ys = _out(os_, xs, w_o_b, g_ffn, w1_b, w3_b, w2_b, g_fin)

    return (yp.reshape(batch, seq, D_MODEL), ys.reshape(dec_batch, dec_seq, D_MODEL),
            sa_p[None], sb_p[None], sa_s[None], sb_s[None])
```

```python
import functools

import jax
import jax.numpy as jnp
from jax import lax
from jax.experimental import pallas as pl
from jax.experimental.pallas import tpu as pltpu

D_MODEL = 1024
N_HEADS = 8
H_A = 4
DK_A = 64
HEAD = 128
GATE_RANK = 16
GATE_NORM = 16.0
D_FF = 2816
FF_CHUNK = 256
EPS = 1e-6
PROMPT_CHUNK = 32
SUBLANES = 8
BF16_ROWS = 16

KA_W = H_A * DK_A
SEC = H_A * HEAD
K_W = KA_W + SEC
OFF_LR = 0
OFF_QA = HEAD
OFF_KA = OFF_QA + KA_W
OFF_VA = OFF_KA + KA_W
OFF_GA = OFF_VA + SEC
OFF_FB = OFF_GA + SEC
OFF_QB = OFF_FB + SEC
OFF_IB = OFF_QB + SEC
OFF_GB = OFF_IB + SEC
D_P = OFF_GB + SEC
MXU_N = 256
PROJ_GROUPS = ((OFF_LR, OFF_VA), (OFF_FB, OFF_IB), (OFF_VA, OFF_FB), (OFF_IB, D_P))

ROW_TILE = 512
OUT_TILE = 1024
SAMPLE_SEQS = 16
VMEM_LIMIT = 56 * 1024 * 1024

f32 = jnp.float32
bf16 = jnp.bfloat16


def _const_spec(shape):
    nd = len(shape)
    return pl.BlockSpec(shape, lambda *_: (0,) * nd, pipeline_mode=pl.Buffered(1))


def _rmsnorm(x, g):
    return x * lax.rsqrt(jnp.mean(x * x, axis=-1, keepdims=True) + EPS) * g


def _sigmoid(x):
    return 0.5 * jnp.tanh(0.5 * x) + 0.5


def _silu(x):
    hx = 0.5 * x
    return hx + hx * jnp.tanh(hx)


def _dot_nt(a, b):
    return lax.dot_general(a, b, (((1,), (1,)), ((), ())), preferred_element_type=f32)


def _dot_tn(a, b):
    return lax.dot_general(a, b, (((0,), (0,)), ((), ())), preferred_element_type=f32)


def _key_lanes(h):
    return HEAD * (h // 2) if h < H_A else KA_W + HEAD * (h - H_A)


_KEY_SHARING = tuple((h, h + 1) for h in range(0, H_A, 2)) + tuple(
    (h,) for h in range(H_A, N_HEADS))


def _own_keys(h):
    return slice(DK_A * (h % 2), DK_A * (h % 2 + 1)) if h < H_A else slice(0, HEAD)


def _cumsum_rows(x, chunk):
    rows, width = x.shape
    row = lax.broadcasted_iota(jnp.int32, (SUBLANES, width), 0)
    groups_per_chunk = chunk // SUBLANES
    out = []
    carry = None
    for g in range(rows // SUBLANES):
        y = x[g * SUBLANES:(g + 1) * SUBLANES]
        for s in (1, 2, 4):
            y = y + jnp.where(row >= s, pltpu.roll(y, s, 0), 0.0)
        if g % groups_per_chunk != 0:
            y = y + carry
        carry = y[SUBLANES - 1:SUBLANES]
        out.append(y)
    return out[0] if len(out) == 1 else jnp.concatenate(out, axis=0)


def _lower_bound(lbp_ref):
    lbp = lbp_ref[...]
    m = jnp.max(lbp, axis=0, keepdims=True)
    e = jnp.exp(lbp - m)
    return e[0:1] / jnp.sum(e, axis=0, keepdims=True)


def _decays(la, chunk):
    rows = la.shape[0]
    b = _cumsum_rows(la, chunk)
    n_chunks = rows // chunk
    lasts = [b[(c + 1) * chunk - 1:(c + 1) * chunk] for c in range(n_chunks)]
    if n_chunks == 1:
        b_last = lasts[0]
        b_last_rows = b_last
    else:
        b_last = jnp.concatenate(lasts, axis=0)
        b_last_rows = jnp.concatenate(
            [jnp.broadcast_to(l, (chunk, l.shape[1])) for l in lasts], axis=0)
    return jnp.exp(b), jnp.exp(-b), jnp.exp(b_last_rows - b), jnp.exp(b_last)


def _emit_keys_paired(refs, r, q_targets, kl, q, k, la):
    q32_ref, q64_ref, kin_ref, kx_ref, ko_ref, d_ref = refs
    c = PROMPT_CHUNK
    rows = slice(2 * c * r, 2 * c * (r + 1))
    e_b, e_nb, e_out, decay = _decays(la, c)
    d0, d1 = decay[0:1], decay[1:2]
    q32 = q * e_b
    q64 = jnp.concatenate([q32[:c], q32[c:] * d0], axis=0)
    k_in = k * e_nb
    k_out = k * e_out
    for ql, own in q_targets:
        q32_ref[rows, ql] = (q32 if own is None else jnp.where(own, q32, 0.0)).astype(bf16)
        q64_ref[rows, ql] = (q64 if own is None else jnp.where(own, q64, 0.0)).astype(bf16)
    kin_ref[c * r:c * (r + 1), kl] = k_in[:c].astype(bf16)
    kx_ref[rows, kl] = jnp.concatenate([k_out[:c], k_in[c:]], axis=0).astype(bf16)
    ko_ref[rows, kl] = jnp.concatenate([k_out[:c] * d1, k_out[c:]], axis=0).astype(bf16)
    d_ref[r:r + 1, kl] = d0 * d1


def _emit_keys_single(refs, r, q_targets, kl, q, k, la, *, chunk):
    q_ref, kin_ref, ko_ref, d_ref = refs
    rows = slice(SUBLANES * chunk * r, SUBLANES * chunk * (r + 1))
    e_b, e_nb, e_out, decay = _decays(la, chunk)
    q_in = q * e_b
    for ql, own in q_targets:
        q_ref[rows, ql] = q_in if own is None else jnp.where(own, q_in, 0.0)
    kin_ref[rows, kl] = k * e_nb
    ko_ref[rows, kl] = k * e_out
    d_ref[SUBLANES * r:SUBLANES * (r + 1), kl] = decay


def _proj_kernel(x_ref, gm_ref, w_ref, w_a2_ref, b_a_ref, lbp_ref, *refs, chunk, paired, n_cast):
    cast_in, refs = refs[:n_cast], refs[n_cast:]
    n_key_refs = 6 if paired else 4
    key_refs = refs[:n_key_refs]
    v_ref, g_ref = refs[n_key_refs:n_key_refs + 2]
    cast_out = refs[n_key_refs + 2:n_key_refs + 2 + n_cast]
    p_refs = refs[n_key_refs + 2 + n_cast:]
    tm = x_ref.shape[0]
    block = 2 * chunk if paired else SUBLANES * chunk
    emit_keys = (_emit_keys_paired if paired
                 else functools.partial(_emit_keys_single, chunk=chunk))
    blocks = [(slice(r * block, (r + 1) * block), r) for r in range(tm // block)]
    h = _rmsnorm(x_ref[...], gm_ref[...]).astype(bf16)

    w_a2 = w_a2_ref[...]
    b_a = b_a_ref[...]
    lb = _lower_bound(lbp_ref)
    lane = lax.broadcasted_iota(jnp.int32, (block, HEAD), 1)

    def p_slot(lo, hi):
        for (g_lo, g_hi), ref in zip(PROJ_GROUPS, p_refs):
            if g_lo <= lo and hi <= g_hi:
                return ref, slice(lo - g_lo, hi - g_lo)
        raise ValueError((lo, hi))

    def project(lo, hi):
        ref, cols = p_slot(lo, hi)
        ref[:, cols] = jnp.dot(h, w_ref[:, lo:hi], preferred_element_type=f32)

    def p(rows, lo, width):
        ref, cols = p_slot(lo, lo + width)
        return ref[rows, cols]

    def gla_keys(rows, r):
        lra = p(rows, OFF_LR, HEAD)
        z = jnp.dot(lra.astype(bf16), w_a2, preferred_element_type=f32) + b_a
        for pr in range(H_A // 2):
            kl = slice(pr * HEAD, (pr + 1) * HEAD)
            zz = z[:, kl]
            la = (jnp.minimum(zz, 0.0) - jnp.log(1.0 + jnp.exp(-jnp.abs(zz)))) * (1.0 / GATE_NORM)
            q = p(rows, OFF_QA + pr * HEAD, HEAD) * (DK_A ** -0.5)
            k = p(rows, OFF_KA + pr * HEAD, HEAD)
            q_targets = [(slice(hd * HEAD, (hd + 1) * HEAD), own)
                         for hd, own in ((2 * pr, lane < DK_A), (2 * pr + 1, lane >= DK_A))]
            emit_keys(key_refs, r, q_targets, kl, q, k, la)

    def hgrn_keys(rows, r):
        for hd in range(N_HEADS - H_A):
            lbh = lb[:, hd * HEAD:(hd + 1) * HEAD]
            f = lbh + (1.0 - lbh) * _sigmoid(p(rows, OFF_FB + hd * HEAD, HEAD))
            q = _silu(p(rows, OFF_QB + hd * HEAD, HEAD))
            ql = slice(SEC + hd * HEAD, SEC + (hd + 1) * HEAD)
            kl = slice(KA_W + hd * HEAD, KA_W + (hd + 1) * HEAD)
            emit_keys(key_refs, r, [(ql, None)], kl, q, 1.0 - f, jnp.log(f))

    def gla_values(rows, crow):
        v_ref[rows, 0:SEC] = p(rows, OFF_VA, SEC).astype(v_ref.dtype)
        g_ref[rows, 0:SEC] = p(rows, OFF_GA, SEC)

    def hgrn_values(rows, crow):
        v_ref[rows, SEC:2 * SEC] = p(rows, OFF_IB, SEC).astype(v_ref.dtype)
        g_ref[rows, SEC:2 * SEC] = p(rows, OFF_GB, SEC)

    def tasks(fn, which):
        return [functools.partial(fn, *blocks[r]) for r in which]

    def cast(src, dst):
        dst[...] = src[...].astype(dst.dtype)

    nb = len(blocks)
    first, second = range(nb // 2), range(nb // 2, nb)
    windows = (
        [functools.partial(cast, s, d) for s, d in zip(cast_in, cast_out)],
        tasks(gla_keys, range(nb)),
        tasks(hgrn_keys, first),
        tasks(hgrn_keys, second) + tasks(gla_values, range(nb)),
    )
    for (lo, hi), work in zip(PROJ_GROUPS, windows):
        starts = list(range(lo, hi, MXU_N))
        for i, c0 in enumerate(starts):
            project(c0, min(c0 + MXU_N, hi))
            for t in work[i * len(work) // len(starts):(i + 1) * len(work) // len(starts)]:
                t()
    for t in tasks(hgrn_values, range(nb)):
        t()


def _slice_spec(shape, steps):
    rows, width = shape
    height = next(r for r in range(BF16_ROWS, rows + 1, BF16_ROWS)
                  if rows % r == 0 and rows // r <= steps)
    last = rows // height - 1
    return pl.BlockSpec((height, width), lambda i: (jnp.minimum(i, last), 0))


def _proj(x2d, g_mix, w_in_p, w_a2p, b_ap, lb_param, *, chunk, paired, cast=()):
    n = x2d.shape[0]
    tm = ROW_TILE
    steps = n // tm
    kern = functools.partial(_proj_kernel, chunk=chunk, paired=paired, n_cast=len(cast))

    def arr(rows, width, dtype):
        return (jax.ShapeDtypeStruct((n // rows, width), dtype),
                pl.BlockSpec((tm // rows, width), lambda i: (i, 0)))

    if paired:
        keys = [arr(1, D_MODEL, bf16), arr(1, D_MODEL, bf16), arr(2, K_W, bf16),
                arr(1, K_W, bf16), arr(1, K_W, bf16), arr(2 * chunk, K_W, f32)]
        v_out = arr(1, D_MODEL, bf16)
    else:
        keys = [arr(1, D_MODEL, f32), arr(1, K_W, f32), arr(1, K_W, f32), arr(chunk, K_W, f32)]
        v_out = arr(1, D_MODEL, f32)
    outs = keys + [v_out, arr(1, D_MODEL, f32)]
    cast_specs = [_slice_spec(w.shape, steps) for w in cast]
    outs += [(jax.ShapeDtypeStruct(w.shape, bf16), spec) for w, spec in zip(cast, cast_specs)]

    return pl.pallas_call(
        kern,
        out_shape=tuple(o[0] for o in outs),
        grid=(steps,),
        in_specs=[
            pl.BlockSpec((tm, D_MODEL), lambda i: (i, 0)),
            _const_spec((1, D_MODEL)),
            _const_spec((D_MODEL, D_P)),
            _const_spec(w_a2p.shape),
            _const_spec(b_ap.shape),
            _const_spec(lb_param.shape),
        ] + cast_specs,
        out_specs=tuple(o[1] for o in outs),
        scratch_shapes=[pltpu.VMEM((tm, hi - lo), f32) for lo, hi in PROJ_GROUPS],
        compiler_params=pltpu.CompilerParams(
            dimension_semantics=("arbitrary",), vmem_limit_bytes=VMEM_LIMIT),
        name="proj",
    )(x2d, g_mix, w_in_p, w_a2p, b_ap, lb_param, *cast)


def _head_norm(o, onorm):
    return o * lax.rsqrt(jnp.mean(o * o, axis=-1, keepdims=True) + EPS) * onorm


def _tril(n):
    r = lax.broadcasted_iota(jnp.int32, (n, n), 0)
    c = lax.broadcasted_iota(jnp.int32, (n, n), 1)
    return r >= c


def _mixer_prompt_kernel(q32_ref, q64_ref, kin_ref, kx_ref, ko_ref, d_ref, v_ref, g_ref,
                         ona_ref, onb_ref, o_ref, sa_ref, sb_ref,
                         st_ref, att_ref, up_ref, start_ref, *, tile, chunk):
    j = pl.program_id(1)

    @pl.when(j == 0)
    def _():
        st_ref[...] = jnp.zeros_like(st_ref)

    onorm = [ona_ref[...]] * H_A + [onb_ref[...]] * (N_HEADS - H_A)
    pair = 2 * chunk
    n_pairs = tile // pair
    stack = max(len(heads) for heads in _KEY_SHARING) * chunk
    r = lax.broadcasted_iota(jnp.int32, (stack, pair), 0) % chunk
    c = lax.broadcasted_iota(jnp.int32, (stack, pair), 1)
    tril = (r >= c)[:, :chunk]
    cross = (c < chunk) | (r >= c - chunk)

    for heads in _KEY_SHARING:
        n = len(heads)
        kl = slice(_key_lanes(heads[0]), _key_lanes(heads[0]) + HEAD)
        vl = slice(heads[0] * HEAD, (heads[-1] + 1) * HEAD)
        for s in range(n_pairs):
            ra = slice(s * pair, s * pair + chunk)
            rb = slice(s * pair + chunk, (s + 1) * pair)
            rp = slice(s * pair, (s + 1) * pair)
            qa = [q32_ref[ra, h * HEAD:(h + 1) * HEAD] for h in heads]
            qb = [q32_ref[rb, h * HEAD:(h + 1) * HEAD] for h in heads]
            qa = qa[0] if n == 1 else jnp.concatenate(qa, axis=0)
            qb = qb[0] if n == 1 else jnp.concatenate(qb, axis=0)
            a_keys = kin_ref[s * chunk:(s + 1) * chunk, kl]
            att0 = jnp.where(tril[:n * chunk], _dot_nt(qa, a_keys), 0.0).astype(bf16)
            att1 = jnp.where(cross[:n * chunk], _dot_nt(qb, kx_ref[rp, kl]), 0.0).astype(bf16)
            up = _dot_tn(v_ref[rp, vl], ko_ref[rp, kl])
            att0 = jnp.concatenate([att0, jnp.zeros_like(att0)], axis=1)
            for i, h in enumerate(heads):
                u = h * n_pairs + s
                att_ref[u, :chunk] = att0[i * chunk:(i + 1) * chunk]
                att_ref[u, chunk:] = att1[i * chunk:(i + 1) * chunk]
                up_ref[u] = up[i * HEAD:(i + 1) * HEAD]

    for h in range(N_HEADS):
        kl = slice(_key_lanes(h), _key_lanes(h) + HEAD)
        st = st_ref[h]
        for s in range(n_pairs):
            u = h * n_pairs + s
            start_ref[u] = st.T.astype(bf16)
            st = d_ref[s:s + 1, kl] * st + up_ref[u]
        st_ref[h] = st

    for h in range(N_HEADS):
        sl = slice(h * HEAD, (h + 1) * HEAD)
        outs = []
        for s in range(n_pairs):
            u = h * n_pairs + s
            rp = slice(s * pair, (s + 1) * pair)
            outs.append(jnp.dot(q64_ref[rp, sl], start_ref[u], preferred_element_type=f32)
                        + jnp.dot(att_ref[u], v_ref[rp, sl], preferred_element_type=f32))
        o = jnp.concatenate(outs, axis=0)
        o_ref[:, sl] = (_head_norm(o, onorm[h]) * _silu(g_ref[:, sl])).astype(o_ref.dtype)

    @pl.when(j == pl.num_programs(1) - 1)
    def _():
        for h in range(H_A):
            sa_ref[0, h] = st_ref[h].T[_own_keys(h), :]
            sb_ref[0, h] = st_ref[H_A + h].T


def _mixer_prompt(acts, ona, onb, batch, seq):
    tile = ROW_TILE
    nt = seq // tile
    chunk = PROMPT_CHUNK
    units = N_HEADS * (tile // (2 * chunk))
    kern = functools.partial(_mixer_prompt_kernel, tile=tile, chunk=chunk)

    def rows(width, per=1):
        return pl.BlockSpec((tile // per, width), lambda b, j: (b * nt + j, 0))

    return pl.pallas_call(
        kern,
        out_shape=(
            jax.ShapeDtypeStruct((batch * seq, D_MODEL), bf16),
            jax.ShapeDtypeStruct((batch, H_A, DK_A, HEAD), f32),
            jax.ShapeDtypeStruct((batch, H_A, HEAD, HEAD), f32),
        ),
        grid=(batch, nt),
        in_specs=[
            rows(D_MODEL), rows(D_MODEL), rows(K_W, 2), rows(K_W), rows(K_W),
            rows(K_W, 2 * chunk), rows(D_MODEL), rows(D_MODEL),
            _const_spec(ona.shape),
            _const_spec(onb.shape),
        ],
        out_specs=(
            rows(D_MODEL),
            pl.BlockSpec((1, H_A, DK_A, HEAD), lambda b, j: (b, 0, 0, 0)),
            pl.BlockSpec((1, H_A, HEAD, HEAD), lambda b, j: (b, 0, 0, 0)),
        ),
        scratch_shapes=[
            pltpu.VMEM((N_HEADS, HEAD, HEAD), f32),
            pltpu.VMEM((units, 2 * chunk, 2 * chunk), bf16),
            pltpu.VMEM((units, HEAD, HEAD), f32),
            pltpu.VMEM((units, HEAD, HEAD), bf16),
        ],
        compiler_params=pltpu.CompilerParams(
            dimension_semantics=("arbitrary", "arbitrary"), vmem_limit_bytes=VMEM_LIMIT),
        name="mixer_prompt",
    )(*acts, ona, onb)


def _mixer_sample_kernel(qi_ref, ki_ref, ko_ref, v_ref, g_ref, d_ref, sa_in_ref, sb_in_ref,
                         ona_ref, onb_ref, o_ref, sa_ref, sb_ref, att_ref, oi_ref, *, bt, seq):
    onorm = [ona_ref[...]] * H_A + [onb_ref[...]] * (N_HEADS - H_A)
    tril = _tril(seq)
    zpad = jnp.zeros((HEAD - DK_A, HEAD), f32)
    zeros = jnp.zeros((seq, HEAD), f32)
    row = lax.broadcasted_iota(jnp.int32, (seq, HEAD), 0)
    ones_rows = jnp.where(row < 2, 1.0, 0.0)
    rhs_bottom = jnp.concatenate([zeros, ones_rows], axis=1)

    for bi in range(bt):
        rows = slice(bi * seq, (bi + 1) * seq)
        decay = d_ref[bi:bi + 1, :]
        d_hi = decay.astype(bf16).astype(f32)
        d_lo = decay - d_hi
        for h in range(N_HEADS):
            sl = slice(h * HEAD, (h + 1) * HEAD)
            kl = slice(_key_lanes(h), _key_lanes(h) + HEAD)
            q_in = qi_ref[rows, sl].astype(bf16)
            k_in = ki_ref[rows, kl].astype(bf16)
            v32 = v_ref[rows, sl]
            if h >= H_A:
                s = sb_in_ref[bi, h - H_A]
            elif h % 2 == 0:
                s = jnp.concatenate([sa_in_ref[bi, h], zpad], axis=0)
            else:
                s = jnp.concatenate([zpad, sa_in_ref[bi, h]], axis=0)
            att_ref[bi * N_HEADS + h] = jnp.where(tril, _dot_nt(q_in, k_in), 0.0)
            oi_ref[rows, sl] = jnp.dot(q_in, s.astype(bf16), preferred_element_type=f32)
            d_rows = jnp.where(row == 0, d_hi[:, kl], jnp.where(row == 1, d_lo[:, kl], 0.0))
            lhs = jnp.concatenate([ko_ref[rows, kl], d_rows], axis=0).astype(bf16)
            rhs = jnp.concatenate(
                [jnp.concatenate([v32, zeros], axis=1), rhs_bottom], axis=0).astype(bf16)
            upd = _dot_tn(lhs, rhs)
            s_new = upd[:, HEAD:] * s + upd[:, :HEAD]
            if h < H_A:
                sa_ref[bi, h] = s_new[_own_keys(h), :]
            else:
                sb_ref[bi, h - H_A] = s_new

    for bi in range(bt):
        rows = slice(bi * seq, (bi + 1) * seq)
        for h in range(N_HEADS):
            sl = slice(h * HEAD, (h + 1) * HEAD)
            att = att_ref[bi * N_HEADS + h].astype(bf16)
            o = oi_ref[rows, sl] + jnp.dot(att, v_ref[rows, sl].astype(bf16),
                                           preferred_element_type=f32)
            o_ref[rows, sl] = (_head_norm(o, onorm[h]) * _silu(g_ref[rows, sl])).astype(
                o_ref.dtype)


def _mixer_sample(acts, sa_in, sb_in, ona, onb, batch, seq):
    bt = SAMPLE_SEQS
    kern = functools.partial(_mixer_sample_kernel, bt=bt, seq=seq)

    def rows(width):
        return pl.BlockSpec((bt * seq, width), lambda i: (i, 0))

    sa_spec = pl.BlockSpec((bt, H_A, DK_A, HEAD), lambda i: (i, 0, 0, 0))
    sb_spec = pl.BlockSpec((bt, H_A, HEAD, HEAD), lambda i: (i, 0, 0, 0))
    return pl.pallas_call(
        kern,
        out_shape=(
            jax.ShapeDtypeStruct((batch * seq, D_MODEL), f32),
            jax.ShapeDtypeStruct((batch, H_A, DK_A, HEAD), f32),
            jax.ShapeDtypeStruct((batch, H_A, HEAD, HEAD), f32),
        ),
        grid=(batch // bt,),
        in_specs=[
            rows(D_MODEL), rows(K_W), rows(K_W), rows(D_MODEL), rows(D_MODEL),
            pl.BlockSpec((bt, K_W), lambda i: (i, 0)),
            sa_spec, sb_spec,
            _const_spec(ona.shape),
            _const_spec(onb.shape),
        ],
        out_specs=(rows(D_MODEL), sa_spec, sb_spec),
        scratch_shapes=[
            pltpu.VMEM((bt * N_HEADS, seq, seq), f32),
            pltpu.VMEM((bt * seq, D_MODEL), f32),
        ],
        compiler_params=pltpu.CompilerParams(
            dimension_semantics=("arbitrary",), vmem_limit_bytes=VMEM_LIMIT),
        name="mixer_sample",
    )(*acts, sa_in, sb_in, ona, onb)


def _out_kernel(o_ref, x_ref, w_o_ref, gf_ref, w1_ref, w3_ref, w2_ref, gl_ref, y_ref):
    x1 = x_ref[...] + jnp.dot(o_ref[...].astype(bf16), w_o_ref[...], preferred_element_type=f32)
    h = _rmsnorm(x1, gf_ref[...]).astype(bf16)
    acc = x1
    for c in range(D_FF // FF_CHUNK):
        cols = slice(c * FF_CHUNK, (c + 1) * FF_CHUNK)
        a = jnp.dot(h, w1_ref[:, cols], preferred_element_type=f32)
        b = jnp.dot(h, w3_ref[:, cols], preferred_element_type=f32)
        u = (_silu(a) * b).astype(bf16)
        acc = acc + jnp.dot(u, w2_ref[cols, :], preferred_element_type=f32)
    y_ref[...] = _rmsnorm(acc, gl_ref[...])


def _out(o2d, x2d, w_o, g_ffn, w1, w3, w2, g_final):
    n = x2d.shape[0]
    row_spec = pl.BlockSpec((OUT_TILE, D_MODEL), lambda i: (i, 0))
    return pl.pallas_call(
        _out_kernel,
        out_shape=jax.ShapeDtypeStruct((n, D_MODEL), f32),
        grid=(n // OUT_TILE,),
        in_specs=[
            row_spec, row_spec,
            _const_spec((D_MODEL, D_MODEL)),
            _const_spec((1, D_MODEL)),
            _const_spec((D_MODEL, D_FF)),
            _const_spec((D_MODEL, D_FF)),
            _const_spec((D_FF, D_MODEL)),
            _const_spec((1, D_MODEL)),
        ],
        out_specs=row_spec,
        compiler_params=pltpu.CompilerParams(
            dimension_semantics=("arbitrary",), vmem_limit_bytes=VMEM_LIMIT),
        name="out_ffn",
    )(o2d, x2d, w_o, g_ffn, w1, w3, w2, g_final)


def _w_in_sections():
    sections, src = [], 0
    for width, dst in ((KA_W, OFF_QA), (KA_W, OFF_KA), (SEC, OFF_VA), (SEC, OFF_GA),
                       (GATE_RANK, OFF_LR), (SEC, OFF_QB), (SEC, OFF_FB), (SEC, OFF_IB),
                       (SEC, OFF_GB)):
        sections.append((src, width, dst))
        src += width
    return tuple(sections), src


_W_IN_SECTIONS, D_IN = _w_in_sections()


def _w_in_layout_kernel(wt_ref, o_ref):
    for src, width, dst in _W_IN_SECTIONS:
        part = wt_ref[src:src + width, :]
        if width % HEAD:
            pad = jnp.zeros((HEAD - width % HEAD, part.shape[1]), part.dtype)
            part = jnp.concatenate([part, pad], axis=0)
        o_ref[:, dst:dst + part.shape[0]] = part.T.astype(o_ref.dtype)


def _layout_w_in(w_in_t, cols=256):
    return pl.pallas_call(
        _w_in_layout_kernel,
        out_shape=jax.ShapeDtypeStruct((D_MODEL, D_P), bf16),
        grid=(D_MODEL // cols,),
        in_specs=[pl.BlockSpec((D_IN, cols), lambda i: (0, i))],
        out_specs=pl.BlockSpec((cols, D_P), lambda i: (i, 0)),
        compiler_params=pltpu.CompilerParams(
            dimension_semantics=("arbitrary",), vmem_limit_bytes=VMEM_LIMIT),
        name="w_in_layout",
    )(w_in_t)


def kernel(x_prompt, x_sample, state_gla, state_hgrn, norm_mix, w_in, w_a2, b_a, lb_param,
           onorm_a, onorm_b, w_o, norm_ffn, w1, w3, w2, norm_final):
    batch, seq, _ = x_prompt.shape
    dec_batch, dec_seq, _ = x_sample.shape

    w_in_p = _layout_w_in(jnp.swapaxes(w_in[0], 0, 1))
    w_a2p = jnp.pad(w_a2[0], ((0, HEAD - GATE_RANK), (0, 0))).astype(bf16)
    b_ap = b_a[0].reshape(1, KA_W)
    g_mix = norm_mix[0].reshape(1, D_MODEL)
    g_ffn = norm_ffn[0].reshape(1, D_MODEL)
    g_fin = norm_final.reshape(1, D_MODEL)
    ona = onorm_a[0].reshape(1, HEAD)
    onb = onorm_b[0].reshape(1, HEAD)

    xp = x_prompt.reshape(batch * seq, D_MODEL)
    xs = x_sample.reshape(dec_batch * dec_seq, D_MODEL)

    later = (w_o[0], w1[0], w3[0], w2[0])
    *acts_p, w_o_b, w1_b, w3_b, w2_b = _proj(xp, g_mix, w_in_p, w_a2p, b_ap, lb_param,
                                             chunk=PROMPT_CHUNK, paired=True, cast=later)
    op, sa_p, sb_p = _mixer_prompt(acts_p, ona, onb, batch, seq)
    yp = _out(op, xp, w_o_b, g_ffn, w1_b, w3_b, w2_b, g_fin)

    q_s, kin_s, ko_s, d_s, v_s, g_s = _proj(xs, g_mix, w_in_p, w_a2p, b_ap, lb_param,
                                            chunk=dec_seq, paired=False)
    os_, sa_s, sb_s = _mixer_sample((q_s, kin_s, ko_s, v_s, g_s, d_s), state_gla[0],
                                    state_hgrn[0], ona, onb, dec_batch, dec_seq)
    ys = _out(os_, xs, w_o_b, g_ffn, w1_b, w3_b, w2_b, g_fin)

    return (yp.reshape(batch, seq, D_MODEL), ys.reshape(dec_batch, dec_seq, D_MODEL),
            sa_p[None], sb_p[None], sa_s[None], sb_s[None])
```

```python
import functools

import jax
import jax.numpy as jnp
from jax import lax
from jax.experimental import pallas as pl
from jax.experimental.pallas import tpu as pltpu

D_MODEL = 1024
N_HEADS = 8
H_A = 4
DK_A = 64
HEAD = 128
GATE_RANK = 16
GATE_NORM = 16.0
D_FF = 2816
FF_CHUNK = 256
EPS = 1e-6
PROMPT_CHUNK = 32
SUBLANES = 8
BF16_ROWS = 16

KA_W = H_A * DK_A
SEC = H_A * HEAD
K_W = KA_W + SEC
OFF_LR = 0
OFF_QA = HEAD
OFF_KA = OFF_QA + KA_W
OFF_VA = OFF_KA + KA_W
OFF_GA = OFF_VA + SEC
OFF_FB = OFF_GA + SEC
OFF_QB = OFF_FB + SEC
OFF_IB = OFF_QB + SEC
OFF_GB = OFF_IB + SEC
D_P = OFF_GB + SEC
MXU_N = 256
PROJ_GROUPS = ((OFF_LR, OFF_VA), (OFF_FB, OFF_IB), (OFF_VA, OFF_FB), (OFF_IB, D_P))

ROW_TILE = 512
MIXER_TILE = 1024
OUT_TILE = 1024
SAMPLE_SEQS = 16
VMEM_LIMIT = 56 * 1024 * 1024

f32 = jnp.float32
bf16 = jnp.bfloat16


def _const_spec(shape):
    nd = len(shape)
    return pl.BlockSpec(shape, lambda *_: (0,) * nd, pipeline_mode=pl.Buffered(1))


def _rmsnorm(x, g):
    return x * lax.rsqrt(jnp.mean(x * x, axis=-1, keepdims=True) + EPS) * g


def _sigmoid(x):
    return 0.5 * jnp.tanh(0.5 * x) + 0.5


def _silu(x):
    hx = 0.5 * x
    return hx + hx * jnp.tanh(hx)


def _dot_nt(a, b):
    return lax.dot_general(a, b, (((1,), (1,)), ((), ())), preferred_element_type=f32)


def _dot_tn(a, b):
    return lax.dot_general(a, b, (((0,), (0,)), ((), ())), preferred_element_type=f32)


def _key_lanes(h):
    return HEAD * (h // 2) if h < H_A else KA_W + HEAD * (h - H_A)


_KEY_SHARING = tuple((h, h + 1) for h in range(0, H_A, 2)) + tuple(
    (h,) for h in range(H_A, N_HEADS))


def _own_keys(h):
    return slice(DK_A * (h % 2), DK_A * (h % 2 + 1)) if h < H_A else slice(0, HEAD)


def _cumsum_rows(x, chunk):
    rows, width = x.shape
    row = lax.broadcasted_iota(jnp.int32, (SUBLANES, width), 0)
    groups_per_chunk = chunk // SUBLANES
    out = []
    carry = None
    for g in range(rows // SUBLANES):
        y = x[g * SUBLANES:(g + 1) * SUBLANES]
        for s in (1, 2, 4):
            y = y + jnp.where(row >= s, pltpu.roll(y, s, 0), 0.0)
        if g % groups_per_chunk != 0:
            y = y + carry
        carry = y[SUBLANES - 1:SUBLANES]
        out.append(y)
    return out[0] if len(out) == 1 else jnp.concatenate(out, axis=0)


def _lower_bound(lbp_ref):
    lbp = lbp_ref[...]
    m = jnp.max(lbp, axis=0, keepdims=True)
    e = jnp.exp(lbp - m)
    return e[0:1] / jnp.sum(e, axis=0, keepdims=True)


def _decays(la, chunk):
    rows = la.shape[0]
    b = _cumsum_rows(la, chunk)
    n_chunks = rows // chunk
    lasts = [b[(c + 1) * chunk - 1:(c + 1) * chunk] for c in range(n_chunks)]
    if n_chunks == 1:
        b_last = lasts[0]
        b_last_rows = b_last
    else:
        b_last = jnp.concatenate(lasts, axis=0)
        b_last_rows = jnp.concatenate(
            [jnp.broadcast_to(l, (chunk, l.shape[1])) for l in lasts], axis=0)
    return jnp.exp(b), jnp.exp(-b), jnp.exp(b_last_rows - b), jnp.exp(b_last)


def _emit_keys_paired(refs, r, q_targets, kl, q, k, la):
    q32_ref, q64_ref, kin_ref, kx_ref, ko_ref, d_ref = refs
    c = PROMPT_CHUNK
    rows = slice(2 * c * r, 2 * c * (r + 1))
    e_b, e_nb, e_out, decay = _decays(la, c)
    d0, d1 = decay[0:1], decay[1:2]
    q32 = q * e_b
    k_in = k * e_nb
    k_out = k * e_out
    q32_ref[rows, kl] = q32.astype(bf16)
    q64_ref[c * r:c * (r + 1), kl] = (q32[c:] * d0).astype(bf16)
    kin_ref[c * r:c * (r + 1), kl] = k_in[:c].astype(bf16)
    kx_ref[rows, kl] = jnp.concatenate([k_out[:c], k_in[c:]], axis=0).astype(bf16)
    ko_ref[rows, kl] = jnp.concatenate([k_out[:c] * d1, k_out[c:]], axis=0).astype(bf16)
    d_ref[r:r + 1, kl] = d0 * d1


def _emit_keys_single(refs, r, q_targets, kl, q, k, la, *, chunk):
    q_ref, kin_ref, ko_ref, d_ref = refs
    rows = slice(SUBLANES * chunk * r, SUBLANES * chunk * (r + 1))
    e_b, e_nb, e_out, decay = _decays(la, chunk)
    q_in = q * e_b
    for ql, own in q_targets:
        q_ref[rows, ql] = q_in if own is None else jnp.where(own, q_in, 0.0)
    kin_ref[rows, kl] = k * e_nb
    ko_ref[rows, kl] = k * e_out
    d_ref[SUBLANES * r:SUBLANES * (r + 1), kl] = decay


def _proj_kernel(x_ref, gm_ref, w_ref, w_a2_ref, b_a_ref, lbp_ref, *refs, chunk, paired, n_cast):
    cast_in, refs = refs[:n_cast], refs[n_cast:]
    n_key_refs = 6 if paired else 4
    key_refs = refs[:n_key_refs]
    v_ref, g_ref = refs[n_key_refs:n_key_refs + 2]
    cast_out = refs[n_key_refs + 2:n_key_refs + 2 + n_cast]
    p_refs = refs[n_key_refs + 2 + n_cast:]
    tm = x_ref.shape[0]
    block = 2 * chunk if paired else SUBLANES * chunk
    emit_keys = (_emit_keys_paired if paired
                 else functools.partial(_emit_keys_single, chunk=chunk))
    blocks = [(slice(r * block, (r + 1) * block), r) for r in range(tm // block)]
    h = _rmsnorm(x_ref[...], gm_ref[...]).astype(bf16)

    w_a2 = w_a2_ref[...]
    b_a = b_a_ref[...]
    lb = _lower_bound(lbp_ref)
    lane = lax.broadcasted_iota(jnp.int32, (block, HEAD), 1)

    def p_slot(lo, hi):
        for (g_lo, g_hi), ref in zip(PROJ_GROUPS, p_refs):
            if g_lo <= lo and hi <= g_hi:
                return ref, slice(lo - g_lo, hi - g_lo)
        raise ValueError((lo, hi))

    def project(lo, hi):
        ref, cols = p_slot(lo, hi)
        ref[:, cols] = jnp.dot(h, w_ref[:, lo:hi], preferred_element_type=f32)

    def p(rows, lo, width):
        ref, cols = p_slot(lo, lo + width)
        return ref[rows, cols]

    def gla_keys(rows, r):
        lra = p(rows, OFF_LR, HEAD)
        z = jnp.dot(lra.astype(bf16), w_a2, preferred_element_type=f32) + b_a
        for pr in range(H_A // 2):
            kl = slice(pr * HEAD, (pr + 1) * HEAD)
            zz = z[:, kl]
            la = (jnp.minimum(zz, 0.0) - jnp.log(1.0 + jnp.exp(-jnp.abs(zz)))) * (1.0 / GATE_NORM)
            q = p(rows, OFF_QA + pr * HEAD, HEAD) * (DK_A ** -0.5)
            k = p(rows, OFF_KA + pr * HEAD, HEAD)
            q_targets = [(slice(hd * HEAD, (hd + 1) * HEAD), own)
                         for hd, own in ((2 * pr, lane < DK_A), (2 * pr + 1, lane >= DK_A))]
            emit_keys(key_refs, r, q_targets, kl, q, k, la)

    def hgrn_keys(rows, r):
        for hd in range(N_HEADS - H_A):
            lbh = lb[:, hd * HEAD:(hd + 1) * HEAD]
            f = lbh + (1.0 - lbh) * _sigmoid(p(rows, OFF_FB + hd * HEAD, HEAD))
            q = _silu(p(rows, OFF_QB + hd * HEAD, HEAD))
            ql = slice(SEC + hd * HEAD, SEC + (hd + 1) * HEAD)
            kl = slice(KA_W + hd * HEAD, KA_W + (hd + 1) * HEAD)
            emit_keys(key_refs, r, [(ql, None)], kl, q, 1.0 - f, jnp.log(f))

    def gla_values(rows, crow):
        v_ref[rows, 0:SEC] = p(rows, OFF_VA, SEC).astype(v_ref.dtype)
        g_ref[rows, 0:SEC] = p(rows, OFF_GA, SEC)

    def hgrn_values(rows, crow):
        v_ref[rows, SEC:2 * SEC] = p(rows, OFF_IB, SEC).astype(v_ref.dtype)
        g_ref[rows, SEC:2 * SEC] = p(rows, OFF_GB, SEC)

    def tasks(fn, which):
        return [functools.partial(fn, *blocks[r]) for r in which]

    def cast(src, dst):
        dst[...] = src[...].astype(dst.dtype)

    nb = len(blocks)
    first, second = range(nb // 2), range(nb // 2, nb)
    windows = (
        [functools.partial(cast, s, d) for s, d in zip(cast_in, cast_out)],
        tasks(gla_keys, range(nb)),
        tasks(hgrn_keys, first),
        tasks(hgrn_keys, second) + tasks(gla_values, range(nb)),
    )
    for (lo, hi), work in zip(PROJ_GROUPS, windows):
        starts = list(range(lo, hi, MXU_N))
        for i, c0 in enumerate(starts):
            project(c0, min(c0 + MXU_N, hi))
            for t in work[i * len(work) // len(starts):(i + 1) * len(work) // len(starts)]:
                t()
    for t in tasks(hgrn_values, range(nb)):
        t()


def _slice_spec(shape, steps):
    rows, width = shape
    height = next(r for r in range(BF16_ROWS, rows + 1, BF16_ROWS)
                  if rows % r == 0 and rows // r <= steps)
    last = rows // height - 1
    return pl.BlockSpec((height, width), lambda i: (jnp.minimum(i, last), 0))


def _proj(x2d, g_mix, w_in_p, w_a2p, b_ap, lb_param, *, chunk, paired, cast=()):
    n = x2d.shape[0]
    tm = ROW_TILE
    steps = n // tm
    kern = functools.partial(_proj_kernel, chunk=chunk, paired=paired, n_cast=len(cast))

    def arr(rows, width, dtype):
        return (jax.ShapeDtypeStruct((n // rows, width), dtype),
                pl.BlockSpec((tm // rows, width), lambda i: (i, 0)))

    if paired:
        keys = [arr(1, K_W, bf16), arr(2, K_W, bf16), arr(2, K_W, bf16),
                arr(1, K_W, bf16), arr(1, K_W, bf16), arr(2 * chunk, K_W, f32)]
        v_out = arr(1, D_MODEL, bf16)
    else:
        keys = [arr(1, D_MODEL, f32), arr(1, K_W, f32), arr(1, K_W, f32), arr(chunk, K_W, f32)]
        v_out = arr(1, D_MODEL, f32)
    outs = keys + [v_out, arr(1, D_MODEL, f32)]
    cast_specs = [_slice_spec(w.shape, steps) for w in cast]
    outs += [(jax.ShapeDtypeStruct(w.shape, bf16), spec) for w, spec in zip(cast, cast_specs)]

    return pl.pallas_call(
        kern,
        out_shape=tuple(o[0] for o in outs),
        grid=(steps,),
        in_specs=[
            pl.BlockSpec((tm, D_MODEL), lambda i: (i, 0)),
            _const_spec((1, D_MODEL)),
            _const_spec((D_MODEL, D_P)),
            _const_spec(w_a2p.shape),
            _const_spec(b_ap.shape),
            _const_spec(lb_param.shape),
        ] + cast_specs,
        out_specs=tuple(o[1] for o in outs),
        scratch_shapes=[pltpu.VMEM((tm, hi - lo), f32) for lo, hi in PROJ_GROUPS],
        compiler_params=pltpu.CompilerParams(
            dimension_semantics=("arbitrary",), vmem_limit_bytes=VMEM_LIMIT),
        name="proj",
    )(x2d, g_mix, w_in_p, w_a2p, b_ap, lb_param, *cast)


def _head_norm(o, onorm):
    return o * lax.rsqrt(jnp.mean(o * o, axis=-1, keepdims=True) + EPS) * onorm


def _tril(n):
    r = lax.broadcasted_iota(jnp.int32, (n, n), 0)
    c = lax.broadcasted_iota(jnp.int32, (n, n), 1)
    return r >= c


def _mixer_prompt_kernel(q32_ref, q64_ref, kin_ref, kx_ref, ko_ref, d_ref, v_ref, g_ref,
                         ona_ref, onb_ref, o_ref, sa_ref, sb_ref,
                         st_ref, att_ref, up_ref, start_ref, *, tile, chunk):
    j = pl.program_id(1)

    @pl.when(j == 0)
    def _():
        st_ref[...] = jnp.zeros_like(st_ref)

    onorm = [ona_ref[...]] * H_A + [onb_ref[...]] * (N_HEADS - H_A)
    pair = 2 * chunk
    n_pairs = tile // pair
    stack = max(len(heads) for heads in _KEY_SHARING) * chunk
    r = lax.broadcasted_iota(jnp.int32, (stack, pair), 0) % chunk
    c = lax.broadcasted_iota(jnp.int32, (stack, pair), 1)
    tril = (r >= c)[:, :chunk]
    cross = (c < chunk) | (r >= c - chunk)
    lane = lax.broadcasted_iota(jnp.int32, (chunk, HEAD), 1)
    low = jnp.where(lane < DK_A, 1.0, 0.0).astype(bf16)
    own = (low > 0, low == 0)

    def queries(ref, rows, h):
        q = ref[rows, _key_lanes(h):_key_lanes(h) + HEAD]
        return jnp.where(own[h % 2], q, jnp.zeros_like(q)) if h < H_A else q

    for heads in _KEY_SHARING:
        n = len(heads)
        kl = slice(_key_lanes(heads[0]), _key_lanes(heads[0]) + HEAD)
        vl = slice(heads[0] * HEAD, (heads[-1] + 1) * HEAD)
        for s in range(n_pairs):
            ra = slice(s * pair, s * pair + chunk)
            rb = slice(s * pair + chunk, (s + 1) * pair)
            rp = slice(s * pair, (s + 1) * pair)
            qa = [queries(q32_ref, ra, h) for h in heads]
            qb = [queries(q32_ref, rb, h) for h in heads]
            qa = qa[0] if n == 1 else jnp.concatenate(qa, axis=0)
            qb = qb[0] if n == 1 else jnp.concatenate(qb, axis=0)
            a_keys = kin_ref[s * chunk:(s + 1) * chunk, kl]
            att0 = jnp.where(tril[:n * chunk], _dot_nt(qa, a_keys), 0.0).astype(bf16)
            att1 = jnp.where(cross[:n * chunk], _dot_nt(qb, kx_ref[rp, kl]), 0.0).astype(bf16)
            up = _dot_tn(v_ref[rp, vl], ko_ref[rp, kl])
            att0 = jnp.concatenate([att0, jnp.zeros_like(att0)], axis=1)
            for i, h in enumerate(heads):
                u = h * n_pairs + s
                att_ref[u, :chunk] = att0[i * chunk:(i + 1) * chunk]
                att_ref[u, chunk:] = att1[i * chunk:(i + 1) * chunk]
                up_ref[u] = up[i * HEAD:(i + 1) * HEAD]

    for h in range(N_HEADS):
        kl = slice(_key_lanes(h), _key_lanes(h) + HEAD)
        st = st_ref[h]
        for s in range(n_pairs):
            u = h * n_pairs + s
            start_ref[u] = st.T.astype(bf16)
            st = d_ref[s:s + 1, kl] * st + up_ref[u]
        st_ref[h] = st

    for h in range(N_HEADS):
        sl = slice(h * HEAD, (h + 1) * HEAD)
        outs = []
        for s in range(n_pairs):
            u = h * n_pairs + s
            ra = slice(s * pair, s * pair + chunk)
            rp = slice(s * pair, (s + 1) * pair)
            q_pair = jnp.concatenate(
                [queries(q32_ref, ra, h),
                 queries(q64_ref, slice(s * chunk, (s + 1) * chunk), h)], axis=0)
            outs.append(jnp.dot(q_pair, start_ref[u], preferred_element_type=f32)
                        + jnp.dot(att_ref[u], v_ref[rp, sl], preferred_element_type=f32))
        o = jnp.concatenate(outs, axis=0)
        o_ref[:, sl] = (_head_norm(o, onorm[h]) * _silu(g_ref[:, sl])).astype(o_ref.dtype)

    @pl.when(j == pl.num_programs(1) - 1)
    def _():
        for h in range(H_A):
            sa_ref[0, h] = st_ref[h].T[_own_keys(h), :]
            sb_ref[0, h] = st_ref[H_A + h].T


def _mixer_prompt(acts, ona, onb, batch, seq):
    tile = MIXER_TILE
    nt = seq // tile
    chunk = PROMPT_CHUNK
    units = N_HEADS * (tile // (2 * chunk))
    kern = functools.partial(_mixer_prompt_kernel, tile=tile, chunk=chunk)

    def rows(width, per=1):
        return pl.BlockSpec((tile // per, width), lambda b, j: (b * nt + j, 0))

    return pl.pallas_call(
        kern,
        out_shape=(
            jax.ShapeDtypeStruct((batch * seq, D_MODEL), bf16),
            jax.ShapeDtypeStruct((batch, H_A, DK_A, HEAD), f32),
            jax.ShapeDtypeStruct((batch, H_A, HEAD, HEAD), f32),
        ),
        grid=(batch, nt),
        in_specs=[
            rows(K_W), rows(K_W, 2), rows(K_W, 2), rows(K_W), rows(K_W),
            rows(K_W, 2 * chunk), rows(D_MODEL), rows(D_MODEL),
            _const_spec(ona.shape),
            _const_spec(onb.shape),
        ],
        out_specs=(
            rows(D_MODEL),
            pl.BlockSpec((1, H_A, DK_A, HEAD), lambda b, j: (b, 0, 0, 0)),
            pl.BlockSpec((1, H_A, HEAD, HEAD), lambda b, j: (b, 0, 0, 0)),
        ),
        scratch_shapes=[
            pltpu.VMEM((N_HEADS, HEAD, HEAD), f32),
            pltpu.VMEM((units, 2 * chunk, 2 * chunk), bf16),
            pltpu.VMEM((units, HEAD, HEAD), f32),
            pltpu.VMEM((units, HEAD, HEAD), bf16),
        ],
        compiler_params=pltpu.CompilerParams(
            dimension_semantics=("arbitrary", "arbitrary"), vmem_limit_bytes=VMEM_LIMIT),
        name="mixer_prompt",
    )(*acts, ona, onb)


def _mixer_sample_kernel(qi_ref, ki_ref, ko_ref, v_ref, g_ref, d_ref, sa_in_ref, sb_in_ref,
                         ona_ref, onb_ref, o_ref, sa_ref, sb_ref, att_ref, oi_ref, *, bt, seq):
    onorm = [ona_ref[...]] * H_A + [onb_ref[...]] * (N_HEADS - H_A)
    tril = _tril(seq)
    zpad = jnp.zeros((HEAD - DK_A, HEAD), f32)
    zeros = jnp.zeros((seq, HEAD), f32)
    row = lax.broadcasted_iota(jnp.int32, (seq, HEAD), 0)
    ones_rows = jnp.where(row < 2, 1.0, 0.0)
    rhs_bottom = jnp.concatenate([zeros, ones_rows], axis=1)

    for bi in range(bt):
        rows = slice(bi * seq, (bi + 1) * seq)
        decay = d_ref[bi:bi + 1, :]
        d_hi = decay.astype(bf16).astype(f32)
        d_lo = decay - d_hi
        for h in range(N_HEADS):
            sl = slice(h * HEAD, (h + 1) * HEAD)
            kl = slice(_key_lanes(h), _key_lanes(h) + HEAD)
            q_in = qi_ref[rows, sl].astype(bf16)
            k_in = ki_ref[rows, kl].astype(bf16)
            v32 = v_ref[rows, sl]
            if h >= H_A:
                s = sb_in_ref[bi, h - H_A]
            elif h % 2 == 0:
                s = jnp.concatenate([sa_in_ref[bi, h], zpad], axis=0)
            else:
                s = jnp.concatenate([zpad, sa_in_ref[bi, h]], axis=0)
            att_ref[bi * N_HEADS + h] = jnp.where(tril, _dot_nt(q_in, k_in), 0.0)
            oi_ref[rows, sl] = jnp.dot(q_in, s.astype(bf16), preferred_element_type=f32)
            d_rows = jnp.where(row == 0, d_hi[:, kl], jnp.where(row == 1, d_lo[:, kl], 0.0))
            lhs = jnp.concatenate([ko_ref[rows, kl], d_rows], axis=0).astype(bf16)
            rhs = jnp.concatenate(
                [jnp.concatenate([v32, zeros], axis=1), rhs_bottom], axis=0).astype(bf16)
            upd = _dot_tn(lhs, rhs)
            s_new = upd[:, HEAD:] * s + upd[:, :HEAD]
            if h < H_A:
                sa_ref[bi, h] = s_new[_own_keys(h), :]
            else:
                sb_ref[bi, h - H_A] = s_new

    for bi in range(bt):
        rows = slice(bi * seq, (bi + 1) * seq)
        for h in range(N_HEADS):
            sl = slice(h * HEAD, (h + 1) * HEAD)
            att = att_ref[bi * N_HEADS + h].astype(bf16)
            o = oi_ref[rows, sl] + jnp.dot(att, v_ref[rows, sl].astype(bf16),
                                           preferred_element_type=f32)
            o_ref[rows, sl] = (_head_norm(o, onorm[h]) * _silu(g_ref[rows, sl])).astype(
                o_ref.dtype)


def _mixer_sample(acts, sa_in, sb_in, ona, onb, batch, seq):
    bt = SAMPLE_SEQS
    kern = functools.partial(_mixer_sample_kernel, bt=bt, seq=seq)

    def rows(width):
        return pl.BlockSpec((bt * seq, width), lambda i: (i, 0))

    sa_spec = pl.BlockSpec((bt, H_A, DK_A, HEAD), lambda i: (i, 0, 0, 0))
    sb_spec = pl.BlockSpec((bt, H_A, HEAD, HEAD), lambda i: (i, 0, 0, 0))
    return pl.pallas_call(
        kern,
        out_shape=(
            jax.ShapeDtypeStruct((batch * seq, D_MODEL), f32),
            jax.ShapeDtypeStruct((batch, H_A, DK_A, HEAD), f32),
            jax.ShapeDtypeStruct((batch, H_A, HEAD, HEAD), f32),
        ),
        grid=(batch // bt,),
        in_specs=[
            rows(D_MODEL), rows(K_W), rows(K_W), rows(D_MODEL), rows(D_MODEL),
            pl.BlockSpec((bt, K_W), lambda i: (i, 0)),
            sa_spec, sb_spec,
            _const_spec(ona.shape),
            _const_spec(onb.shape),
        ],
        out_specs=(rows(D_MODEL), sa_spec, sb_spec),
        scratch_shapes=[
            pltpu.VMEM((bt * N_HEADS, seq, seq), f32),
            pltpu.VMEM((bt * seq, D_MODEL), f32),
        ],
        compiler_params=pltpu.CompilerParams(
            dimension_semantics=("arbitrary",), vmem_limit_bytes=VMEM_LIMIT),
        name="mixer_sample",
    )(*acts, sa_in, sb_in, ona, onb)


def _out_kernel(o_ref, x_ref, w_o_ref, gf_ref, w1_ref, w3_ref, w2_ref, gl_ref, y_ref):
    x1 = x_ref[...] + jnp.dot(o_ref[...].astype(bf16), w_o_ref[...], preferred_element_type=f32)
    h = _rmsnorm(x1, gf_ref[...]).astype(bf16)
    acc = x1
    for c in range(D_FF // FF_CHUNK):
        cols = slice(c * FF_CHUNK, (c + 1) * FF_CHUNK)
        a = jnp.dot(h, w1_ref[:, cols], preferred_element_type=f32)
        b = jnp.dot(h, w3_ref[:, cols], preferred_element_type=f32)
        u = (_silu(a) * b).astype(bf16)
        acc = acc + jnp.dot(u, w2_ref[cols, :], preferred_element_type=f32)
    y_ref[...] = _rmsnorm(acc, gl_ref[...])


def _out(o2d, x2d, w_o, g_ffn, w1, w3, w2, g_final):
    n = x2d.shape[0]
    tm = min(OUT_TILE, n // 2)
    row_spec = pl.BlockSpec((tm, D_MODEL), lambda i: (i, 0))
    return pl.pallas_call(
        _out_kernel,
        out_shape=jax.ShapeDtypeStruct((n, D_MODEL), f32),
        grid=(n // tm,),
        in_specs=[
            row_spec, row_spec,
            _const_spec((D_MODEL, D_MODEL)),
            _const_spec((1, D_MODEL)),
            _const_spec((D_MODEL, D_FF)),
            _const_spec((D_MODEL, D_FF)),
            _const_spec((D_FF, D_MODEL)),
            _const_spec((1, D_MODEL)),
        ],
        out_specs=row_spec,
        compiler_params=pltpu.CompilerParams(
            dimension_semantics=("arbitrary",), vmem_limit_bytes=VMEM_LIMIT),
        name="out_ffn",
    )(o2d, x2d, w_o, g_ffn, w1, w3, w2, g_final)


def _w_in_sections():
    sections, src = [], 0
    for width, dst in ((KA_W, OFF_QA), (KA_W, OFF_KA), (SEC, OFF_VA), (SEC, OFF_GA),
                       (GATE_RANK, OFF_LR), (SEC, OFF_QB), (SEC, OFF_FB), (SEC, OFF_IB),
                       (SEC, OFF_GB)):
        sections.append((src, width, dst))
        src += width
    return tuple(sections), src


_W_IN_SECTIONS, D_IN = _w_in_sections()


def _w_in_layout_kernel(wt_ref, o_ref):
    for src, width, dst in _W_IN_SECTIONS:
        part = wt_ref[src:src + width, :]
        if width % HEAD:
            pad = jnp.zeros((HEAD - width % HEAD, part.shape[1]), part.dtype)
            part = jnp.concatenate([part, pad], axis=0)
        o_ref[:, dst:dst + part.shape[0]] = part.T.astype(o_ref.dtype)


def _layout_w_in(w_in_t, cols=256):
    return pl.pallas_call(
        _w_in_layout_kernel,
        out_shape=jax.ShapeDtypeStruct((D_MODEL, D_P), bf16),
        grid=(D_MODEL // cols,),
        in_specs=[pl.BlockSpec((D_IN, cols), lambda i: (0, i))],
        out_specs=pl.BlockSpec((cols, D_P), lambda i: (i, 0)),
        compiler_params=pltpu.CompilerParams(
            dimension_semantics=("arbitrary",), vmem_limit_bytes=VMEM_LIMIT),
        name="w_in_layout",
    )(w_in_t)


def kernel(x_prompt, x_sample, state_gla, state_hgrn, norm_mix, w_in, w_a2, b_a, lb_param,
           onorm_a, onorm_b, w_o, norm_ffn, w1, w3, w2, norm_final):
    batch, seq, _ = x_prompt.shape
    dec_batch, dec_seq, _ = x_sample.shape

    w_in_p = _layout_w_in(jnp.swapaxes(w_in[0], 0, 1))
    w_a2p = jnp.pad(w_a2[0], ((0, HEAD - GATE_RANK), (0, 0))).astype(bf16)
    b_ap = b_a[0].reshape(1, KA_W)
    g_mix = norm_mix[0].reshape(1, D_MODEL)
    g_ffn = norm_ffn[0].reshape(1, D_MODEL)
    g_fin = norm_final.reshape(1, D_MODEL)
    ona = onorm_a[0].reshape(1, HEAD)
    onb = onorm_b[0].reshape(1, HEAD)

    xp = x_prompt.reshape(batch * seq, D_MODEL)
    xs = x_sample.reshape(dec_batch * dec_seq, D_MODEL)

    later = (w_o[0], w1[0], w3[0], w2[0])
    *acts_p, w_o_b, w1_b, w3_b, w2_b = _proj(xp, g_mix, w_in_p, w_a2p, b_ap, lb_param,
                                             chunk=PROMPT_CHUNK, paired=True, cast=later)
    op, sa_p, sb_p = _mixer_prompt(acts_p, ona, onb, batch, seq)
    yp = _out(op, xp, w_o_b, g_ffn, w1_b, w3_b, w2_b, g_fin)

    q_s, kin_s, ko_s, d_s, v_s, g_s = _proj(xs, g_mix, w_in_p, w_a2p, b_ap, lb_param,
                                            chunk=dec_seq, paired=False)
    os_, sa_s, sb_s = _mixer_sample((q_s, kin_s, ko_s, v_s, g_s, d_s), state_gla[0],
                                    state_hgrn[0], ona, onb, dec_batch, dec_seq)
    ys = _out(os_, xs, w_o_b, g_ffn, w1_b, w3_b, w2_b, g_fin)

    return (yp.reshape(batch, seq, D_MODEL), ys.reshape(dec_batch, dec_seq, D_MODEL),
            sa_p[None], sb_p[None], sa_s[None], sb_s[None])
```

```python
import functools

import jax
import jax.numpy as jnp
from jax import lax
from jax.experimental import pallas as pl
from jax.experimental.pallas import tpu as pltpu

D_MODEL = 1024
N_HEADS = 8
H_A = 4
DK_A = 64
HEAD = 128
GATE_RANK = 16
GATE_NORM = 16.0
D_FF = 2816
FF_CHUNK = 256
EPS = 1e-6
PROMPT_CHUNK = 32
SUBLANES = 8
BF16_ROWS = 16

KA_W = H_A * DK_A
SEC = H_A * HEAD
K_W = KA_W + SEC
OFF_LR = 0
OFF_QA = HEAD
OFF_KA = OFF_QA + KA_W
OFF_VA = OFF_KA + KA_W
OFF_GA = OFF_VA + SEC
OFF_FB = OFF_GA + SEC
OFF_QB = OFF_FB + SEC
OFF_IB = OFF_QB + SEC
OFF_GB = OFF_IB + SEC
D_P = OFF_GB + SEC
MXU_N = 256
PROJ_GROUPS = ((OFF_LR, OFF_VA), (OFF_FB, OFF_IB), (OFF_VA, OFF_FB), (OFF_IB, D_P))

ROW_TILE = 512
MIXER_TILE = 1024
OUT_TILE = 1024
SAMPLE_SEQS = 16
VMEM_LIMIT = 56 * 1024 * 1024

f32 = jnp.float32
bf16 = jnp.bfloat16


def _const_spec(shape):
    nd = len(shape)
    return pl.BlockSpec(shape, lambda *_: (0,) * nd, pipeline_mode=pl.Buffered(1))


def _rmsnorm(x, g):
    return x * lax.rsqrt(jnp.mean(x * x, axis=-1, keepdims=True) + EPS) * g


def _sigmoid(x):
    return 0.5 * jnp.tanh(0.5 * x) + 0.5


def _silu(x):
    hx = 0.5 * x
    return hx + hx * jnp.tanh(hx)


def _dot_nt(a, b):
    return lax.dot_general(a, b, (((1,), (1,)), ((), ())), preferred_element_type=f32)


def _dot_tn(a, b):
    return lax.dot_general(a, b, (((0,), (0,)), ((), ())), preferred_element_type=f32)


def _key_lanes(h):
    return HEAD * (h // 2) if h < H_A else KA_W + HEAD * (h - H_A)


_KEY_SHARING = tuple((h, h + 1) for h in range(0, H_A, 2)) + tuple(
    (h,) for h in range(H_A, N_HEADS))


def _own_keys(h):
    return slice(DK_A * (h % 2), DK_A * (h % 2 + 1)) if h < H_A else slice(0, HEAD)


def _cumsum_rows(x, chunk):
    rows, width = x.shape
    row = lax.broadcasted_iota(jnp.int32, (SUBLANES, width), 0)
    groups_per_chunk = chunk // SUBLANES
    out = []
    carry = None
    for g in range(rows // SUBLANES):
        y = x[g * SUBLANES:(g + 1) * SUBLANES]
        for s in (1, 2, 4):
            y = y + jnp.where(row >= s, pltpu.roll(y, s, 0), 0.0)
        if g % groups_per_chunk != 0:
            y = y + carry
        carry = y[SUBLANES - 1:SUBLANES]
        out.append(y)
    return out[0] if len(out) == 1 else jnp.concatenate(out, axis=0)


def _lower_bound(lbp_ref):
    lbp = lbp_ref[...]
    m = jnp.max(lbp, axis=0, keepdims=True)
    e = jnp.exp(lbp - m)
    return e[0:1] / jnp.sum(e, axis=0, keepdims=True)


def _decays(la, chunk):
    rows = la.shape[0]
    b = _cumsum_rows(la, chunk)
    n_chunks = rows // chunk
    lasts = [b[(c + 1) * chunk - 1:(c + 1) * chunk] for c in range(n_chunks)]
    if n_chunks == 1:
        b_last = lasts[0]
        b_last_rows = b_last
    else:
        b_last = jnp.concatenate(lasts, axis=0)
        b_last_rows = jnp.concatenate(
            [jnp.broadcast_to(l, (chunk, l.shape[1])) for l in lasts], axis=0)
    return jnp.exp(b), jnp.exp(-b), jnp.exp(b_last_rows - b), jnp.exp(b_last)


def _emit_keys_paired(refs, r, q_targets, kl, q, k, la):
    q32_ref, q64_ref, kin_ref, kx_ref, ko_ref, d_ref = refs
    c = PROMPT_CHUNK
    rows = slice(2 * c * r, 2 * c * (r + 1))
    e_b, e_nb, e_out, decay = _decays(la, c)
    d0, d1 = decay[0:1], decay[1:2]
    q32 = q * e_b
    k_in = k * e_nb
    k_out = k * e_out
    q32_ref[rows, kl] = q32.astype(bf16)
    q64_ref[c * r:c * (r + 1), kl] = (q32[c:] * d0).astype(bf16)
    kin_ref[c * r:c * (r + 1), kl] = k_in[:c].astype(bf16)
    kx_ref[rows, kl] = jnp.concatenate([k_out[:c], k_in[c:]], axis=0).astype(bf16)
    ko_ref[rows, kl] = jnp.concatenate([k_out[:c] * d1, k_out[c:]], axis=0).astype(bf16)
    d_ref[r:r + 1, kl] = d0 * d1


def _emit_keys_single(refs, r, q_targets, kl, q, k, la, *, chunk):
    q_ref, kin_ref, ko_ref, d_ref = refs
    rows = slice(SUBLANES * chunk * r, SUBLANES * chunk * (r + 1))
    e_b, e_nb, e_out, decay = _decays(la, chunk)
    q_in = q * e_b
    for ql, own in q_targets:
        q_ref[rows, ql] = q_in if own is None else jnp.where(own, q_in, 0.0)
    kin_ref[rows, kl] = k * e_nb
    ko_ref[rows, kl] = k * e_out
    d_ref[SUBLANES * r:SUBLANES * (r + 1), kl] = decay


def _proj_kernel(x_ref, gm_ref, w_ref, w_a2_ref, b_a_ref, lbp_ref, *refs, chunk, paired, n_cast):
    cast_in, refs = refs[:n_cast], refs[n_cast:]
    n_key_refs = 6 if paired else 4
    key_refs = refs[:n_key_refs]
    v_ref, g_ref = refs[n_key_refs:n_key_refs + 2]
    cast_out = refs[n_key_refs + 2:n_key_refs + 2 + n_cast]
    p_refs = refs[n_key_refs + 2 + n_cast:]
    tm = x_ref.shape[0]
    block = 2 * chunk if paired else SUBLANES * chunk
    emit_keys = (_emit_keys_paired if paired
                 else functools.partial(_emit_keys_single, chunk=chunk))
    blocks = [(slice(r * block, (r + 1) * block), r) for r in range(tm // block)]
    h = _rmsnorm(x_ref[...], gm_ref[...]).astype(bf16)

    w_a2 = w_a2_ref[...]
    b_a = b_a_ref[...]
    lb = _lower_bound(lbp_ref)
    lane = lax.broadcasted_iota(jnp.int32, (block, HEAD), 1)

    def p_slot(lo, hi):
        for (g_lo, g_hi), ref in zip(PROJ_GROUPS, p_refs):
            if g_lo <= lo and hi <= g_hi:
                return ref, slice(lo - g_lo, hi - g_lo)
        raise ValueError((lo, hi))

    def project(lo, hi):
        ref, cols = p_slot(lo, hi)
        ref[:, cols] = jnp.dot(h, w_ref[:, lo:hi], preferred_element_type=f32)

    def p(rows, lo, width):
        ref, cols = p_slot(lo, lo + width)
        return ref[rows, cols]

    def gla_keys(rows, r):
        lra = p(rows, OFF_LR, HEAD)
        z = jnp.dot(lra.astype(bf16), w_a2, preferred_element_type=f32) + b_a
        for pr in range(H_A // 2):
            kl = slice(pr * HEAD, (pr + 1) * HEAD)
            zz = z[:, kl]
            la = (jnp.minimum(zz, 0.0) - jnp.log(1.0 + jnp.exp(-jnp.abs(zz)))) * (1.0 / GATE_NORM)
            q = p(rows, OFF_QA + pr * HEAD, HEAD) * (DK_A ** -0.5)
            k = p(rows, OFF_KA + pr * HEAD, HEAD)
            q_targets = [(slice(hd * HEAD, (hd + 1) * HEAD), own)
                         for hd, own in ((2 * pr, lane < DK_A), (2 * pr + 1, lane >= DK_A))]
            emit_keys(key_refs, r, q_targets, kl, q, k, la)

    def hgrn_keys(rows, r):
        for hd in range(N_HEADS - H_A):
            lbh = lb[:, hd * HEAD:(hd + 1) * HEAD]
            f = lbh + (1.0 - lbh) * _sigmoid(p(rows, OFF_FB + hd * HEAD, HEAD))
            q = _silu(p(rows, OFF_QB + hd * HEAD, HEAD))
            ql = slice(SEC + hd * HEAD, SEC + (hd + 1) * HEAD)
            kl = slice(KA_W + hd * HEAD, KA_W + (hd + 1) * HEAD)
            emit_keys(key_refs, r, [(ql, None)], kl, q, 1.0 - f, jnp.log(f))

    def gla_values(rows, crow):
        v_ref[rows, 0:SEC] = p(rows, OFF_VA, SEC).astype(v_ref.dtype)
        g_ref[rows, 0:SEC] = p(rows, OFF_GA, SEC).astype(g_ref.dtype)

    def hgrn_values(rows, crow):
        v_ref[rows, SEC:2 * SEC] = p(rows, OFF_IB, SEC).astype(v_ref.dtype)
        g_ref[rows, SEC:2 * SEC] = p(rows, OFF_GB, SEC).astype(g_ref.dtype)

    def tasks(fn, which):
        return [functools.partial(fn, *blocks[r]) for r in which]

    def cast(src, dst):
        dst[...] = src[...].astype(dst.dtype)

    nb = len(blocks)
    first, second = range(nb // 2), range(nb // 2, nb)
    windows = (
        [functools.partial(cast, s, d) for s, d in zip(cast_in, cast_out)],
        tasks(gla_keys, range(nb)),
        tasks(hgrn_keys, first),
        tasks(hgrn_keys, second) + tasks(gla_values, range(nb)),
    )
    for (lo, hi), work in zip(PROJ_GROUPS, windows):
        starts = list(range(lo, hi, MXU_N))
        for i, c0 in enumerate(starts):
            project(c0, min(c0 + MXU_N, hi))
            for t in work[i * len(work) // len(starts):(i + 1) * len(work) // len(starts)]:
                t()
    for t in tasks(hgrn_values, range(nb)):
        t()


def _slice_spec(shape, steps):
    rows, width = shape
    height = next(r for r in range(BF16_ROWS, rows + 1, BF16_ROWS)
                  if rows % r == 0 and rows // r <= steps)
    last = rows // height - 1
    return pl.BlockSpec((height, width), lambda i: (jnp.minimum(i, last), 0))


def _proj(x2d, g_mix, w_in_p, w_a2p, b_ap, lb_param, *, chunk, paired, cast=()):
    n = x2d.shape[0]
    tm = ROW_TILE
    steps = n // tm
    kern = functools.partial(_proj_kernel, chunk=chunk, paired=paired, n_cast=len(cast))

    def arr(rows, width, dtype):
        return (jax.ShapeDtypeStruct((n // rows, width), dtype),
                pl.BlockSpec((tm // rows, width), lambda i: (i, 0)))

    if paired:
        keys = [arr(1, K_W, bf16), arr(2, K_W, bf16), arr(2, K_W, bf16),
                arr(1, K_W, bf16), arr(1, K_W, bf16), arr(2 * chunk, K_W, f32)]
        v_out = arr(1, D_MODEL, bf16)
    else:
        keys = [arr(1, D_MODEL, f32), arr(1, K_W, f32), arr(1, K_W, f32), arr(chunk, K_W, f32)]
        v_out = arr(1, D_MODEL, f32)
    outs = keys + [v_out, arr(1, D_MODEL, v_out[0].dtype)]
    cast_specs = [_slice_spec(w.shape, steps) for w in cast]
    outs += [(jax.ShapeDtypeStruct(w.shape, bf16), spec) for w, spec in zip(cast, cast_specs)]

    return pl.pallas_call(
        kern,
        out_shape=tuple(o[0] for o in outs),
        grid=(steps,),
        in_specs=[
            pl.BlockSpec((tm, D_MODEL), lambda i: (i, 0)),
            _const_spec((1, D_MODEL)),
            _const_spec((D_MODEL, D_P)),
            _const_spec(w_a2p.shape),
            _const_spec(b_ap.shape),
            _const_spec(lb_param.shape),
        ] + cast_specs,
        out_specs=tuple(o[1] for o in outs),
        scratch_shapes=[pltpu.VMEM((tm, hi - lo), f32) for lo, hi in PROJ_GROUPS],
        compiler_params=pltpu.CompilerParams(
            dimension_semantics=("arbitrary",), vmem_limit_bytes=VMEM_LIMIT),
        name="proj",
    )(x2d, g_mix, w_in_p, w_a2p, b_ap, lb_param, *cast)


def _head_norm(o, onorm):
    return o * lax.rsqrt(jnp.mean(o * o, axis=-1, keepdims=True) + EPS) * onorm


def _tril(n):
    r = lax.broadcasted_iota(jnp.int32, (n, n), 0)
    c = lax.broadcasted_iota(jnp.int32, (n, n), 1)
    return r >= c


def _mixer_prompt_kernel(q32_ref, q64_ref, kin_ref, kx_ref, ko_ref, d_ref, v_ref, g_ref,
                         ona_ref, onb_ref, o_ref, sa_ref, sb_ref,
                         st_ref, att_ref, up_ref, start_ref, *, tile, chunk):
    j = pl.program_id(1)

    @pl.when(j == 0)
    def _():
        st_ref[...] = jnp.zeros_like(st_ref)

    onorm = [ona_ref[...]] * H_A + [onb_ref[...]] * (N_HEADS - H_A)
    pair = 2 * chunk
    n_pairs = tile // pair
    stack = max(len(heads) for heads in _KEY_SHARING) * chunk
    r = lax.broadcasted_iota(jnp.int32, (stack, pair), 0) % chunk
    c = lax.broadcasted_iota(jnp.int32, (stack, pair), 1)
    tril = (r >= c)[:, :chunk]
    cross = (c < chunk) | (r >= c - chunk)
    lane = lax.broadcasted_iota(jnp.int32, (chunk, HEAD), 1)
    low = jnp.where(lane < DK_A, 1.0, 0.0).astype(bf16)
    own = (low > 0, low == 0)

    def queries(ref, rows, h):
        q = ref[rows, _key_lanes(h):_key_lanes(h) + HEAD]
        return jnp.where(own[h % 2], q, jnp.zeros_like(q)) if h < H_A else q

    for heads in _KEY_SHARING:
        n = len(heads)
        kl = slice(_key_lanes(heads[0]), _key_lanes(heads[0]) + HEAD)
        vl = slice(heads[0] * HEAD, (heads[-1] + 1) * HEAD)
        for s in range(n_pairs):
            ra = slice(s * pair, s * pair + chunk)
            rb = slice(s * pair + chunk, (s + 1) * pair)
            rp = slice(s * pair, (s + 1) * pair)
            qa = [queries(q32_ref, ra, h) for h in heads]
            qb = [queries(q32_ref, rb, h) for h in heads]
            qa = qa[0] if n == 1 else jnp.concatenate(qa, axis=0)
            qb = qb[0] if n == 1 else jnp.concatenate(qb, axis=0)
            a_keys = kin_ref[s * chunk:(s + 1) * chunk, kl]
            att0 = jnp.where(tril[:n * chunk], _dot_nt(qa, a_keys), 0.0).astype(bf16)
            att1 = jnp.where(cross[:n * chunk], _dot_nt(qb, kx_ref[rp, kl]), 0.0).astype(bf16)
            up = _dot_tn(v_ref[rp, vl], ko_ref[rp, kl])
            att0 = jnp.concatenate([att0, jnp.zeros_like(att0)], axis=1)
            for i, h in enumerate(heads):
                u = h * n_pairs + s
                att_ref[u, :chunk] = att0[i * chunk:(i + 1) * chunk]
                att_ref[u, chunk:] = att1[i * chunk:(i + 1) * chunk]
                up_ref[u] = up[i * HEAD:(i + 1) * HEAD]

    for h in range(N_HEADS):
        kl = slice(_key_lanes(h), _key_lanes(h) + HEAD)
        st = st_ref[h]
        for s in range(n_pairs):
            u = h * n_pairs + s
            start_ref[u] = st.T.astype(bf16)
            st = d_ref[s:s + 1, kl] * st + up_ref[u]
        st_ref[h] = st

    for h in range(N_HEADS):
        sl = slice(h * HEAD, (h + 1) * HEAD)
        outs = []
        for s in range(n_pairs):
            u = h * n_pairs + s
            ra = slice(s * pair, s * pair + chunk)
            rp = slice(s * pair, (s + 1) * pair)
            q_pair = jnp.concatenate(
                [queries(q32_ref, ra, h),
                 queries(q64_ref, slice(s * chunk, (s + 1) * chunk), h)], axis=0)
            outs.append(jnp.dot(q_pair, start_ref[u], preferred_element_type=f32)
                        + jnp.dot(att_ref[u], v_ref[rp, sl], preferred_element_type=f32))
        o = jnp.concatenate(outs, axis=0)
        gate = _silu(g_ref[:, sl].astype(f32))
        o_ref[:, sl] = (_head_norm(o, onorm[h]) * gate).astype(o_ref.dtype)

    @pl.when(j == pl.num_programs(1) - 1)
    def _():
        for h in range(H_A):
            sa_ref[0, h] = st_ref[h].T[_own_keys(h), :]
            sb_ref[0, h] = st_ref[H_A + h].T


def _mixer_prompt(acts, ona, onb, batch, seq):
    tile = MIXER_TILE
    nt = seq // tile
    chunk = PROMPT_CHUNK
    units = N_HEADS * (tile // (2 * chunk))
    kern = functools.partial(_mixer_prompt_kernel, tile=tile, chunk=chunk)

    def rows(width, per=1):
        return pl.BlockSpec((tile // per, width), lambda b, j: (b * nt + j, 0))

    return pl.pallas_call(
        kern,
        out_shape=(
            jax.ShapeDtypeStruct((batch * seq, D_MODEL), bf16),
            jax.ShapeDtypeStruct((batch, H_A, DK_A, HEAD), f32),
            jax.ShapeDtypeStruct((batch, H_A, HEAD, HEAD), f32),
        ),
        grid=(batch, nt),
        in_specs=[
            rows(K_W), rows(K_W, 2), rows(K_W, 2), rows(K_W), rows(K_W),
            rows(K_W, 2 * chunk), rows(D_MODEL), rows(D_MODEL),
            _const_spec(ona.shape),
            _const_spec(onb.shape),
        ],
        out_specs=(
            rows(D_MODEL),
            pl.BlockSpec((1, H_A, DK_A, HEAD), lambda b, j: (b, 0, 0, 0)),
            pl.BlockSpec((1, H_A, HEAD, HEAD), lambda b, j: (b, 0, 0, 0)),
        ),
        scratch_shapes=[
            pltpu.VMEM((N_HEADS, HEAD, HEAD), f32),
            pltpu.VMEM((units, 2 * chunk, 2 * chunk), bf16),
            pltpu.VMEM((units, HEAD, HEAD), f32),
            pltpu.VMEM((units, HEAD, HEAD), bf16),
        ],
        compiler_params=pltpu.CompilerParams(
            dimension_semantics=("arbitrary", "arbitrary"), vmem_limit_bytes=VMEM_LIMIT),
        name="mixer_prompt",
    )(*acts, ona, onb)


def _mixer_sample_kernel(qi_ref, ki_ref, ko_ref, v_ref, g_ref, d_ref, sa_in_ref, sb_in_ref,
                         ona_ref, onb_ref, o_ref, sa_ref, sb_ref, att_ref, oi_ref, *, bt, seq):
    onorm = [ona_ref[...]] * H_A + [onb_ref[...]] * (N_HEADS - H_A)
    tril = _tril(seq)
    zpad = jnp.zeros((HEAD - DK_A, HEAD), f32)
    zeros = jnp.zeros((seq, HEAD), f32)
    row = lax.broadcasted_iota(jnp.int32, (seq, HEAD), 0)
    ones_rows = jnp.where(row < 2, 1.0, 0.0)
    rhs_bottom = jnp.concatenate([zeros, ones_rows], axis=1)

    for bi in range(bt):
        rows = slice(bi * seq, (bi + 1) * seq)
        decay = d_ref[bi:bi + 1, :]
        d_hi = decay.astype(bf16).astype(f32)
        d_lo = decay - d_hi
        for h in range(N_HEADS):
            sl = slice(h * HEAD, (h + 1) * HEAD)
            kl = slice(_key_lanes(h), _key_lanes(h) + HEAD)
            q_in = qi_ref[rows, sl].astype(bf16)
            k_in = ki_ref[rows, kl].astype(bf16)
            v32 = v_ref[rows, sl]
            if h >= H_A:
                s = sb_in_ref[bi, h - H_A]
            elif h % 2 == 0:
                s = jnp.concatenate([sa_in_ref[bi, h], zpad], axis=0)
            else:
                s = jnp.concatenate([zpad, sa_in_ref[bi, h]], axis=0)
            att_ref[bi * N_HEADS + h] = jnp.where(tril, _dot_nt(q_in, k_in), 0.0)
            oi_ref[rows, sl] = jnp.dot(q_in, s.astype(bf16), preferred_element_type=f32)
            d_rows = jnp.where(row == 0, d_hi[:, kl], jnp.where(row == 1, d_lo[:, kl], 0.0))
            lhs = jnp.concatenate([ko_ref[rows, kl], d_rows], axis=0).astype(bf16)
            rhs = jnp.concatenate(
                [jnp.concatenate([v32, zeros], axis=1), rhs_bottom], axis=0).astype(bf16)
            upd = _dot_tn(lhs, rhs)
            s_new = upd[:, HEAD:] * s + upd[:, :HEAD]
            if h < H_A:
                sa_ref[bi, h] = s_new[_own_keys(h), :]
            else:
                sb_ref[bi, h - H_A] = s_new

    for bi in range(bt):
        rows = slice(bi * seq, (bi + 1) * seq)
        for h in range(N_HEADS):
            sl = slice(h * HEAD, (h + 1) * HEAD)
            att = att_ref[bi * N_HEADS + h].astype(bf16)
            o = oi_ref[rows, sl] + jnp.dot(att, v_ref[rows, sl].astype(bf16),
                                           preferred_element_type=f32)
            o_ref[rows, sl] = (_head_norm(o, onorm[h]) * _silu(g_ref[rows, sl])).astype(
                o_ref.dtype)


def _mixer_sample(acts, sa_in, sb_in, ona, onb, batch, seq):
    bt = SAMPLE_SEQS
    kern = functools.partial(_mixer_sample_kernel, bt=bt, seq=seq)

    def rows(width):
        return pl.BlockSpec((bt * seq, width), lambda i: (i, 0))

    sa_spec = pl.BlockSpec((bt, H_A, DK_A, HEAD), lambda i: (i, 0, 0, 0))
    sb_spec = pl.BlockSpec((bt, H_A, HEAD, HEAD), lambda i: (i, 0, 0, 0))
    return pl.pallas_call(
        kern,
        out_shape=(
            jax.ShapeDtypeStruct((batch * seq, D_MODEL), f32),
            jax.ShapeDtypeStruct((batch, H_A, DK_A, HEAD), f32),
            jax.ShapeDtypeStruct((batch, H_A, HEAD, HEAD), f32),
        ),
        grid=(batch // bt,),
        in_specs=[
            rows(D_MODEL), rows(K_W), rows(K_W), rows(D_MODEL), rows(D_MODEL),
            pl.BlockSpec((bt, K_W), lambda i: (i, 0)),
            sa_spec, sb_spec,
            _const_spec(ona.shape),
            _const_spec(onb.shape),
        ],
        out_specs=(rows(D_MODEL), sa_spec, sb_spec),
        scratch_shapes=[
            pltpu.VMEM((bt * N_HEADS, seq, seq), f32),
            pltpu.VMEM((bt * seq, D_MODEL), f32),
        ],
        compiler_params=pltpu.CompilerParams(
            dimension_semantics=("arbitrary",), vmem_limit_bytes=VMEM_LIMIT),
        name="mixer_sample",
    )(*acts, sa_in, sb_in, ona, onb)


def _out_kernel(o_ref, x_ref, w_o_ref, gf_ref, w1_ref, w3_ref, w2_ref, gl_ref, y_ref):
    x1 = x_ref[...] + jnp.dot(o_ref[...].astype(bf16), w_o_ref[...], preferred_element_type=f32)
    h = _rmsnorm(x1, gf_ref[...]).astype(bf16)
    acc = x1
    for c in range(D_FF // FF_CHUNK):
        cols = slice(c * FF_CHUNK, (c + 1) * FF_CHUNK)
        a = jnp.dot(h, w1_ref[:, cols], preferred_element_type=f32)
        b = jnp.dot(h, w3_ref[:, cols], preferred_element_type=f32)
        u = (_silu(a) * b).astype(bf16)
        acc = acc + jnp.dot(u, w2_ref[cols, :], preferred_element_type=f32)
    y_ref[...] = _rmsnorm(acc, gl_ref[...])


def _out(o2d, x2d, w_o, g_ffn, w1, w3, w2, g_final):
    n = x2d.shape[0]
    tm = min(OUT_TILE, n // 2)
    row_spec = pl.BlockSpec((tm, D_MODEL), lambda i: (i, 0))
    return pl.pallas_call(
        _out_kernel,
        out_shape=jax.ShapeDtypeStruct((n, D_MODEL), f32),
        grid=(n // tm,),
        in_specs=[
            row_spec, row_spec,
            _const_spec((D_MODEL, D_MODEL)),
            _const_spec((1, D_MODEL)),
            _const_spec((D_MODEL, D_FF)),
            _const_spec((D_MODEL, D_FF)),
            _const_spec((D_FF, D_MODEL)),
            _const_spec((1, D_MODEL)),
        ],
        out_specs=row_spec,
        compiler_params=pltpu.CompilerParams(
            dimension_semantics=("arbitrary",), vmem_limit_bytes=VMEM_LIMIT),
        name="out_ffn",
    )(o2d, x2d, w_o, g_ffn, w1, w3, w2, g_final)


def _w_in_sections():
    sections, src = [], 0
    for width, dst in ((KA_W, OFF_QA), (KA_W, OFF_KA), (SEC, OFF_VA), (SEC, OFF_GA),
                       (GATE_RANK, OFF_LR), (SEC, OFF_QB), (SEC, OFF_FB), (SEC, OFF_IB),
                       (SEC, OFF_GB)):
        sections.append((src, width, dst))
        src += width
    return tuple(sections), src


_W_IN_SECTIONS, D_IN = _w_in_sections()


def _w_in_layout_kernel(wt_ref, o_ref):
    for src, width, dst in _W_IN_SECTIONS:
        part = wt_ref[src:src + width, :]
        if width % HEAD:
            pad = jnp.zeros((HEAD - width % HEAD, part.shape[1]), part.dtype)
            part = jnp.concatenate([part, pad], axis=0)
        o_ref[:, dst:dst + part.shape[0]] = part.T.astype(o_ref.dtype)


def _layout_w_in(w_in_t, cols=256):
    return pl.pallas_call(
        _w_in_layout_kernel,
        out_shape=jax.ShapeDtypeStruct((D_MODEL, D_P), bf16),
        grid=(D_MODEL // cols,),
        in_specs=[pl.BlockSpec((D_IN, cols), lambda i: (0, i))],
        out_specs=pl.BlockSpec((cols, D_P), lambda i: (i, 0)),
        compiler_params=pltpu.CompilerParams(
            dimension_semantics=("arbitrary",), vmem_limit_bytes=VMEM_LIMIT),
        name="w_in_layout",
    )(w_in_t)


def kernel(x_prompt, x_sample, state_gla, state_hgrn, norm_mix, w_in, w_a2, b_a, lb_param,
           onorm_a, onorm_b, w_o, norm_ffn, w1, w3, w2, norm_final):
    batch, seq, _ = x_prompt.shape
    dec_batch, dec_seq, _ = x_sample.shape

    w_in_p = _layout_w_in(jnp.swapaxes(w_in[0], 0, 1))
    w_a2p = jnp.pad(w_a2[0], ((0, HEAD - GATE_RANK), (0, 0))).astype(bf16)
    b_ap = b_a[0].reshape(1, KA_W)
    g_mix = norm_mix[0].reshape(1, D_MODEL)
    g_ffn = norm_ffn[0].reshape(1, D_MODEL)
    g_fin = norm_final.reshape(1, D_MODEL)
    ona = onorm_a[0].reshape(1, HEAD)
    onb = onorm_b[0].reshape(1, HEAD)

    xp = x_prompt.reshape(batch * seq, D_MODEL)
    xs = x_sample.reshape(dec_batch * dec_seq, D_MODEL)

    later = (w_o[0], w1[0], w3[0], w2[0])
    *acts_p, w_o_b, w1_b, w3_b, w2_b = _proj(xp, g_mix, w_in_p, w_a2p, b_ap, lb_param,
                                             chunk=PROMPT_CHUNK, paired=True, cast=later)
    op, sa_p, sb_p = _mixer_prompt(acts_p, ona, onb, batch, seq)
    yp = _out(op, xp, w_o_b, g_ffn, w1_b, w3_b, w2_b, g_fin)

    q_s, kin_s, ko_s, d_s, v_s, g_s = _proj(xs, g_mix, w_in_p, w_a2p, b_ap, lb_param,
                                            chunk=dec_seq, paired=False)
    os_, sa_s, sb_s = _mixer_sample((q_s, kin_s, ko_s, v_s, g_s, d_s), state_gla[0],
                                    state_hgrn[0], ona, onb, dec_batch, dec_seq)
    ys = _out(os_, xs, w_o_b, g_ffn, w1_b, w3_b, w2_b, g_fin)

    return (yp.reshape(batch, seq, D_MODEL), ys.reshape(dec_batch, dec_seq, D_MODEL),
            sa_p[None], sb_p[None], sa_s[None], sb_s[None])
```

```python
import functools

import jax
import jax.numpy as jnp
from jax import lax
from jax.experimental import pallas as pl
from jax.experimental.pallas import tpu as pltpu

D_MODEL = 1024
N_HEADS = 8
H_A = 4
DK_A = 64
HEAD = 128
GATE_RANK = 16
GATE_NORM = 16.0
D_FF = 2816
FF_CHUNK = 256
EPS = 1e-6
PROMPT_CHUNK = 32
SUBLANES = 8
BF16_ROWS = 16

KA_W = H_A * DK_A
SEC = H_A * HEAD
K_W = KA_W + SEC
OFF_LR = 0
OFF_QA = HEAD
OFF_KA = OFF_QA + KA_W
OFF_VA = OFF_KA + KA_W
OFF_GA = OFF_VA + SEC
OFF_FB = OFF_GA + SEC
OFF_QB = OFF_FB + SEC
OFF_IB = OFF_QB + SEC
OFF_GB = OFF_IB + SEC
D_P = OFF_GB + SEC
MXU_N = 256
PROJ_GROUPS = ((OFF_LR, OFF_VA), (OFF_FB, OFF_IB), (OFF_VA, OFF_FB), (OFF_IB, D_P))

ROW_TILE = 512
MIXER_TILE = 1024
OUT_TILE = 1024
SAMPLE_SEQS = 16
VMEM_LIMIT = 56 * 1024 * 1024

f32 = jnp.float32
bf16 = jnp.bfloat16


def _const_spec(shape):
    nd = len(shape)
    return pl.BlockSpec(shape, lambda *_: (0,) * nd, pipeline_mode=pl.Buffered(1))


def _rmsnorm(x, g):
    return x * lax.rsqrt(jnp.mean(x * x, axis=-1, keepdims=True) + EPS) * g


def _sigmoid(x):
    return 0.5 * jnp.tanh(0.5 * x) + 0.5


def _silu(x):
    hx = 0.5 * x
    return hx + hx * jnp.tanh(hx)


def _dot_nt(a, b):
    return lax.dot_general(a, b, (((1,), (1,)), ((), ())), preferred_element_type=f32)


def _dot_tn(a, b):
    return lax.dot_general(a, b, (((0,), (0,)), ((), ())), preferred_element_type=f32)


def _key_lanes(h):
    return HEAD * (h // 2) if h < H_A else KA_W + HEAD * (h - H_A)


_KEY_SHARING = tuple((h, h + 1) for h in range(0, H_A, 2)) + tuple(
    (h,) for h in range(H_A, N_HEADS))


def _own_keys(h):
    return slice(DK_A * (h % 2), DK_A * (h % 2 + 1)) if h < H_A else slice(0, HEAD)


def _cumsum_rows(x, chunk):
    rows, width = x.shape
    row = lax.broadcasted_iota(jnp.int32, (SUBLANES, width), 0)
    groups_per_chunk = chunk // SUBLANES
    out = []
    carry = None
    for g in range(rows // SUBLANES):
        y = x[g * SUBLANES:(g + 1) * SUBLANES]
        for s in (1, 2, 4):
            y = y + jnp.where(row >= s, pltpu.roll(y, s, 0), 0.0)
        if g % groups_per_chunk != 0:
            y = y + carry
        carry = y[SUBLANES - 1:SUBLANES]
        out.append(y)
    return out[0] if len(out) == 1 else jnp.concatenate(out, axis=0)


def _lower_bound(lbp_ref):
    lbp = lbp_ref[...]
    m = jnp.max(lbp, axis=0, keepdims=True)
    e = jnp.exp(lbp - m)
    return e[0:1] / jnp.sum(e, axis=0, keepdims=True)


def _decays(la, chunk):
    rows = la.shape[0]
    b = _cumsum_rows(la, chunk)
    n_chunks = rows // chunk
    lasts = [b[(c + 1) * chunk - 1:(c + 1) * chunk] for c in range(n_chunks)]
    if n_chunks == 1:
        b_last = lasts[0]
        b_last_rows = b_last
    else:
        b_last = jnp.concatenate(lasts, axis=0)
        b_last_rows = jnp.concatenate(
            [jnp.broadcast_to(l, (chunk, l.shape[1])) for l in lasts], axis=0)
    return jnp.exp(b), jnp.exp(-b), jnp.exp(b_last_rows - b), jnp.exp(b_last)


def _emit_keys_paired(refs, r, q_targets, kl, q, k, la):
    q32_ref, q64_ref, kin_ref, kx_ref, ko_ref, d_ref = refs
    c = PROMPT_CHUNK
    rows = slice(2 * c * r, 2 * c * (r + 1))
    e_b, e_nb, e_out, decay = _decays(la, c)
    d0, d1 = decay[0:1], decay[1:2]
    q32 = q * e_b
    k_in = k * e_nb
    k_out = k * e_out
    q32_ref[rows, kl] = q32.astype(bf16)
    q64_ref[c * r:c * (r + 1), kl] = (q32[c:] * d0).astype(bf16)
    kin_ref[c * r:c * (r + 1), kl] = k_in[:c].astype(bf16)
    kx_ref[rows, kl] = jnp.concatenate([k_out[:c], k_in[c:]], axis=0).astype(bf16)
    ko_ref[rows, kl] = jnp.concatenate([k_out[:c] * d1, k_out[c:]], axis=0).astype(bf16)
    d_ref[r:r + 1, kl] = d0 * d1


def _emit_keys_single(refs, r, q_targets, kl, q, k, la, *, chunk):
    q_ref, kin_ref, ko_ref, d_ref = refs
    rows = slice(SUBLANES * chunk * r, SUBLANES * chunk * (r + 1))
    e_b, e_nb, e_out, decay = _decays(la, chunk)
    q_in = q * e_b
    for ql, own in q_targets:
        q_ref[rows, ql] = (q_in if own is None else jnp.where(own, q_in, 0.0)).astype(bf16)
    kin_ref[rows, kl] = (k * e_nb).astype(bf16)
    ko_ref[rows, kl] = (k * e_out).astype(bf16)
    d_ref[SUBLANES * r:SUBLANES * (r + 1), kl] = decay


def _proj_kernel(x_ref, gm_ref, w_ref, w_a2_ref, b_a_ref, lbp_ref, *refs, chunk, paired, n_cast):
    cast_in, refs = refs[:n_cast], refs[n_cast:]
    n_key_refs = 6 if paired else 4
    key_refs = refs[:n_key_refs]
    v_ref, g_ref = refs[n_key_refs:n_key_refs + 2]
    cast_out = refs[n_key_refs + 2:n_key_refs + 2 + n_cast]
    p_refs = refs[n_key_refs + 2 + n_cast:]
    tm = x_ref.shape[0]
    block = 2 * chunk if paired else SUBLANES * chunk
    emit_keys = (_emit_keys_paired if paired
                 else functools.partial(_emit_keys_single, chunk=chunk))
    blocks = [(slice(r * block, (r + 1) * block), r) for r in range(tm // block)]
    h = _rmsnorm(x_ref[...], gm_ref[...]).astype(bf16)

    w_a2 = w_a2_ref[...]
    b_a = b_a_ref[...]
    lb = _lower_bound(lbp_ref)
    lane = lax.broadcasted_iota(jnp.int32, (block, HEAD), 1)

    def p_slot(lo, hi):
        for (g_lo, g_hi), ref in zip(PROJ_GROUPS, p_refs):
            if g_lo <= lo and hi <= g_hi:
                return ref, slice(lo - g_lo, hi - g_lo)
        raise ValueError((lo, hi))

    def project(lo, hi):
        ref, cols = p_slot(lo, hi)
        ref[:, cols] = jnp.dot(h, w_ref[:, lo:hi], preferred_element_type=f32)

    def p(rows, lo, width):
        ref, cols = p_slot(lo, lo + width)
        return ref[rows, cols]

    def gla_keys(rows, r):
        lra = p(rows, OFF_LR, HEAD)
        z = jnp.dot(lra.astype(bf16), w_a2, preferred_element_type=f32) + b_a
        for pr in range(H_A // 2):
            kl = slice(pr * HEAD, (pr + 1) * HEAD)
            zz = z[:, kl]
            la = (jnp.minimum(zz, 0.0) - jnp.log(1.0 + jnp.exp(-jnp.abs(zz)))) * (1.0 / GATE_NORM)
            q = p(rows, OFF_QA + pr * HEAD, HEAD) * (DK_A ** -0.5)
            k = p(rows, OFF_KA + pr * HEAD, HEAD)
            q_targets = [(slice(hd * HEAD, (hd + 1) * HEAD), own)
                         for hd, own in ((2 * pr, lane < DK_A), (2 * pr + 1, lane >= DK_A))]
            emit_keys(key_refs, r, q_targets, kl, q, k, la)

    def hgrn_keys(rows, r):
        for hd in range(N_HEADS - H_A):
            lbh = lb[:, hd * HEAD:(hd + 1) * HEAD]
            f = lbh + (1.0 - lbh) * _sigmoid(p(rows, OFF_FB + hd * HEAD, HEAD))
            q = _silu(p(rows, OFF_QB + hd * HEAD, HEAD))
            ql = slice(SEC + hd * HEAD, SEC + (hd + 1) * HEAD)
            kl = slice(KA_W + hd * HEAD, KA_W + (hd + 1) * HEAD)
            emit_keys(key_refs, r, [(ql, None)], kl, q, 1.0 - f, jnp.log(f))

    def gla_values(rows, crow):
        v_ref[rows, 0:SEC] = p(rows, OFF_VA, SEC).astype(v_ref.dtype)
        g_ref[rows, 0:SEC] = p(rows, OFF_GA, SEC)

    def hgrn_values(rows, crow):
        v_ref[rows, SEC:2 * SEC] = p(rows, OFF_IB, SEC).astype(v_ref.dtype)
        g_ref[rows, SEC:2 * SEC] = p(rows, OFF_GB, SEC)

    def tasks(fn, which):
        return [functools.partial(fn, *blocks[r]) for r in which]

    def cast(src, dst):
        dst[...] = src[...].astype(dst.dtype)

    nb = len(blocks)
    first, second = range(nb // 2), range(nb // 2, nb)
    windows = (
        [functools.partial(cast, s, d) for s, d in zip(cast_in, cast_out)],
        tasks(gla_keys, range(nb)),
        tasks(hgrn_keys, first),
        tasks(hgrn_keys, second) + tasks(gla_values, range(nb)),
    )
    for (lo, hi), work in zip(PROJ_GROUPS, windows):
        starts = list(range(lo, hi, MXU_N))
        for i, c0 in enumerate(starts):
            project(c0, min(c0 + MXU_N, hi))
            for t in work[i * len(work) // len(starts):(i + 1) * len(work) // len(starts)]:
                t()
    for t in tasks(hgrn_values, range(nb)):
        t()


def _slice_spec(shape, steps):
    rows, width = shape
    height = next(r for r in range(BF16_ROWS, rows + 1, BF16_ROWS)
                  if rows % r == 0 and rows // r <= steps)
    last = rows // height - 1
    return pl.BlockSpec((height, width), lambda i: (jnp.minimum(i, last), 0))


def _proj(x2d, g_mix, w_in_p, w_a2p, b_ap, lb_param, *, chunk, paired, cast=()):
    n = x2d.shape[0]
    tm = ROW_TILE
    steps = n // tm
    kern = functools.partial(_proj_kernel, chunk=chunk, paired=paired, n_cast=len(cast))

    def arr(rows, width, dtype):
        return (jax.ShapeDtypeStruct((n // rows, width), dtype),
                pl.BlockSpec((tm // rows, width), lambda i: (i, 0)))

    if paired:
        keys = [arr(1, K_W, bf16), arr(2, K_W, bf16), arr(2, K_W, bf16),
                arr(1, K_W, bf16), arr(1, K_W, bf16), arr(2 * chunk, K_W, f32)]
    else:
        keys = [arr(1, D_MODEL, bf16), arr(1, K_W, bf16), arr(1, K_W, bf16),
                arr(chunk, K_W, f32)]
    outs = keys + [arr(1, D_MODEL, bf16), arr(1, D_MODEL, f32)]
    cast_specs = [_slice_spec(w.shape, steps) for w in cast]
    outs += [(jax.ShapeDtypeStruct(w.shape, bf16), spec) for w, spec in zip(cast, cast_specs)]

    return pl.pallas_call(
        kern,
        out_shape=tuple(o[0] for o in outs),
        grid=(steps,),
        in_specs=[
            pl.BlockSpec((tm, D_MODEL), lambda i: (i, 0)),
            _const_spec((1, D_MODEL)),
            _const_spec((D_MODEL, D_P)),
            _const_spec(w_a2p.shape),
            _const_spec(b_ap.shape),
            _const_spec(lb_param.shape),
        ] + cast_specs,
        out_specs=tuple(o[1] for o in outs),
        scratch_shapes=[pltpu.VMEM((tm, hi - lo), f32) for lo, hi in PROJ_GROUPS],
        compiler_params=pltpu.CompilerParams(
            dimension_semantics=("arbitrary",), vmem_limit_bytes=VMEM_LIMIT),
        name="proj",
    )(x2d, g_mix, w_in_p, w_a2p, b_ap, lb_param, *cast)


def _head_norm(o, onorm):
    return o * lax.rsqrt(jnp.mean(o * o, axis=-1, keepdims=True) + EPS) * onorm


def _tril(n):
    r = lax.broadcasted_iota(jnp.int32, (n, n), 0)
    c = lax.broadcasted_iota(jnp.int32, (n, n), 1)
    return r >= c


def _mixer_prompt_kernel(q32_ref, q64_ref, kin_ref, kx_ref, ko_ref, d_ref, v_ref, g_ref,
                         ona_ref, onb_ref, o_ref, sa_ref, sb_ref,
                         st_ref, att_ref, up_ref, start_ref, *, tile, chunk):
    j = pl.program_id(1)

    @pl.when(j == 0)
    def _():
        st_ref[...] = jnp.zeros_like(st_ref)

    onorm = [ona_ref[...]] * H_A + [onb_ref[...]] * (N_HEADS - H_A)
    pair = 2 * chunk
    n_pairs = tile // pair
    stack = max(len(heads) for heads in _KEY_SHARING) * chunk
    r = lax.broadcasted_iota(jnp.int32, (stack, pair), 0) % chunk
    c = lax.broadcasted_iota(jnp.int32, (stack, pair), 1)
    tril = (r >= c)[:, :chunk]
    cross = (c < chunk) | (r >= c - chunk)
    lane = lax.broadcasted_iota(jnp.int32, (chunk, HEAD), 1)
    low = jnp.where(lane < DK_A, 1.0, 0.0).astype(bf16)
    own = (low > 0, low == 0)

    def queries(ref, rows, h):
        q = ref[rows, _key_lanes(h):_key_lanes(h) + HEAD]
        return jnp.where(own[h % 2], q, jnp.zeros_like(q)) if h < H_A else q

    for heads in _KEY_SHARING:
        n = len(heads)
        kl = slice(_key_lanes(heads[0]), _key_lanes(heads[0]) + HEAD)
        vl = slice(heads[0] * HEAD, (heads[-1] + 1) * HEAD)
        for s in range(n_pairs):
            ra = slice(s * pair, s * pair + chunk)
            rb = slice(s * pair + chunk, (s + 1) * pair)
            rp = slice(s * pair, (s + 1) * pair)
            qa = [queries(q32_ref, ra, h) for h in heads]
            qb = [queries(q32_ref, rb, h) for h in heads]
            qa = qa[0] if n == 1 else jnp.concatenate(qa, axis=0)
            qb = qb[0] if n == 1 else jnp.concatenate(qb, axis=0)
            a_keys = kin_ref[s * chunk:(s + 1) * chunk, kl]
            att0 = jnp.where(tril[:n * chunk], _dot_nt(qa, a_keys), 0.0).astype(bf16)
            att1 = jnp.where(cross[:n * chunk], _dot_nt(qb, kx_ref[rp, kl]), 0.0).astype(bf16)
            up = _dot_tn(v_ref[rp, vl], ko_ref[rp, kl])
            att0 = jnp.concatenate([att0, jnp.zeros_like(att0)], axis=1)
            for i, h in enumerate(heads):
                u = h * n_pairs + s
                att_ref[u, :chunk] = att0[i * chunk:(i + 1) * chunk]
                att_ref[u, chunk:] = att1[i * chunk:(i + 1) * chunk]
                up_ref[u] = up[i * HEAD:(i + 1) * HEAD]

    for h in range(N_HEADS):
        kl = slice(_key_lanes(h), _key_lanes(h) + HEAD)
        st = st_ref[h]
        for s in range(n_pairs):
            u = h * n_pairs + s
            start_ref[u] = st.T.astype(bf16)
            st = d_ref[s:s + 1, kl] * st + up_ref[u]
        st_ref[h] = st

    for h in range(N_HEADS):
        sl = slice(h * HEAD, (h + 1) * HEAD)
        outs = []
        for s in range(n_pairs):
            u = h * n_pairs + s
            ra = slice(s * pair, s * pair + chunk)
            rp = slice(s * pair, (s + 1) * pair)
            q_pair = jnp.concatenate(
                [queries(q32_ref, ra, h),
                 queries(q64_ref, slice(s * chunk, (s + 1) * chunk), h)], axis=0)
            outs.append(jnp.dot(q_pair, start_ref[u], preferred_element_type=f32)
                        + jnp.dot(att_ref[u], v_ref[rp, sl], preferred_element_type=f32))
        o = jnp.concatenate(outs, axis=0)
        o_ref[:, sl] = (_head_norm(o, onorm[h]) * _silu(g_ref[:, sl])).astype(o_ref.dtype)

    @pl.when(j == pl.num_programs(1) - 1)
    def _():
        for h in range(H_A):
            sa_ref[0, h] = st_ref[h].T[_own_keys(h), :]
            sb_ref[0, h] = st_ref[H_A + h].T


def _mixer_prompt(acts, ona, onb, batch, seq):
    tile = MIXER_TILE
    nt = seq // tile
    chunk = PROMPT_CHUNK
    units = N_HEADS * (tile // (2 * chunk))
    kern = functools.partial(_mixer_prompt_kernel, tile=tile, chunk=chunk)

    def rows(width, per=1):
        return pl.BlockSpec((tile // per, width), lambda b, j: (b * nt + j, 0))

    return pl.pallas_call(
        kern,
        out_shape=(
            jax.ShapeDtypeStruct((batch * seq, D_MODEL), bf16),
            jax.ShapeDtypeStruct((batch, H_A, DK_A, HEAD), f32),
            jax.ShapeDtypeStruct((batch, H_A, HEAD, HEAD), f32),
        ),
        grid=(batch, nt),
        in_specs=[
            rows(K_W), rows(K_W, 2), rows(K_W, 2), rows(K_W), rows(K_W),
            rows(K_W, 2 * chunk), rows(D_MODEL), rows(D_MODEL),
            _const_spec(ona.shape),
            _const_spec(onb.shape),
        ],
        out_specs=(
            rows(D_MODEL),
            pl.BlockSpec((1, H_A, DK_A, HEAD), lambda b, j: (b, 0, 0, 0)),
            pl.BlockSpec((1, H_A, HEAD, HEAD), lambda b, j: (b, 0, 0, 0)),
        ),
        scratch_shapes=[
            pltpu.VMEM((N_HEADS, HEAD, HEAD), f32),
            pltpu.VMEM((units, 2 * chunk, 2 * chunk), bf16),
            pltpu.VMEM((units, HEAD, HEAD), f32),
            pltpu.VMEM((units, HEAD, HEAD), bf16),
        ],
        compiler_params=pltpu.CompilerParams(
            dimension_semantics=("arbitrary", "arbitrary"), vmem_limit_bytes=VMEM_LIMIT),
        name="mixer_prompt",
    )(*acts, ona, onb)


def _mixer_sample_kernel(q16_ref, k16_ref, ko16_ref, v16_ref, g_ref, d_ref, sa_in_ref, sb_in_ref,
                         ona_ref, onb_ref, o_ref, sa_ref, sb_ref, att_ref, oi_ref,
                         qi_ref, ki_ref, ko_ref, v_ref, *, bt, seq):
    for src, dst in ((q16_ref, qi_ref), (k16_ref, ki_ref), (ko16_ref, ko_ref), (v16_ref, v_ref)):
        dst[...] = src[...].astype(f32)

    onorm = [ona_ref[...]] * H_A + [onb_ref[...]] * (N_HEADS - H_A)
    tril = _tril(seq)
    zpad = jnp.zeros((HEAD - DK_A, HEAD), f32)
    zeros = jnp.zeros((seq, HEAD), f32)
    row = lax.broadcasted_iota(jnp.int32, (seq, HEAD), 0)
    ones_rows = jnp.where(row < 2, 1.0, 0.0)
    rhs_bottom = jnp.concatenate([zeros, ones_rows], axis=1)

    for bi in range(bt):
        rows = slice(bi * seq, (bi + 1) * seq)
        decay = d_ref[bi:bi + 1, :]
        d_hi = decay.astype(bf16).astype(f32)
        d_lo = decay - d_hi
        for h in range(N_HEADS):
            sl = slice(h * HEAD, (h + 1) * HEAD)
            kl = slice(_key_lanes(h), _key_lanes(h) + HEAD)
            q_in = qi_ref[rows, sl].astype(bf16)
            k_in = ki_ref[rows, kl].astype(bf16)
            v32 = v_ref[rows, sl]
            if h >= H_A:
                s = sb_in_ref[bi, h - H_A]
            elif h % 2 == 0:
                s = jnp.concatenate([sa_in_ref[bi, h], zpad], axis=0)
            else:
                s = jnp.concatenate([zpad, sa_in_ref[bi, h]], axis=0)
            att_ref[bi * N_HEADS + h] = jnp.where(tril, _dot_nt(q_in, k_in), 0.0)
            oi_ref[rows, sl] = jnp.dot(q_in, s.astype(bf16), preferred_element_type=f32)
            d_rows = jnp.where(row == 0, d_hi[:, kl], jnp.where(row == 1, d_lo[:, kl], 0.0))
            lhs = jnp.concatenate([ko_ref[rows, kl], d_rows], axis=0).astype(bf16)
            rhs = jnp.concatenate(
                [jnp.concatenate([v32, zeros], axis=1), rhs_bottom], axis=0).astype(bf16)
            upd = _dot_tn(lhs, rhs)
            s_new = upd[:, HEAD:] * s + upd[:, :HEAD]
            if h < H_A:
                sa_ref[bi, h] = s_new[_own_keys(h), :]
            else:
                sb_ref[bi, h - H_A] = s_new

    for bi in range(bt):
        rows = slice(bi * seq, (bi + 1) * seq)
        for h in range(N_HEADS):
            sl = slice(h * HEAD, (h + 1) * HEAD)
            att = att_ref[bi * N_HEADS + h].astype(bf16)
            o = oi_ref[rows, sl] + jnp.dot(att, v_ref[rows, sl].astype(bf16),
                                           preferred_element_type=f32)
            o_ref[rows, sl] = (_head_norm(o, onorm[h]) * _silu(g_ref[rows, sl])).astype(
                o_ref.dtype)


def _mixer_sample(acts, sa_in, sb_in, ona, onb, batch, seq):
    bt = SAMPLE_SEQS
    kern = functools.partial(_mixer_sample_kernel, bt=bt, seq=seq)

    def rows(width):
        return pl.BlockSpec((bt * seq, width), lambda i: (i, 0))

    sa_spec = pl.BlockSpec((bt, H_A, DK_A, HEAD), lambda i: (i, 0, 0, 0))
    sb_spec = pl.BlockSpec((bt, H_A, HEAD, HEAD), lambda i: (i, 0, 0, 0))
    return pl.pallas_call(
        kern,
        out_shape=(
            jax.ShapeDtypeStruct((batch * seq, D_MODEL), f32),
            jax.ShapeDtypeStruct((batch, H_A, DK_A, HEAD), f32),
            jax.ShapeDtypeStruct((batch, H_A, HEAD, HEAD), f32),
        ),
        grid=(batch // bt,),
        in_specs=[
            rows(D_MODEL), rows(K_W), rows(K_W), rows(D_MODEL), rows(D_MODEL),
            pl.BlockSpec((bt, K_W), lambda i: (i, 0)),
            sa_spec, sb_spec,
            _const_spec(ona.shape),
            _const_spec(onb.shape),
        ],
        out_specs=(rows(D_MODEL), sa_spec, sb_spec),
        scratch_shapes=[
            pltpu.VMEM((bt * N_HEADS, seq, seq), f32),
            pltpu.VMEM((bt * seq, D_MODEL), f32),
            pltpu.VMEM((bt * seq, D_MODEL), f32),
            pltpu.VMEM((bt * seq, K_W), f32),
            pltpu.VMEM((bt * seq, K_W), f32),
            pltpu.VMEM((bt * seq, D_MODEL), f32),
        ],
        compiler_params=pltpu.CompilerParams(
            dimension_semantics=("arbitrary",), vmem_limit_bytes=VMEM_LIMIT),
        name="mixer_sample",
    )(*acts, sa_in, sb_in, ona, onb)


def _out_kernel(o_ref, x_ref, w_o_ref, gf_ref, w1_ref, w3_ref, w2_ref, gl_ref, y_ref):
    x1 = x_ref[...] + jnp.dot(o_ref[...].astype(bf16), w_o_ref[...], preferred_element_type=f32)
    h = _rmsnorm(x1, gf_ref[...]).astype(bf16)
    acc = x1
    for c in range(D_FF // FF_CHUNK):
        cols = slice(c * FF_CHUNK, (c + 1) * FF_CHUNK)
        a = jnp.dot(h, w1_ref[:, cols], preferred_element_type=f32)
        b = jnp.dot(h, w3_ref[:, cols], preferred_element_type=f32)
        u = (_silu(a) * b).astype(bf16)
        acc = acc + jnp.dot(u, w2_ref[cols, :], preferred_element_type=f32)
    y_ref[...] = _rmsnorm(acc, gl_ref[...])


def _out(o2d, x2d, w_o, g_ffn, w1, w3, w2, g_final):
    n = x2d.shape[0]
    tm = min(OUT_TILE, n // 2)
    row_spec = pl.BlockSpec((tm, D_MODEL), lambda i: (i, 0))
    return pl.pallas_call(
        _out_kernel,
        out_shape=jax.ShapeDtypeStruct((n, D_MODEL), f32),
        grid=(n // tm,),
        in_specs=[
            row_spec, row_spec,
            _const_spec((D_MODEL, D_MODEL)),
            _const_spec((1, D_MODEL)),
            _const_spec((D_MODEL, D_FF)),
            _const_spec((D_MODEL, D_FF)),
            _const_spec((D_FF, D_MODEL)),
            _const_spec((1, D_MODEL)),
        ],
        out_specs=row_spec,
        compiler_params=pltpu.CompilerParams(
            dimension_semantics=("arbitrary",), vmem_limit_bytes=VMEM_LIMIT),
        name="out_ffn",
    )(o2d, x2d, w_o, g_ffn, w1, w3, w2, g_final)


def _w_in_sections():
    sections, src = [], 0
    for width, dst in ((KA_W, OFF_QA), (KA_W, OFF_KA), (SEC, OFF_VA), (SEC, OFF_GA),
                       (GATE_RANK, OFF_LR), (SEC, OFF_QB), (SEC, OFF_FB), (SEC, OFF_IB),
                       (SEC, OFF_GB)):
        sections.append((src, width, dst))
        src += width
    return tuple(sections), src


_W_IN_SECTIONS, D_IN = _w_in_sections()


def _w_in_layout_kernel(wt_ref, o_ref):
    for src, width, dst in _W_IN_SECTIONS:
        part = wt_ref[src:src + width, :]
        if width % HEAD:
            pad = jnp.zeros((HEAD - width % HEAD, part.shape[1]), part.dtype)
            part = jnp.concatenate([part, pad], axis=0)
        o_ref[:, dst:dst + part.shape[0]] = part.T.astype(o_ref.dtype)


def _layout_w_in(w_in_t, cols=256):
    return pl.pallas_call(
        _w_in_layout_kernel,
        out_shape=jax.ShapeDtypeStruct((D_MODEL, D_P), bf16),
        grid=(D_MODEL // cols,),
        in_specs=[pl.BlockSpec((D_IN, cols), lambda i: (0, i))],
        out_specs=pl.BlockSpec((cols, D_P), lambda i: (i, 0)),
        compiler_params=pltpu.CompilerParams(
            dimension_semantics=("arbitrary",), vmem_limit_bytes=VMEM_LIMIT),
        name="w_in_layout",
    )(w_in_t)


def kernel(x_prompt, x_sample, state_gla, state_hgrn, norm_mix, w_in, w_a2, b_a, lb_param,
           onorm_a, onorm_b, w_o, norm_ffn, w1, w3, w2, norm_final):
    batch, seq, _ = x_prompt.shape
    dec_batch, dec_seq, _ = x_sample.shape

    w_in_p = _layout_w_in(jnp.swapaxes(w_in[0], 0, 1))
    w_a2p = jnp.pad(w_a2[0], ((0, HEAD - GATE_RANK), (0, 0))).astype(bf16)
    b_ap = b_a[0].reshape(1, KA_W)
    g_mix = norm_mix[0].reshape(1, D_MODEL)
    g_ffn = norm_ffn[0].reshape(1, D_MODEL)
    g_fin = norm_final.reshape(1, D_MODEL)
    ona = onorm_a[0].reshape(1, HEAD)
    onb = onorm_b[0].reshape(1, HEAD)

    xp = x_prompt.reshape(batch * seq, D_MODEL)
    xs = x_sample.reshape(dec_batch * dec_seq, D_MODEL)

    later = (w_o[0], w1[0], w3[0], w2[0])
    *acts_p, w_o_b, w1_b, w3_b, w2_b = _proj(xp, g_mix, w_in_p, w_a2p, b_ap, lb_param,
                                             chunk=PROMPT_CHUNK, paired=True, cast=later)
    op, sa_p, sb_p = _mixer_prompt(acts_p, ona, onb, batch, seq)
    yp = _out(op, xp, w_o_b, g_ffn, w1_b, w3_b, w2_b, g_fin)

    q_s, kin_s, ko_s, d_s, v_s, g_s = _proj(xs, g_mix, w_in_p, w_a2p, b_ap, lb_param,
                                            chunk=dec_seq, paired=False)
    os_, sa_s, sb_s = _mixer_sample((q_s, kin_s, ko_s, v_s, g_s, d_s), state_gla[0],
                                    state_hgrn[0], ona, onb, dec_batch, dec_seq)
    ys = _out(os_, xs, w_o_b, g_ffn, w1_b, w3_b, w2_b, g_fin)

    return (yp.reshape(batch, seq, D_MODEL), ys.reshape(dec_batch, dec_seq, D_MODEL),
            sa_p[None], sb_p[None], sa_s[None], sb_s[None])
```
